```python
import math
import jax, jax.numpy as jnp
from jax import lax
import numpy as np

D_MODEL = 1024
BATCH = 16
SEQ = 4096
DEPTH = 4

CHUNK = 64
D_MIX = D_MODEL
D_HGRN = D_MIX // 2
HGRN_HEADS = 4
HGRN_HEAD_DIM = D_HGRN // HGRN_HEADS
D_SB = D_MIX - D_HGRN
SB_HEADS = 8
SB_HEAD_DIM = D_SB // SB_HEADS
Q_BLOCK = 128
IN_SPLITS = (D_HGRN, 2 * D_HGRN, 3 * D_HGRN, 4 * D_HGRN, 4 * D_HGRN + D_SB, 4 * D_HGRN + 2 * D_SB)
D_IN_PROJ = 4 * D_HGRN + 3 * D_SB
D_FF_DENSE = 2816
N_EXPERTS = 8
TOP_K = 2
D_FF_EXPERT = 3584
N_DENSE = (DEPTH + 1) // 2
N_MOE = DEPTH // 2
DEEPNORM_ALPHA = (2 * DEPTH) ** 0.25
DEEPNORM_BETA = (8 * DEPTH) ** -0.25
LN_EPS = 1e-5
RMS_EPS = 1e-6

kernel_name = "hybrid_hgrn2_stickbreaking_moe_deepnorm"


def layer_norm(x, gain, bias):
    xf = x.astype(jnp.float32)
    mu = jnp.mean(xf, axis=-1, keepdims=True)
    var = jnp.mean(jnp.square(xf - mu), axis=-1, keepdims=True)
    return ((xf - mu) * lax.rsqrt(var + LN_EPS)).astype(x.dtype) * gain + bias


def ada_modulation(c, w, b):
    m = (jax.nn.silu(c) @ w + b)[:, None, :]
    return jnp.split(m, 3, axis=-1)


def hgrn_lower_bounds(lb_logits):
    p = jax.nn.softmax(lb_logits.astype(jnp.float32), axis=0)
    return (jnp.cumsum(p, axis=0) - p[0:1]).astype(lb_logits.dtype)


def hgrn2_mixer(q, f_logit, i, g, lb, norm_gain):
    B, S, _ = q.shape
    n_chunks = S // CHUNK
    f = lb + (1.0 - lb) * jax.nn.sigmoid(f_logit)
    log_f = jnp.log(f)
    k = 1.0 - f
    q = jax.nn.silu(q)

    def to_chunks(t):
        return t.reshape(B, n_chunks, CHUNK, HGRN_HEADS, HGRN_HEAD_DIM).transpose(1, 0, 3, 2, 4)

    causal = jnp.tril(jnp.ones((CHUNK, CHUNK), dtype=bool))[:, :, None]

    def step(state, inp):
        qc, gc, kc, vc = inp
        b = jnp.cumsum(gc, axis=2)
        o_inter = jnp.einsum('bhck,bhkv->bhcv', qc * jnp.exp(b), state)
        diff = b[:, :, :, None, :] - b[:, :, None, :, :]
        decay = jnp.exp(jnp.where(causal, diff, -jnp.inf))
        scores = jnp.sum(qc[:, :, :, None, :] * decay * kc[:, :, None, :, :], axis=-1)
        o_intra = jnp.einsum('bhts,bhsv->bhtv', scores, vc)
        b_last = b[:, :, -1:, :]
        k_dec = kc * jnp.exp(b_last - b)
        state = jnp.exp(b_last)[:, :, 0, :, None] * state + jnp.einsum('bhsk,bhsv->bhkv', k_dec, vc)
        return state, o_inter + o_intra

    state0 = jnp.zeros((B, HGRN_HEADS, HGRN_HEAD_DIM, HGRN_HEAD_DIM), dtype=q.dtype)
    _, o = lax.scan(step, state0, (to_chunks(q), to_chunks(log_f), to_chunks(k), to_chunks(i)))
    o = o.transpose(1, 0, 3, 2, 4).reshape(B, S, HGRN_HEADS, HGRN_HEAD_DIM)
    of = o.astype(jnp.float32)
    o = (of * lax.rsqrt(jnp.mean(jnp.square(of), axis=-1, keepdims=True) + RMS_EPS)).astype(q.dtype)
    return o.reshape(B, S, D_HGRN) * norm_gain * jax.nn.silu(g)


def stick_breaking_mixer(q, k, v):
    B, S, _ = q.shape

    def heads(t):
        return t.reshape(B, S, SB_HEADS, SB_HEAD_DIM).transpose(0, 2, 1, 3)

    q, k, v = heads(q), heads(k), heads(v)
    scale = SB_HEAD_DIM ** -0.5
    outs = []
    for start in range(0, S, Q_BLOCK):
        end = start + Q_BLOCK
        qb, kb, vb = q[:, :, start:end], k[:, :, :end], v[:, :, :end]
        z = jnp.einsum('bhqd,bhkd->bhqk', qb, kb).astype(jnp.float32) * scale
        t_pos = start + jnp.arange(Q_BLOCK)[:, None]
        s_pos = jnp.arange(end)[None, :]
        strict = s_pos < t_pos
        log_not = jnp.where(strict, jax.nn.log_sigmoid(-z), 0.0)
        after = lax.cumsum(log_not, axis=3, reverse=True) - log_not
        a = jnp.where(strict, jnp.exp(jax.nn.log_sigmoid(z) + after), 0.0)
        outs.append(jnp.einsum('bhqk,bhkd->bhqd', a.astype(vb.dtype), vb))
    o = jnp.concatenate(outs, axis=2)
    return o.transpose(0, 2, 1, 3).reshape(B, S, D_SB)


def token_mixer(h, w_in, w_out, lb, hgrn_norm_gain):
    proj = h @ w_in
    hq, hf, hi, hg, sq, sk, sv = jnp.split(proj, IN_SPLITS, axis=-1)
    o_hgrn = hgrn2_mixer(hq, hf, hi, hg, lb, hgrn_norm_gain)
    o_sb = stick_breaking_mixer(sq, sk, sv)
    return jnp.concatenate([o_hgrn, o_sb], axis=-1) @ w_out


def swiglu(h, wg, wu, wd):
    return (jax.nn.silu(h @ wg) * (h @ wu)) @ wd


def moe_swiglu(h, w_router, wg, wu, wd):
    logits = (h @ w_router).astype(jnp.float32)
    top_v, top_i = lax.top_k(logits, TOP_K)
    top_w = jax.nn.softmax(top_v, axis=-1).astype(h.dtype)
    combine = jnp.sum(jax.nn.one_hot(top_i, N_EXPERTS, dtype=h.dtype) * top_w[..., None], axis=-2)
    y = jnp.zeros_like(h)
    for e in range(N_EXPERTS):
        y = y + combine[..., e:e + 1] * swiglu(h, wg[e], wu[e], wd[e])
    return y


def setup_inputs(seed: int = 0) -> dict:
    key = jax.random.key(seed)
    ks = jax.random.split(key, 18)
    f32 = jnp.float32
    beta = DEEPNORM_BETA
    x = jax.random.normal(ks[0], (BATCH, SEQ, D_MODEL), f32)
    c = jax.random.normal(ks[1], (BATCH, D_MODEL), f32)
    w_ada = jax.random.normal(ks[2], (DEPTH, 2, D_MODEL, 3 * D_MODEL), f32) * (0.5 * D_MODEL ** -0.5)
    b_ada = jax.random.normal(ks[3], (DEPTH, 2, 3 * D_MODEL), f32) * 0.02
    col_scale = jnp.concatenate([
        jnp.ones((2 * D_HGRN,), f32), jnp.full((D_HGRN,), beta, f32),
        jnp.ones((D_HGRN + 2 * D_SB,), f32), jnp.full((D_SB,), beta, f32)])
    w_in = jax.random.normal(ks[4], (DEPTH, D_MODEL, D_IN_PROJ), f32) * (D_MODEL ** -0.5) * col_scale
    w_out = jax.random.normal(ks[5], (DEPTH, D_MIX, D_MODEL), f32) * (D_MIX ** -0.5) * beta
    hgrn_lb_logits = jax.random.normal(ks[6], (DEPTH, D_HGRN), f32)
    hgrn_norm_gain = 1.0 + 0.02 * jax.random.normal(ks[7], (DEPTH, D_HGRN), f32)
    w_dense_gate = jax.random.normal(ks[8], (N_DENSE, D_MODEL, D_FF_DENSE), f32) * (D_MODEL ** -0.5) * beta
    w_dense_up = jax.random.normal(ks[9], (N_DENSE, D_MODEL, D_FF_DENSE), f32) * (D_MODEL ** -0.5) * beta
    w_dense_down = jax.random.normal(ks[10], (N_DENSE, D_FF_DENSE, D_MODEL), f32) * (D_FF_DENSE ** -0.5) * beta
    w_router = jax.random.normal(ks[11], (N_MOE, D_MODEL, N_EXPERTS), f32) * (D_MODEL ** -0.5)
    w_moe_gate = jax.random.normal(ks[12], (N_MOE, N_EXPERTS, D_MODEL, D_FF_EXPERT), f32) * (D_MODEL ** -0.5) * beta
    w_moe_up = jax.random.normal(ks[13], (N_MOE, N_EXPERTS, D_MODEL, D_FF_EXPERT), f32) * (D_MODEL ** -0.5) * beta
    w_moe_down = jax.random.normal(ks[14], (N_MOE, N_EXPERTS, D_FF_EXPERT, D_MODEL), f32) * (D_FF_EXPERT ** -0.5) * beta
    ln_gain = 1.0 + 0.02 * jax.random.normal(ks[15], (DEPTH, 2, D_MODEL), f32)
    ln_bias = 0.02 * jax.random.normal(ks[16], (DEPTH, 2, D_MODEL), f32)
    return {"x": x, "c": c, "w_ada": w_ada, "b_ada": b_ada, "w_in": w_in, "w_out": w_out,
            "hgrn_lb_logits": hgrn_lb_logits, "hgrn_norm_gain": hgrn_norm_gain,
            "w_dense_gate": w_dense_gate, "w_dense_up": w_dense_up, "w_dense_down": w_dense_down,
            "w_router": w_router, "w_moe_gate": w_moe_gate, "w_moe_up": w_moe_up, "w_moe_down": w_moe_down,
            "ln_gain": ln_gain, "ln_bias": ln_bias}


def reference(x, c, w_ada, b_ada, w_in, w_out, hgrn_lb_logits, hgrn_norm_gain,
              w_dense_gate, w_dense_up, w_dense_down, w_router, w_moe_gate, w_moe_up, w_moe_down,
              ln_gain, ln_bias):
    lb_all = hgrn_lower_bounds(hgrn_lb_logits)
    for layer in range(DEPTH):
        shift, scale, gate = ada_modulation(c, w_ada[layer, 0], b_ada[layer, 0])
        h = x * (1.0 + scale) + shift
        y = token_mixer(h, w_in[layer], w_out[layer], lb_all[layer], hgrn_norm_gain[layer])
        x = layer_norm(DEEPNORM_ALPHA * x + gate * y, ln_gain[layer, 0], ln_bias[layer, 0])
        shift, scale, gate = ada_modulation(c, w_ada[layer, 1], b_ada[layer, 1])
        h = x * (1.0 + scale) + shift
        if layer % 2 == 0:
            j = layer // 2
            y = swiglu(h, w_dense_gate[j], w_dense_up[j], w_dense_down[j])
        else:
            j = layer // 2
            y = moe_swiglu(h, w_router[j], w_moe_gate[j], w_moe_up[j], w_moe_down[j])
        x = layer_norm(DEEPNORM_ALPHA * x + gate * y, ln_gain[layer, 1], ln_bias[layer, 1])
    return x
```

```python
import functools
import math

import jax
import jax.numpy as jnp
from jax import lax
from jax.experimental import pallas as pl
from jax.experimental.pallas import tpu as pltpu

D_MODEL = 1024
D_HGRN = 512
HGRN_HEADS = 4
HGRN_DH = 128
D_SB = 512
SB_HEADS = 8
SB_DH = 64
D_IN_PROJ = 4 * D_HGRN + 3 * D_SB
N_EXPERTS = 8
LN_EPS = 1e-5
RMS_EPS = 1e-6

LANES = 128
VMEM_LIMIT = 52 * 1024 * 1024

HG_UNIT = 128
HG_BAND = 8
HG_LEVELS = (64, 32, 16, 8)
SB_BLOCK = 128
SB_SKIP = 110.0
MOE_TILE = 1024
MOE_FF_TILE = 512

F32 = jnp.float32
BF16 = jnp.bfloat16


def _cparams(sem):
    return pltpu.CompilerParams(dimension_semantics=sem, vmem_limit_bytes=VMEM_LIMIT)


def _dot(a, b):
    return jnp.dot(a, b, preferred_element_type=F32)


def _dot_nt(a, b):
    return lax.dot_general(a, b, (((1,), (1,)), ((), ())), preferred_element_type=F32)


def _dot_tn(a, b):
    return lax.dot_general(a, b, (((0,), (0,)), ((), ())), preferred_element_type=F32)


def _sigmoid(x):
    return 1.0 / (1.0 + jnp.exp(-x))


def _silu(x):
    return x * _sigmoid(x)


def _ada_kernel(c_ref, w_ref, b_ref, o_ref):
    sc = _silu(c_ref[...])
    o_ref[0] = jnp.dot(sc, w_ref[0], preferred_element_type=F32,
                       precision=lax.Precision.HIGHEST) + b_ref[0]


def _ada_all(c, w_ada, b_ada):
    n_sub = w_ada.shape[0] * w_ada.shape[1]
    bsz, d = c.shape
    w = w_ada.reshape(n_sub, d, 3 * d)
    b = b_ada.reshape(n_sub, 1, 3 * d)
    tn = 1024
    return pl.pallas_call(
        _ada_kernel,
        grid=(n_sub, 3 * d // tn),
        in_specs=[pl.BlockSpec((bsz, d), lambda s, j: (0, 0)),
                  pl.BlockSpec((1, d, tn), lambda s, j: (s, 0, j)),
                  pl.BlockSpec((1, 1, tn), lambda s, j: (s, 0, j))],
        out_specs=pl.BlockSpec((1, bsz, tn), lambda s, j: (s, 0, j)),
        out_shape=jax.ShapeDtypeStruct((n_sub, bsz, 3 * d), F32),
        compiler_params=_cparams(("parallel", "parallel")),
        name="ada_modulation",
    )(c, w, b)


def _lb_kernel(l_ref, o_ref):
    x = l_ref[...]
    m = jnp.max(x, axis=0, keepdims=True)
    e = jnp.exp(x - m)
    p = e / jnp.sum(e, axis=0, keepdims=True)
    depth = x.shape[0]
    run = jnp.zeros_like(p[0:1])
    rows = []
    for i in range(depth):
        run = run + p[i:i + 1]
        rows.append(run - p[0:1])
    o_ref[...] = jnp.concatenate(rows, axis=0)


def _lower_bounds(lb_logits):
    return pl.pallas_call(
        _lb_kernel,
        out_shape=jax.ShapeDtypeStruct(lb_logits.shape, F32),
        name="hgrn_lower_bounds",
    )(lb_logits)


def _modulate_kernel(x_ref, m_ref, h_ref):
    m = m_ref[0]
    h_ref[0] = (x_ref[0] * (1.0 + m[1:2]) + m[0:1]).astype(BF16)


def _modulate(x, mod):
    bsz, s, d = x.shape
    ts = min(s, 1024)
    return pl.pallas_call(
        _modulate_kernel,
        grid=(bsz, s // ts),
        in_specs=[pl.BlockSpec((1, ts, d), lambda b, i: (b, i, 0)),
                  pl.BlockSpec((1, 3, d), lambda b, i: (b, 0, 0))],
        out_specs=pl.BlockSpec((1, ts, d), lambda b, i: (b, i, 0)),
        out_shape=jax.ShapeDtypeStruct((bsz, s, d), BF16),
        compiler_params=_cparams(("parallel", "parallel")),
        name="modulate_first",
    )(x, mod)


def _inproj_kernel(h_ref, w_ref, o_ref):
    h = h_ref[...]
    n = w_ref.shape[1]
    tn = 512
    for j in range(n // tn):
        o_ref[:, j * tn:(j + 1) * tn] = _dot(h, w_ref[:, j * tn:(j + 1) * tn]).astype(BF16)


def _in_proj(h2d, w_in):
    n, d = h2d.shape
    dout = w_in.shape[1]
    tm = min(n, 512)
    return pl.pallas_call(
        _inproj_kernel,
        grid=(n // tm,),
        in_specs=[pl.BlockSpec((tm, d), lambda i: (i, 0)),
                  pl.BlockSpec((d, dout), lambda i: (0, 0), pipeline_mode=pl.Buffered(1))],
        out_specs=pl.BlockSpec((tm, dout), lambda i: (i, 0)),
        out_shape=jax.ShapeDtypeStruct((n, dout), BF16),
        compiler_params=_cparams(("parallel",)),
        name="in_proj",
    )(h2d, w_in)


def _hgrn_consts():
    t = jnp.arange(HG_UNIT)[:, None]
    s = jnp.arange(HG_UNIT)[None, :]
    tril = (s <= t).astype(F32)
    lv = jnp.full((HG_UNIT, HG_UNIT), -1, jnp.int32)
    for idx, m in enumerate(HG_LEVELS):
        blk = 2 * m
        hit = (t // blk == s // blk) & (t % blk >= m) & (s % blk < m)
        lv = jnp.where(hit, idx, lv)
    return tril, lv


def _hgrn_kernel(q_ref, f_ref, i_ref, g_ref, lb_ref, gain_ref, tril_ref, lv_ref, o_ref,
                 state_ref, kpad_ref, bpad_ref, vpad_ref):
    @pl.when(pl.program_id(2) == 0)
    def _():
        state_ref[...] = jnp.zeros_like(state_ref)

    zpad = jnp.zeros((HG_BAND, HGRN_DH), F32)
    kpad_ref[0:HG_BAND, :] = zpad
    bpad_ref[0:HG_BAND, :] = zpad
    vpad_ref[0:HG_BAND, :] = zpad

    lb = lb_ref[...]
    gain = gain_ref[...]
    n_units = q_ref.shape[1] // HG_UNIT

    def unit(u, carry):
        r0 = pl.multiple_of(u * HG_UNIT, HG_UNIT)
        rows = pl.ds(r0, HG_UNIT)
        qr = q_ref[0, rows, :].astype(F32)
        fl = f_ref[0, rows, :].astype(F32)
        v = i_ref[0, rows, :].astype(F32)
        gg = g_ref[0, rows, :].astype(F32)

        f = lb + (1.0 - lb) * _sigmoid(fl)
        logf = jnp.log(f)
        kk = 1.0 - f
        qq = _silu(qr)
        b = jnp.dot(tril_ref[...], logf, preferred_element_type=F32,
                    precision=lax.Precision.HIGHEST)
        vb = v.astype(BF16)

        state = state_ref[...]
        o = _dot_nt((qq * jnp.exp(b)).astype(BF16), state.astype(BF16))

        lv = lv_ref[...]
        scores = jnp.zeros((HG_UNIT, HG_UNIT), F32)
        for idx, m in enumerate(HG_LEVELS):
            blk = 2 * m
            nb = HG_UNIT // blk
            ref = b.reshape(nb, blk, HGRN_DH)[:, m - 1:m, :]
            ref = jnp.broadcast_to(ref, (nb, blk, HGRN_DH)).reshape(HG_UNIT, HGRN_DH)
            ql = qq * jnp.exp(jnp.minimum(b - ref, 0.0))
            kl = kk * jnp.exp(jnp.minimum(ref - b, 0.0))
            s_l = _dot_nt(ql.astype(BF16), kl.astype(BF16))
            scores = jnp.where(lv == idx, s_l, scores)
        o = o + _dot(scores.astype(BF16), vb)

        kpad_ref[HG_BAND:, :] = kk
        bpad_ref[HG_BAND:, :] = b
        vpad_ref[HG_BAND:, :] = v
        tmod = lax.broadcasted_iota(jnp.int32, (HG_UNIT, HGRN_DH), 0) % HG_BAND
        for d in range(HG_BAND):
            win = pl.ds(HG_BAND - d, HG_UNIT)
            k_d = kpad_ref[win, :]
            b_d = bpad_ref[win, :]
            v_d = vpad_ref[win, :]
            p = qq * k_d * jnp.exp(jnp.minimum(b - b_d, 0.0))
            s_d = jnp.sum(p, axis=-1, keepdims=True)
            o = o + jnp.where(tmod >= d, s_d * v_d, 0.0)

        b_last = b[HG_UNIT - 1:HG_UNIT, :]
        kdec = kk * jnp.exp(b_last - b)
        state_ref[...] = state * jnp.exp(b_last) + _dot_tn(vb, kdec.astype(BF16))

        o = o * lax.rsqrt(jnp.mean(o * o, axis=-1, keepdims=True) + RMS_EPS)
        o_ref[0, rows, :] = (o * gain * _silu(gg)).astype(BF16)
        return carry

    lax.fori_loop(0, n_units, unit, 0)


def _hgrn(proj, lb, gain):
    bsz, s, _ = proj.shape
    ts = min(s, 512)
    tril, lv = _hgrn_consts()
    nh = HGRN_HEADS

    def col(off):
        return pl.BlockSpec((1, ts, HGRN_DH), lambda b, h, i, off=off: (b, i, off * nh + h))

    return pl.pallas_call(
        _hgrn_kernel,
        grid=(bsz, nh, s // ts),
        in_specs=[col(0), col(1), col(2), col(3),
                  pl.BlockSpec((1, HGRN_DH), lambda b, h, i: (0, h)),
                  pl.BlockSpec((1, HGRN_DH), lambda b, h, i: (0, h)),
                  pl.BlockSpec((HG_UNIT, HG_UNIT), lambda b, h, i: (0, 0)),
                  pl.BlockSpec((HG_UNIT, HG_UNIT), lambda b, h, i: (0, 0))],
        out_specs=pl.BlockSpec((1, ts, HGRN_DH), lambda b, h, i: (b, i, h)),
        out_shape=jax.ShapeDtypeStruct((bsz, s, D_HGRN), BF16),
        scratch_shapes=[pltpu.VMEM((HGRN_DH, HGRN_DH), F32),
                        pltpu.VMEM((HG_UNIT + HG_BAND, HGRN_DH), F32),
                        pltpu.VMEM((HG_UNIT + HG_BAND, HGRN_DH), F32),
                        pltpu.VMEM((HG_UNIT + HG_BAND, HGRN_DH), F32)],
        compiler_params=_cparams(("parallel", "parallel", "arbitrary")),
        name="hgrn2",
    )(proj, proj, proj, proj, lb, gain, tril, lv)


def _sb_kernel(q_ref, k_ref, v_ref, u_ref, o_ref):
    qi = pl.program_id(2)
    tq = SB_BLOCK
    u_mat = u_ref[...]
    row = lax.broadcasted_iota(jnp.int32, (tq, tq), 0)
    colv = lax.broadcasted_iota(jnp.int32, (tq, tq), 1)
    heads_per_block = LANES // SB_DH
    outs = []
    for hh in range(heads_per_block):
        lanes = slice(hh * SB_DH, (hh + 1) * SB_DH)
        q = q_ref[0, :, lanes] * (SB_DH ** -0.5)

        def cond(carry):
            kb, mx, _, _ = carry
            return jnp.logical_and(kb >= 0, mx > -SB_SKIP)

        def body(carry):
            kb, _, crow, acc = carry
            start = pl.multiple_of(kb * tq, tq)
            kblk = k_ref[0, pl.ds(start, tq), lanes]
            vblk = v_ref[0, pl.ds(start, tq), lanes]
            z = _dot_nt(q, kblk)
            strict = (colv + (kb - qi) * tq) < row
            logn = -(jnp.maximum(z, 0.0) + jnp.log(1.0 + jnp.exp(-jnp.abs(z))))
            logn = jnp.where(strict, logn, 0.0)
            hi = logn.astype(BF16)
            lo = (logn - hi.astype(F32)).astype(BF16)
            after = crow + _dot(hi, u_mat) + _dot(lo, u_mat)
            a = jnp.where(strict, jnp.exp(z + logn + after), 0.0)
            acc = acc + _dot(a.astype(BF16), vblk)
            crow = crow + jnp.sum(logn, axis=-1, keepdims=True)
            return kb - 1, jnp.max(crow), crow, acc

        init = (qi, jnp.float32(0.0), jnp.zeros((tq, 1), F32), jnp.zeros((tq, SB_DH), F32))
        _, _, _, acc = lax.while_loop(cond, body, init)
        outs.append(acc)
    o_ref[0] = jnp.concatenate(outs, axis=-1).astype(BF16)


def _stick_breaking(proj):
    bsz, s, _ = proj.shape
    tq = SB_BLOCK
    n_pairs = D_SB // LANES
    base = 4 * D_HGRN // LANES
    jj = jnp.arange(tq)
    u_mat = (jj[:, None] > jj[None, :]).astype(BF16)
    return pl.pallas_call(
        _sb_kernel,
        grid=(bsz, n_pairs, s // tq),
        in_specs=[pl.BlockSpec((1, tq, LANES), lambda b, p, i: (b, i, base + p)),
                  pl.BlockSpec((1, s, LANES), lambda b, p, i: (b, 0, base + n_pairs + p)),
                  pl.BlockSpec((1, s, LANES), lambda b, p, i: (b, 0, base + 2 * n_pairs + p)),
                  pl.BlockSpec((tq, tq), lambda b, p, i: (0, 0))],
        out_specs=pl.BlockSpec((1, tq, LANES), lambda b, p, i: (b, i, p)),
        out_shape=jax.ShapeDtypeStruct((bsz, s, D_SB), BF16),
        compiler_params=_cparams(("parallel", "parallel", "parallel")),
        name="stick_breaking",
    )(proj, proj, proj, u_mat)


def _residual_ln(x, y, gate, gain, bias, alpha):
    r = alpha * x + gate * y
    mu = jnp.mean(r, axis=-1, keepdims=True)
    rc = r - mu
    var = jnp.mean(rc * rc, axis=-1, keepdims=True)
    return rc * lax.rsqrt(var + LN_EPS) * gain + bias


def _emit(xn, outs, mod_next_ref):
    outs[0][0] = xn
    if mod_next_ref is not None:
        mn = mod_next_ref[0]
        outs[1][0] = (xn * (1.0 + mn[1:2]) + mn[0:1]).astype(BF16)


def _outproj_kernel(alpha, has_next, oh_ref, os_ref, x_ref, w_ref, mod_ref, ln_ref, *rest):
    mod_next_ref = rest[0] if has_next else None
    outs = rest[1:] if has_next else rest
    y = _dot(oh_ref[0], w_ref[0:D_HGRN, :]) + _dot(os_ref[0], w_ref[D_HGRN:, :])
    ln = ln_ref[...]
    xn = _residual_ln(x_ref[0], y, mod_ref[0][2:3], ln[0:1], ln[1:2], alpha)
    _emit(xn, outs, mod_next_ref)


def _row_specs(bsz, s, d, tm, mod_next):
    x_spec = pl.BlockSpec((1, tm, d), lambda b, i: (b, i, 0))
    mod_spec = pl.BlockSpec((1, 3, d), lambda b, i: (b, 0, 0))
    ln_spec = pl.BlockSpec((2, d), lambda b, i: (0, 0))
    out_specs = [pl.BlockSpec((1, tm, d), lambda b, i: (b, i, 0))]
    out_shape = [jax.ShapeDtypeStruct((bsz, s, d), F32)]
    if mod_next is not None:
        out_specs.append(pl.BlockSpec((1, tm, d), lambda b, i: (b, i, 0)))
        out_shape.append(jax.ShapeDtypeStruct((bsz, s, d), BF16))
    return x_spec, mod_spec, ln_spec, out_specs, out_shape


def _out_proj_ln(o_h, o_s, x, w_out, mod, ln, mod_next, alpha):
    bsz, s, d = x.shape
    tm = min(s, 512)
    x_spec, mod_spec, ln_spec, out_specs, out_shape = _row_specs(bsz, s, d, tm, mod_next)
    in_specs = [pl.BlockSpec((1, tm, D_HGRN), lambda b, i: (b, i, 0)),
                pl.BlockSpec((1, tm, D_SB), lambda b, i: (b, i, 0)),
                x_spec,
                pl.BlockSpec((d, d), lambda b, i: (0, 0), pipeline_mode=pl.Buffered(1)),
                mod_spec, ln_spec]
    args = [o_h, o_s, x, w_out, mod, ln]
    if mod_next is not None:
        in_specs.append(mod_spec)
        args.append(mod_next)
    return pl.pallas_call(
        functools.partial(_outproj_kernel, alpha, mod_next is not None),
        grid=(bsz, s // tm),
        in_specs=in_specs, out_specs=out_specs, out_shape=out_shape,
        compiler_params=_cparams(("parallel", "parallel")),
        name="out_proj_ln",
    )(*args)


def _dense_kernel(alpha, has_next, n_chunks, h_ref, x_ref, wg_ref, wu_ref, wd_ref, mod_ref, ln_ref, *rest):
    mod_next_ref = rest[0] if has_next else None
    outs = rest[1:] if has_next else rest
    h = h_ref[0]
    ff = wg_ref.shape[1]
    tf = ff // n_chunks
    y = None
    for j in range(n_chunks):
        cols = slice(j * tf, (j + 1) * tf)
        g = _dot(h, wg_ref[:, cols])
        u = _dot(h, wu_ref[:, cols])
        part = _dot((_silu(g) * u).astype(BF16), wd_ref[cols, :])
        y = part if y is None else y + part
    ln = ln_ref[...]
    xn = _residual_ln(x_ref[0], y, mod_ref[0][2:3], ln[0:1], ln[1:2], alpha)
    _emit(xn, outs, mod_next_ref)


def _dense_ffn_ln(h, x, wg, wu, wd, mod, ln, mod_next, alpha):
    bsz, s, d = x.shape
    ff = wg.shape[1]
    tm = min(s, 512)
    x_spec, mod_spec, ln_spec, out_specs, out_shape = _row_specs(bsz, s, d, tm, mod_next)
    once = pl.Buffered(1)
    in_specs = [pl.BlockSpec((1, tm, d), lambda b, i: (b, i, 0)),
                x_spec,
                pl.BlockSpec((d, ff), lambda b, i: (0, 0), pipeline_mode=once),
                pl.BlockSpec((d, ff), lambda b, i: (0, 0), pipeline_mode=once),
                pl.BlockSpec((ff, d), lambda b, i: (0, 0), pipeline_mode=once),
                mod_spec, ln_spec]
    args = [h, x, wg, wu, wd, mod, ln]
    if mod_next is not None:
        in_specs.append(mod_spec)
        args.append(mod_next)
    n_chunks = 2 if (ff // 2) % LANES == 0 else 1
    return pl.pallas_call(
        functools.partial(_dense_kernel, alpha, mod_next is not None, n_chunks),
        grid=(bsz, s // tm),
        in_specs=in_specs, out_specs=out_specs, out_shape=out_shape,
        compiler_params=_cparams(("parallel", "parallel")),
        name="dense_ffn_ln",
    )(*args)


def _router_kernel(h_ref, w_ref, tri_ref, route_ref, count_ref, run_ref):
    @pl.when(pl.program_id(0) == 0)
    def _():
        run_ref[...] = jnp.zeros_like(run_ref)

    tm = h_ref.shape[0]
    lane = lax.broadcasted_iota(jnp.int32, (tm, LANES), 1)
    logits = _dot(h_ref[...], w_ref[...])
    neg = jnp.float32(-jnp.inf)
    logits = jnp.where(lane < N_EXPERTS, logits, neg)
    m1 = jnp.max(logits, axis=-1, keepdims=True)
    i1 = jnp.min(jnp.where(logits == m1, lane, LANES), axis=-1, keepdims=True)
    rest = jnp.where(lane == i1, neg, logits)
    m2 = jnp.max(rest, axis=-1, keepdims=True)
    i2 = jnp.min(jnp.where(rest == m2, lane, LANES), axis=-1, keepdims=True)
    e21 = jnp.exp(m2 - m1)
    w1 = 1.0 / (1.0 + e21)
    w2 = e21 / (1.0 + e21)

    hot1 = lane == i1
    hot2 = lane == i2
    both = jnp.where(jnp.logical_or(hot1, hot2), 1.0, 0.0)
    before = _dot(tri_ref[...], both.astype(BF16)) + run_ref[...]
    r1 = jnp.sum(jnp.where(hot1, before, 0.0), axis=-1, keepdims=True)
    r2 = jnp.sum(jnp.where(hot2, before, 0.0), axis=-1, keepdims=True)
    run_ref[...] = run_ref[...] + jnp.sum(both, axis=0, keepdims=True)
    count_ref[...] = run_ref[...]

    cols = [i1.astype(F32), i2.astype(F32), r1, r2, w1, w2]
    route = jnp.zeros((tm, LANES), F32)
    for c, val in enumerate(cols):
        route = jnp.where(lane == c, val, route)
    route_ref[...] = route[:, 0:8]


def _router(h2d, w_router):
    n, d = h2d.shape
    tm = min(n, 512)
    w_pad = jnp.zeros((d, LANES), BF16).at[:, :N_EXPERTS].set(w_router.astype(BF16))
    jj = jnp.arange(tm)
    tri = (jj[None, :] < jj[:, None]).astype(BF16)
    return pl.pallas_call(
        _router_kernel,
        grid=(n // tm,),
        in_specs=[pl.BlockSpec((tm, d), lambda i: (i, 0)),
                  pl.BlockSpec((d, LANES), lambda i: (0, 0)),
                  pl.BlockSpec((tm, tm), lambda i: (0, 0))],
        out_specs=[pl.BlockSpec((tm, 8), lambda i: (i, 0)),
                   pl.BlockSpec((1, LANES), lambda i: (0, 0))],
        out_shape=[jax.ShapeDtypeStruct((n, 8), F32),
                   jax.ShapeDtypeStruct((1, LANES), F32)],
        scratch_shapes=[pltpu.VMEM((1, LANES), F32)],
        compiler_params=_cparams(("arbitrary",)),
        name="moe_router",
    )(h2d, w_pad, tri)


def _gmm_kernel(te_ref, nt_ref, x_ref, wg_ref, wu_ref, wd_ref, o_ref, acc_ref):
    i = pl.program_id(0)
    j = pl.program_id(1)
    nj = pl.num_programs(1)
    live = i < nt_ref[0]

    @pl.when(jnp.logical_and(live, j == 0))
    def _():
        acc_ref[...] = jnp.zeros_like(acc_ref)

    @pl.when(live)
    def _():
        x = x_ref[...]
        g = _dot(x, wg_ref[0])
        u = _dot(x, wu_ref[0])
        acc_ref[...] += _dot((_silu(g) * u).astype(BF16), wd_ref[0])

    @pl.when(jnp.logical_and(live, j == nj - 1))
    def _():
        o_ref[...] = acc_ref[...].astype(BF16)

    @pl.when(jnp.logical_and(jnp.logical_not(live), j == nj - 1))
    def _():
        o_ref[...] = jnp.zeros_like(o_ref)


def _grouped_swiglu(xs, tile_expert, n_tiles_used, wg, wu, wd):
    n_slots, d = xs.shape
    ff = wg.shape[2]
    tm, tf = MOE_TILE, MOE_FF_TILE
    n_tiles = n_slots // tm

    def row_map(i, j, te, nt):
        return (jnp.minimum(i, nt[0] - 1), 0)

    def ff_block(i, j, te, nt):
        return jnp.where(i < nt[0], j, ff // tf - 1)

    grid_spec = pltpu.PrefetchScalarGridSpec(
        num_scalar_prefetch=2,
        grid=(n_tiles, ff // tf),
        in_specs=[pl.BlockSpec((tm, d), row_map),
                  pl.BlockSpec((1, d, tf), lambda i, j, te, nt: (te[i], 0, ff_block(i, j, te, nt))),
                  pl.BlockSpec((1, d, tf), lambda i, j, te, nt: (te[i], 0, ff_block(i, j, te, nt))),
                  pl.BlockSpec((1, tf, d), lambda i, j, te, nt: (te[i], ff_block(i, j, te, nt), 0))],
        out_specs=pl.BlockSpec((tm, d), lambda i, j, te, nt: (i, 0)),
        scratch_shapes=[pltpu.VMEM((tm, d), F32)],
    )
    return pl.pallas_call(
        _gmm_kernel,
        grid_spec=grid_spec,
        out_shape=jax.ShapeDtypeStruct((n_slots, d), BF16),
        compiler_params=_cparams(("arbitrary", "arbitrary")),
        name="moe_grouped_swiglu",
    )(tile_expert, n_tiles_used, xs, wg, wu, wd)


def _combine_kernel(alpha, has_next, ya_ref, yb_ref, r_ref, x_ref, mod_ref, ln_ref, *rest):
    mod_next_ref = rest[0] if has_next else None
    outs = rest[1:] if has_next else rest
    r = r_ref[0]
    y = r[:, 4:5] * ya_ref[0].astype(F32) + r[:, 5:6] * yb_ref[0].astype(F32)
    ln = ln_ref[...]
    xn = _residual_ln(x_ref[0], y, mod_ref[0][2:3], ln[0:1], ln[1:2], alpha)
    _emit(xn, outs, mod_next_ref)


def _combine_ln(ya, yb, route, x, mod, ln, mod_next, alpha):
    bsz, s, d = x.shape
    tm = min(s, 512)
    x_spec, mod_spec, ln_spec, out_specs, out_shape = _row_specs(bsz, s, d, tm, mod_next)
    in_specs = [pl.BlockSpec((1, tm, d), lambda b, i: (b, i, 0)),
                pl.BlockSpec((1, tm, d), lambda b, i: (b, i, 0)),
                pl.BlockSpec((1, tm, 8), lambda b, i: (b, i, 0)),
                x_spec, mod_spec, ln_spec]
    args = [ya.reshape(bsz, s, d), yb.reshape(bsz, s, d), route.reshape(bsz, s, 8), x, mod, ln]
    if mod_next is not None:
        in_specs.append(mod_spec)
        args.append(mod_next)
    return pl.pallas_call(
        functools.partial(_combine_kernel, alpha, mod_next is not None),
        grid=(bsz, s // tm),
        in_specs=in_specs, out_specs=out_specs, out_shape=out_shape,
        compiler_params=_cparams(("parallel", "parallel")),
        name="moe_combine_ln",
    )(*args)


def _dispatch_rows(h2d, pos, n_slots):
    xs = jnp.zeros((n_slots, h2d.shape[1]), h2d.dtype)
    xs = xs.at[pos[0]].set(h2d)
    return xs.at[pos[1]].set(h2d)


def _collect_rows(ys, pos):
    return jnp.take(ys, pos[0], axis=0), jnp.take(ys, pos[1], axis=0)


def _moe_ffn_ln(h, x, w_router, wg, wu, wd, mod, ln, mod_next, alpha):
    bsz, s, d = x.shape
    n = bsz * s
    h2d = h.reshape(n, d)
    route, counts = _router(h2d, w_router)

    tm = MOE_TILE
    cnt = counts[0, :N_EXPERTS].astype(jnp.int32)
    padded = ((cnt + tm - 1) // tm) * tm
    ends = jnp.cumsum(padded)
    offs = ends - padded
    e = route[:, 0:2].astype(jnp.int32)
    rank = route[:, 2:4].astype(jnp.int32)
    pos = (jnp.take(offs, e) + rank).T
    n_tiles = (2 * n) // tm + N_EXPERTS
    n_slots = n_tiles * tm
    tile_start = jnp.arange(n_tiles, dtype=jnp.int32) * tm
    tile_expert = jnp.minimum(jnp.searchsorted(ends, tile_start, side="right"),
                              N_EXPERTS - 1).astype(jnp.int32)
    n_used = (ends[-1] // tm).astype(jnp.int32).reshape(1)

    xs = _dispatch_rows(h2d, pos, n_slots)
    ys = _grouped_swiglu(xs, tile_expert, n_used, wg, wu, wd)
    ya, yb = _collect_rows(ys, pos)
    return _combine_ln(ya, yb, route, x, mod, ln, mod_next, alpha)


def kernel(x, c, w_ada, b_ada, w_in, w_out, hgrn_lb_logits, hgrn_norm_gain,
           w_dense_gate, w_dense_up, w_dense_down, w_router, w_moe_gate, w_moe_up, w_moe_down,
           ln_gain, ln_bias):
    depth = w_in.shape[0]
    bsz, s, d = x.shape
    alpha = (2 * depth) ** 0.25

    mods = _ada_all(c, w_ada, b_ada).reshape(depth, 2, bsz, 3, d)
    lb_all = _lower_bounds(hgrn_lb_logits)
    ln_all = jnp.stack([ln_gain, ln_bias], axis=2)

    h = _modulate(x, mods[0, 0])
    for layer in range(depth):
        proj = _in_proj(h.reshape(bsz * s, d), w_in[layer].astype(BF16)).reshape(bsz, s, D_IN_PROJ)
        o_h = _hgrn(proj, lb_all[layer:layer + 1], hgrn_norm_gain[layer:layer + 1])
        o_s = _stick_breaking(proj)
        x, h = _out_proj_ln(o_h, o_s, x, w_out[layer].astype(BF16), mods[layer, 0],
                            ln_all[layer, 0], mods[layer, 1], alpha)
        mod_next = mods[layer + 1, 0] if layer + 1 < depth else None
        j = layer // 2
        if layer % 2 == 0:
            res = _dense_ffn_ln(h, x, w_dense_gate[j].astype(BF16), w_dense_up[j].astype(BF16),
                                w_dense_down[j].astype(BF16), mods[layer, 1], ln_all[layer, 1],
                                mod_next, alpha)
        else:
            res = _moe_ffn_ln(h, x, w_router[j], w_moe_gate[j].astype(BF16), w_moe_up[j].astype(BF16),
                              w_moe_down[j].astype(BF16), mods[layer, 1], ln_all[layer, 1],
                              mod_next, alpha)
        if mod_next is not None:
            x, h = res
        else:
            (x,) = res
    return x
```

```python
import functools
import math

import jax
import jax.numpy as jnp
from jax import lax
from jax.experimental import pallas as pl
from jax.experimental.pallas import tpu as pltpu

D_MODEL = 1024
D_HGRN = 512
HGRN_HEADS = 4
HGRN_DH = 128
D_SB = 512
SB_HEADS = 8
SB_DH = 64
D_IN_PROJ = 4 * D_HGRN + 3 * D_SB
N_EXPERTS = 8
LN_EPS = 1e-5
RMS_EPS = 1e-6

LANES = 128
VMEM_LIMIT = 52 * 1024 * 1024

HG_UNIT = 128
HG_BAND = 4
HG_PAD = 8
HG_LEVELS = (64, 32, 16, 8, 4)
SB_BLOCK = 128
SB_SKIP = 110.0
SB_WIN = 3
MOE_TILE = 1024
MOE_FF_TILE = 512

F32 = jnp.float32
BF16 = jnp.bfloat16


def _cparams(sem):
    return pltpu.CompilerParams(dimension_semantics=sem, vmem_limit_bytes=VMEM_LIMIT)


def _dot(a, b):
    return jnp.dot(a, b, preferred_element_type=F32)


def _dot_nt(a, b):
    return lax.dot_general(a, b, (((1,), (1,)), ((), ())), preferred_element_type=F32)


def _dot_tn(a, b):
    return lax.dot_general(a, b, (((0,), (0,)), ((), ())), preferred_element_type=F32)


def _sigmoid(x):
    return 1.0 / (1.0 + jnp.exp(-x))


def _silu(x):
    return x * _sigmoid(x)


def _ada_kernel(c_ref, w_ref, b_ref, o_ref):
    sc = _silu(c_ref[...])
    o_ref[0] = jnp.dot(sc, w_ref[0], preferred_element_type=F32,
                       precision=lax.Precision.HIGHEST) + b_ref[0]


def _ada_all(c, w_ada, b_ada):
    n_sub = w_ada.shape[0] * w_ada.shape[1]
    bsz, d = c.shape
    w = w_ada.reshape(n_sub, d, 3 * d)
    b = b_ada.reshape(n_sub, 1, 3 * d)
    tn = 1024
    return pl.pallas_call(
        _ada_kernel,
        grid=(n_sub, 3 * d // tn),
        in_specs=[pl.BlockSpec((bsz, d), lambda s, j: (0, 0)),
                  pl.BlockSpec((1, d, tn), lambda s, j: (s, 0, j)),
                  pl.BlockSpec((1, 1, tn), lambda s, j: (s, 0, j))],
        out_specs=pl.BlockSpec((1, bsz, tn), lambda s, j: (s, 0, j)),
        out_shape=jax.ShapeDtypeStruct((n_sub, bsz, 3 * d), F32),
        compiler_params=_cparams(("parallel", "parallel")),
        name="ada_modulation",
    )(c, w, b)


def _lb_kernel(l_ref, o_ref):
    x = l_ref[...]
    m = jnp.max(x, axis=0, keepdims=True)
    e = jnp.exp(x - m)
    p = e / jnp.sum(e, axis=0, keepdims=True)
    depth = x.shape[0]
    run = jnp.zeros_like(p[0:1])
    rows = []
    for i in range(depth):
        run = run + p[i:i + 1]
        rows.append(run - p[0:1])
    o_ref[...] = jnp.concatenate(rows, axis=0)


def _lower_bounds(lb_logits):
    return pl.pallas_call(
        _lb_kernel,
        out_shape=jax.ShapeDtypeStruct(lb_logits.shape, F32),
        name="hgrn_lower_bounds",
    )(lb_logits)


def _modulate_kernel(x_ref, m_ref, h_ref):
    m = m_ref[0]
    h_ref[0] = (x_ref[0] * (1.0 + m[1:2]) + m[0:1]).astype(BF16)


def _modulate(x, mod):
    bsz, s, d = x.shape
    ts = min(s, 1024)
    return pl.pallas_call(
        _modulate_kernel,
        grid=(bsz, s // ts),
        in_specs=[pl.BlockSpec((1, ts, d), lambda b, i: (b, i, 0)),
                  pl.BlockSpec((1, 3, d), lambda b, i: (b, 0, 0))],
        out_specs=pl.BlockSpec((1, ts, d), lambda b, i: (b, i, 0)),
        out_shape=jax.ShapeDtypeStruct((bsz, s, d), BF16),
        compiler_params=_cparams(("parallel", "parallel")),
        name="modulate_first",
    )(x, mod)


def _inproj_kernel(h_ref, w_ref, o_ref):
    h = h_ref[...]
    n = w_ref.shape[1]
    tn = 512
    for j in range(n // tn):
        o_ref[:, j * tn:(j + 1) * tn] = _dot(h, w_ref[:, j * tn:(j + 1) * tn]).astype(BF16)


def _in_proj(h2d, w_in):
    n, d = h2d.shape
    dout = w_in.shape[1]
    tm = min(n, 512)
    return pl.pallas_call(
        _inproj_kernel,
        grid=(n // tm,),
        in_specs=[pl.BlockSpec((tm, d), lambda i: (i, 0)),
                  pl.BlockSpec((d, dout), lambda i: (0, 0), pipeline_mode=pl.Buffered(1))],
        out_specs=pl.BlockSpec((tm, dout), lambda i: (i, 0)),
        out_shape=jax.ShapeDtypeStruct((n, dout), BF16),
        compiler_params=_cparams(("parallel",)),
        name="in_proj",
    )(h2d, w_in)


def _hgrn_consts():
    t = jnp.arange(HG_UNIT)[:, None]
    s = jnp.arange(HG_UNIT)[None, :]
    tril = (s <= t).astype(F32)
    lv = jnp.full((HG_UNIT, HG_UNIT), -1, jnp.int32)
    for idx, m in enumerate(HG_LEVELS):
        blk = 2 * m
        hit = (t // blk == s // blk) & (t % blk >= m) & (s % blk < m)
        lv = jnp.where(hit, idx, lv)
    return tril, lv


def _hgrn_kernel(q_ref, f_ref, i_ref, g_ref, lb_ref, gain_ref, tril_ref, lv_ref, o_ref,
                 state_ref, kpad_ref, bpad_ref, vpad_ref):
    @pl.when(pl.program_id(2) == 0)
    def _():
        state_ref[...] = jnp.zeros_like(state_ref)

    zpad = jnp.zeros((HG_PAD, HGRN_DH), F32)
    kpad_ref[0:HG_PAD, :] = zpad
    bpad_ref[0:HG_PAD, :] = zpad
    vpad_ref[0:HG_PAD, :] = zpad

    lb = lb_ref[...]
    gain = gain_ref[...]
    n_units = q_ref.shape[1] // HG_UNIT

    def unit(u, carry):
        r0 = pl.multiple_of(u * HG_UNIT, HG_UNIT)
        rows = pl.ds(r0, HG_UNIT)
        qr = q_ref[0, rows, :].astype(F32)
        fl = f_ref[0, rows, :].astype(F32)
        v = i_ref[0, rows, :].astype(F32)
        gg = g_ref[0, rows, :].astype(F32)

        f = lb + (1.0 - lb) * _sigmoid(fl)
        logf = jnp.log(f)
        kk = 1.0 - f
        qq = _silu(qr)
        b = jnp.dot(tril_ref[...], logf, preferred_element_type=F32,
                    precision=lax.Precision.HIGHEST)
        vb = v.astype(BF16)

        state = state_ref[...]
        o = _dot_nt((qq * jnp.exp(b)).astype(BF16), state.astype(BF16))

        lv = lv_ref[...]
        scores = jnp.zeros((HG_UNIT, HG_UNIT), F32)
        for idx, m in enumerate(HG_LEVELS):
            blk = 2 * m
            nb = HG_UNIT // blk
            ref = b.reshape(nb, blk, HGRN_DH)[:, m - 1:m, :]
            ref = jnp.broadcast_to(ref, (nb, blk, HGRN_DH)).reshape(HG_UNIT, HGRN_DH)
            ql = qq * jnp.exp(b - ref)
            kl = kk * jnp.exp(ref - b)
            s_l = _dot_nt(ql.astype(BF16), kl.astype(BF16))
            scores = jnp.where(lv == idx, s_l, scores)
        o = o + _dot(scores.astype(BF16), vb)

        kpad_ref[HG_PAD:, :] = kk
        bpad_ref[HG_PAD:, :] = b
        vpad_ref[HG_PAD:, :] = v
        tmod = jnp.bitwise_and(lax.broadcasted_iota(jnp.int32, (HG_UNIT, HGRN_DH), 0), HG_BAND - 1)
        for d in range(HG_BAND):
            win = pl.ds(HG_PAD - d, HG_UNIT)
            k_d = kpad_ref[win, :]
            b_d = bpad_ref[win, :]
            v_d = vpad_ref[win, :]
            p = qq * k_d * jnp.exp(b - b_d)
            s_d = jnp.sum(p, axis=-1, keepdims=True)
            o = o + jnp.where(tmod >= d, s_d * v_d, 0.0)

        b_last = b[HG_UNIT - 1:HG_UNIT, :]
        kdec = kk * jnp.exp(b_last - b)
        state_ref[...] = state * jnp.exp(b_last) + _dot_tn(vb, kdec.astype(BF16))

        o = o * lax.rsqrt(jnp.mean(o * o, axis=-1, keepdims=True) + RMS_EPS)
        o_ref[0, rows, :] = (o * gain * _silu(gg)).astype(BF16)
        return carry

    lax.fori_loop(0, n_units, unit, 0, unroll=2)


def _hgrn(proj, lb, gain):
    bsz, s, _ = proj.shape
    ts = min(s, 512)
    tril, lv = _hgrn_consts()
    nh = HGRN_HEADS

    def col(off):
        return pl.BlockSpec((1, ts, HGRN_DH), lambda b, h, i, off=off: (b, i, off * nh + h))

    return pl.pallas_call(
        _hgrn_kernel,
        grid=(bsz, nh, s // ts),
        in_specs=[col(0), col(1), col(2), col(3),
                  pl.BlockSpec((1, HGRN_DH), lambda b, h, i: (0, h)),
                  pl.BlockSpec((1, HGRN_DH), lambda b, h, i: (0, h)),
                  pl.BlockSpec((HG_UNIT, HG_UNIT), lambda b, h, i: (0, 0)),
                  pl.BlockSpec((HG_UNIT, HG_UNIT), lambda b, h, i: (0, 0))],
        out_specs=pl.BlockSpec((1, ts, HGRN_DH), lambda b, h, i: (b, i, h)),
        out_shape=jax.ShapeDtypeStruct((bsz, s, D_HGRN), BF16),
        scratch_shapes=[pltpu.VMEM((HGRN_DH, HGRN_DH), F32),
                        pltpu.VMEM((HG_UNIT + HG_PAD, HGRN_DH), F32),
                        pltpu.VMEM((HG_UNIT + HG_PAD, HGRN_DH), F32),
                        pltpu.VMEM((HG_UNIT + HG_PAD, HGRN_DH), F32)],
        compiler_params=_cparams(("parallel", "parallel", "arbitrary")),
        name="hgrn2",
    )(proj, proj, proj, proj, lb, gain, tril, lv)


def _sb_kernel(q_ref, k_ref, v_ref, uo_ref, o_ref, kp_ref, vp_ref, c_ref, acc_ref):
    tq = SB_BLOCK
    nq = q_ref.shape[1] // tq
    pad = (SB_WIN - 1) * tq
    kp_ref[0:pad, :] = jnp.zeros((pad, LANES), BF16)
    vp_ref[0:pad, :] = jnp.zeros((pad, LANES), BF16)
    kp_ref[pad:, :] = k_ref[0]
    vp_ref[pad:, :] = v_ref[0]

    first = lax.broadcasted_iota(jnp.int32, (tq, LANES), 1) < SB_DH
    rowi = lax.broadcasted_iota(jnp.int32, (2 * tq, tq), 0)
    coli = lax.broadcasted_iota(jnp.int32, (2 * tq, tq), 1)
    strict = coli < jnp.bitwise_and(rowi, tq - 1)
    neg_inf = jnp.float32(-jnp.inf)

    def stacked_q(qi):
        rows = pl.ds(pl.multiple_of(qi * tq, tq), tq)
        q = q_ref[0, rows, :] * (SB_DH ** -0.5)
        zero = jnp.zeros_like(q)
        return rows, jnp.concatenate([jnp.where(first, q, zero), jnp.where(first, zero, q)], axis=0)

    def window(q2, p_top, diag, crow, acc):
        start = pl.multiple_of((p_top - (SB_WIN - 1)) * tq, tq)
        kw = kp_ref[pl.ds(start, SB_WIN * tq), :]
        vw = vp_ref[pl.ds(start, SB_WIN * tq), :]
        z = _dot_nt(q2, kw)
        logn = -(jnp.maximum(z, 0.0) + jnp.log(1.0 + jnp.exp(-jnp.abs(z))))
        order = list(reversed(range(SB_WIN)))
        logs, lhs = [], []
        for j in order:
            lj = logn[:, j * tq:(j + 1) * tq]
            if diag and j == SB_WIN - 1:
                lj = jnp.where(strict, lj, 0.0)
            hi = lj.astype(BF16)
            lo = (lj - hi.astype(F32)).astype(BF16)
            logs.append(lj)
            lhs.append(jnp.concatenate([hi, lo], axis=1))
        ct = _dot(jnp.concatenate(lhs, axis=0), uo_ref[...])
        parts = [None] * SB_WIN
        for idx, j in enumerate(order):
            blk = ct[idx * 2 * tq:(idx + 1) * 2 * tq]
            a = jnp.exp(z[:, j * tq:(j + 1) * tq] + logs[idx] + (crow + blk[:, 0:tq]))
            if diag and j == SB_WIN - 1:
                a = jnp.where(strict, a, 0.0)
            parts[j] = a.astype(BF16)
            crow = crow + blk[:, tq:]
        acc = acc + _dot(jnp.concatenate(parts, axis=1), vw)
        return crow, acc

    def emit(rows, acc):
        o_ref[0, rows, :] = jnp.where(first, acc[0:tq], acc[tq:]).astype(BF16)

    def fast(qi, worst):
        rows, q2 = stacked_q(qi)
        zero = jnp.zeros((2 * tq, LANES), F32)
        crow, acc = window(q2, qi + SB_WIN - 1, True, zero, zero)
        emit(rows, acc)
        c_ref[qi] = crow
        acc_ref[qi] = acc
        return jnp.maximum(worst, jnp.where(qi >= SB_WIN, crow, neg_inf))

    worst = lax.fori_loop(0, nq, fast, jnp.full((2 * tq, LANES), neg_inf, F32), unroll=2)

    @pl.when(jnp.max(worst) > -SB_SKIP)
    def _():
        def finish(qi, carry):
            rows, q2 = stacked_q(qi)

            def cond(c):
                p_top, mx, _, _ = c
                return jnp.logical_and(p_top >= SB_WIN - 1, mx > -SB_SKIP)

            def body(c):
                p_top, _, crow, acc = c
                crow, acc = window(q2, p_top, False, crow, acc)
                return p_top - SB_WIN, jnp.max(crow), crow, acc

            crow0 = c_ref[qi]
            _, _, _, acc = lax.while_loop(cond, body, (qi - 1, jnp.max(crow0), crow0, acc_ref[qi]))
            emit(rows, acc)
            return carry

        lax.fori_loop(SB_WIN, nq, finish, 0)


def _stick_breaking(proj):
    bsz, s, _ = proj.shape
    tq = SB_BLOCK
    nq = s // tq
    n_pairs = D_SB // LANES
    base = 4 * D_HGRN // LANES
    jj = jnp.arange(tq)
    upper = (jj[:, None] > jj[None, :]).astype(BF16)
    half = jnp.concatenate([upper, jnp.ones((tq, tq), BF16)], axis=1)
    uo = jnp.concatenate([half, half], axis=0)

    def seq(off):
        return pl.BlockSpec((1, s, LANES), lambda b, p, off=off: (b, 0, base + off * n_pairs + p))

    return pl.pallas_call(
        _sb_kernel,
        grid=(bsz, n_pairs),
        in_specs=[seq(0), seq(1), seq(2),
                  pl.BlockSpec((2 * tq, 2 * tq), lambda b, p: (0, 0))],
        out_specs=pl.BlockSpec((1, s, LANES), lambda b, p: (b, 0, p)),
        out_shape=jax.ShapeDtypeStruct((bsz, s, D_SB), BF16),
        scratch_shapes=[pltpu.VMEM((s + (SB_WIN - 1) * tq, LANES), BF16),
                        pltpu.VMEM((s + (SB_WIN - 1) * tq, LANES), BF16),
                        pltpu.VMEM((nq, 2 * tq, LANES), F32),
                        pltpu.VMEM((nq, 2 * tq, LANES), F32)],
        compiler_params=_cparams(("parallel", "parallel")),
        name="stick_breaking",
    )(proj, proj, proj, uo)


def _residual_ln(x, y, gate, gain, bias, alpha):
    r = alpha * x + gate * y
    mu = jnp.mean(r, axis=-1, keepdims=True)
    rc = r - mu
    var = jnp.mean(rc * rc, axis=-1, keepdims=True)
    return rc * lax.rsqrt(var + LN_EPS) * gain + bias


def _emit(xn, outs, mod_next_ref):
    outs[0][0] = xn
    if mod_next_ref is not None:
        mn = mod_next_ref[0]
        outs[1][0] = (xn * (1.0 + mn[1:2]) + mn[0:1]).astype(BF16)


def _outproj_kernel(alpha, has_next, oh_ref, os_ref, x_ref, w_ref, mod_ref, ln_ref, *rest):
    mod_next_ref = rest[0] if has_next else None
    outs = rest[1:] if has_next else rest
    y = _dot(oh_ref[0], w_ref[0:D_HGRN, :]) + _dot(os_ref[0], w_ref[D_HGRN:, :])
    ln = ln_ref[...]
    xn = _residual_ln(x_ref[0], y, mod_ref[0][2:3], ln[0:1], ln[1:2], alpha)
    _emit(xn, outs, mod_next_ref)


def _row_specs(bsz, s, d, tm, mod_next):
    x_spec = pl.BlockSpec((1, tm, d), lambda b, i: (b, i, 0))
    mod_spec = pl.BlockSpec((1, 3, d), lambda b, i: (b, 0, 0))
    ln_spec = pl.BlockSpec((2, d), lambda b, i: (0, 0))
    out_specs = [pl.BlockSpec((1, tm, d), lambda b, i: (b, i, 0))]
    out_shape = [jax.ShapeDtypeStruct((bsz, s, d), F32)]
    if mod_next is not None:
        out_specs.append(pl.BlockSpec((1, tm, d), lambda b, i: (b, i, 0)))
        out_shape.append(jax.ShapeDtypeStruct((bsz, s, d), BF16))
    return x_spec, mod_spec, ln_spec, out_specs, out_shape


def _out_proj_ln(o_h, o_s, x, w_out, mod, ln, mod_next, alpha):
    bsz, s, d = x.shape
    tm = min(s, 512)
    x_spec, mod_spec, ln_spec, out_specs, out_shape = _row_specs(bsz, s, d, tm, mod_next)
    in_specs = [pl.BlockSpec((1, tm, D_HGRN), lambda b, i: (b, i, 0)),
                pl.BlockSpec((1, tm, D_SB), lambda b, i: (b, i, 0)),
                x_spec,
                pl.BlockSpec((d, d), lambda b, i: (0, 0), pipeline_mode=pl.Buffered(1)),
                mod_spec, ln_spec]
    args = [o_h, o_s, x, w_out, mod, ln]
    if mod_next is not None:
        in_specs.append(mod_spec)
        args.append(mod_next)
    return pl.pallas_call(
        functools.partial(_outproj_kernel, alpha, mod_next is not None),
        grid=(bsz, s // tm),
        in_specs=in_specs, out_specs=out_specs, out_shape=out_shape,
        compiler_params=_cparams(("parallel", "parallel")),
        name="out_proj_ln",
    )(*args)


def _dense_kernel(alpha, has_next, n_chunks, h_ref, x_ref, wg_ref, wu_ref, wd_ref, mod_ref, ln_ref, *rest):
    mod_next_ref = rest[0] if has_next else None
    outs = rest[1:] if has_next else rest
    h = h_ref[0]
    ff = wg_ref.shape[1]
    tf = ff // n_chunks
    y = None
    for j in range(n_chunks):
        cols = slice(j * tf, (j + 1) * tf)
        g = _dot(h, wg_ref[:, cols])
        u = _dot(h, wu_ref[:, cols])
        part = _dot((_silu(g) * u).astype(BF16), wd_ref[cols, :])
        y = part if y is None else y + part
    ln = ln_ref[...]
    xn = _residual_ln(x_ref[0], y, mod_ref[0][2:3], ln[0:1], ln[1:2], alpha)
    _emit(xn, outs, mod_next_ref)


def _dense_ffn_ln(h, x, wg, wu, wd, mod, ln, mod_next, alpha):
    bsz, s, d = x.shape
    ff = wg.shape[1]
    tm = min(s, 512)
    x_spec, mod_spec, ln_spec, out_specs, out_shape = _row_specs(bsz, s, d, tm, mod_next)
    once = pl.Buffered(1)
    in_specs = [pl.BlockSpec((1, tm, d), lambda b, i: (b, i, 0)),
                x_spec,
                pl.BlockSpec((d, ff), lambda b, i: (0, 0), pipeline_mode=once),
                pl.BlockSpec((d, ff), lambda b, i: (0, 0), pipeline_mode=once),
                pl.BlockSpec((ff, d), lambda b, i: (0, 0), pipeline_mode=once),
                mod_spec, ln_spec]
    args = [h, x, wg, wu, wd, mod, ln]
    if mod_next is not None:
        in_specs.append(mod_spec)
        args.append(mod_next)
    n_chunks = 2 if (ff // 2) % LANES == 0 else 1
    return pl.pallas_call(
        functools.partial(_dense_kernel, alpha, mod_next is not None, n_chunks),
        grid=(bsz, s // tm),
        in_specs=in_specs, out_specs=out_specs, out_shape=out_shape,
        compiler_params=_cparams(("parallel", "parallel")),
        name="dense_ffn_ln",
    )(*args)


def _router_kernel(h_ref, w_ref, tri_ref, route_ref, count_ref, run_ref):
    @pl.when(pl.program_id(0) == 0)
    def _():
        run_ref[...] = jnp.zeros_like(run_ref)

    tm = h_ref.shape[0]
    lane = lax.broadcasted_iota(jnp.int32, (tm, LANES), 1)
    logits = _dot(h_ref[...], w_ref[...])
    neg = jnp.float32(-jnp.inf)
    logits = jnp.where(lane < N_EXPERTS, logits, neg)
    m1 = jnp.max(logits, axis=-1, keepdims=True)
    i1 = jnp.min(jnp.where(logits == m1, lane, LANES), axis=-1, keepdims=True)
    rest = jnp.where(lane == i1, neg, logits)
    m2 = jnp.max(rest, axis=-1, keepdims=True)
    i2 = jnp.min(jnp.where(rest == m2, lane, LANES), axis=-1, keepdims=True)
    e21 = jnp.exp(m2 - m1)
    w1 = 1.0 / (1.0 + e21)
    w2 = e21 / (1.0 + e21)

    hot1 = lane == i1
    hot2 = lane == i2
    both = jnp.where(jnp.logical_or(hot1, hot2), 1.0, 0.0)
    before = _dot(tri_ref[...], both.astype(BF16)) + run_ref[...]
    r1 = jnp.sum(jnp.where(hot1, before, 0.0), axis=-1, keepdims=True)
    r2 = jnp.sum(jnp.where(hot2, before, 0.0), axis=-1, keepdims=True)
    run_ref[...] = run_ref[...] + jnp.sum(both, axis=0, keepdims=True)
    count_ref[...] = run_ref[...]

    cols = [i1.astype(F32), i2.astype(F32), r1, r2, w1, w2]
    route = jnp.zeros((tm, LANES), F32)
    for c, val in enumerate(cols):
        route = jnp.where(lane == c, val, route)
    route_ref[...] = route[:, 0:8]


def _router(h2d, w_router):
    n, d = h2d.shape
    tm = min(n, 512)
    w_pad = jnp.zeros((d, LANES), BF16).at[:, :N_EXPERTS].set(w_router.astype(BF16))
    jj = jnp.arange(tm)
    tri = (jj[None, :] < jj[:, None]).astype(BF16)
    return pl.pallas_call(
        _router_kernel,
        grid=(n // tm,),
        in_specs=[pl.BlockSpec((tm, d), lambda i: (i, 0)),
                  pl.BlockSpec((d, LANES), lambda i: (0, 0)),
                  pl.BlockSpec((tm, tm), lambda i: (0, 0))],
        out_specs=[pl.BlockSpec((tm, 8), lambda i: (i, 0)),
                   pl.BlockSpec((1, LANES), lambda i: (0, 0))],
        out_shape=[jax.ShapeDtypeStruct((n, 8), F32),
                   jax.ShapeDtypeStruct((1, LANES), F32)],
        scratch_shapes=[pltpu.VMEM((1, LANES), F32)],
        compiler_params=_cparams(("arbitrary",)),
        name="moe_router",
    )(h2d, w_pad, tri)


def _gmm_kernel(te_ref, nt_ref, x_ref, wg_ref, wu_ref, wd_ref, o_ref, acc_ref):
    i = pl.program_id(0)
    j = pl.program_id(1)
    nj = pl.num_programs(1)
    live = i < nt_ref[0]

    @pl.when(jnp.logical_and(live, j == 0))
    def _():
        acc_ref[...] = jnp.zeros_like(acc_ref)

    @pl.when(live)
    def _():
        x = x_ref[...]
        g = _dot(x, wg_ref[0])
        u = _dot(x, wu_ref[0])
        acc_ref[...] += _dot((_silu(g) * u).astype(BF16), wd_ref[0])

    @pl.when(jnp.logical_and(live, j == nj - 1))
    def _():
        o_ref[...] = acc_ref[...].astype(BF16)

    @pl.when(jnp.logical_and(jnp.logical_not(live), j == nj - 1))
    def _():
        o_ref[...] = jnp.zeros_like(o_ref)


def _grouped_swiglu(xs, tile_expert, n_tiles_used, wg, wu, wd):
    n_slots, d = xs.shape
    ff = wg.shape[2]
    tm, tf = MOE_TILE, MOE_FF_TILE
    n_tiles = n_slots // tm

    def row_map(i, j, te, nt):
        return (jnp.minimum(i, nt[0] - 1), 0)

    def ff_block(i, j, te, nt):
        return jnp.where(i < nt[0], j, ff // tf - 1)

    grid_spec = pltpu.PrefetchScalarGridSpec(
        num_scalar_prefetch=2,
        grid=(n_tiles, ff // tf),
        in_specs=[pl.BlockSpec((tm, d), row_map),
                  pl.BlockSpec((1, d, tf), lambda i, j, te, nt: (te[i], 0, ff_block(i, j, te, nt))),
                  pl.BlockSpec((1, d, tf), lambda i, j, te, nt: (te[i], 0, ff_block(i, j, te, nt))),
                  pl.BlockSpec((1, tf, d), lambda i, j, te, nt: (te[i], ff_block(i, j, te, nt), 0))],
        out_specs=pl.BlockSpec((tm, d), lambda i, j, te, nt: (i, 0)),
        scratch_shapes=[pltpu.VMEM((tm, d), F32)],
    )
    return pl.pallas_call(
        _gmm_kernel,
        grid_spec=grid_spec,
        out_shape=jax.ShapeDtypeStruct((n_slots, d), BF16),
        compiler_params=_cparams(("arbitrary", "arbitrary")),
        name="moe_grouped_swiglu",
    )(tile_expert, n_tiles_used, xs, wg, wu, wd)


def _combine_kernel(alpha, has_next, ya_ref, yb_ref, r_ref, x_ref, mod_ref, ln_ref, *rest):
    mod_next_ref = rest[0] if has_next else None
    outs = rest[1:] if has_next else rest
    r = r_ref[0]
    y = r[:, 4:5] * ya_ref[0].astype(F32) + r[:, 5:6] * yb_ref[0].astype(F32)
    ln = ln_ref[...]
    xn = _residual_ln(x_ref[0], y, mod_ref[0][2:3], ln[0:1], ln[1:2], alpha)
    _emit(xn, outs, mod_next_ref)


def _combine_ln(ya, yb, route, x, mod, ln, mod_next, alpha):
    bsz, s, d = x.shape
    tm = min(s, 512)
    x_spec, mod_spec, ln_spec, out_specs, out_shape = _row_specs(bsz, s, d, tm, mod_next)
    in_specs = [pl.BlockSpec((1, tm, d), lambda b, i: (b, i, 0)),
                pl.BlockSpec((1, tm, d), lambda b, i: (b, i, 0)),
                pl.BlockSpec((1, tm, 8), lambda b, i: (b, i, 0)),
                x_spec, mod_spec, ln_spec]
    args = [ya.reshape(bsz, s, d), yb.reshape(bsz, s, d), route.reshape(bsz, s, 8), x, mod, ln]
    if mod_next is not None:
        in_specs.append(mod_spec)
        args.append(mod_next)
    return pl.pallas_call(
        functools.partial(_combine_kernel, alpha, mod_next is not None),
        grid=(bsz, s // tm),
        in_specs=in_specs, out_specs=out_specs, out_shape=out_shape,
        compiler_params=_cparams(("parallel", "parallel")),
        name="moe_combine_ln",
    )(*args)


def _dispatch_rows(h2d, pos, n_slots):
    xs = jnp.zeros((n_slots, h2d.shape[1]), h2d.dtype)
    xs = xs.at[pos[0]].set(h2d)
    return xs.at[pos[1]].set(h2d)


def _collect_rows(ys, pos):
    return jnp.take(ys, pos[0], axis=0), jnp.take(ys, pos[1], axis=0)


def _moe_ffn_ln(h, x, w_router, wg, wu, wd, mod, ln, mod_next, alpha):
    bsz, s, d = x.shape
    n = bsz * s
    h2d = h.reshape(n, d)
    route, counts = _router(h2d, w_router)

    tm = MOE_TILE
    cnt = counts[0, :N_EXPERTS].astype(jnp.int32)
    padded = ((cnt + tm - 1) // tm) * tm
    ends = jnp.cumsum(padded)
    offs = ends - padded
    e = route[:, 0:2].astype(jnp.int32)
    rank = route[:, 2:4].astype(jnp.int32)
    pos = (jnp.take(offs, e) + rank).T
    n_tiles = (2 * n) // tm + N_EXPERTS
    n_slots = n_tiles * tm
    tile_start = jnp.arange(n_tiles, dtype=jnp.int32) * tm
    tile_expert = jnp.minimum(jnp.sum(tile_start[:, None] >= ends[None, :], axis=1),
                              N_EXPERTS - 1).astype(jnp.int32)
    n_used = (ends[-1] // tm).astype(jnp.int32).reshape(1)

    xs = _dispatch_rows(h2d, pos, n_slots)
    ys = _grouped_swiglu(xs, tile_expert, n_used, wg, wu, wd)
    ya, yb = _collect_rows(ys, pos)
    return _combine_ln(ya, yb, route, x, mod, ln, mod_next, alpha)


def kernel(x, c, w_ada, b_ada, w_in, w_out, hgrn_lb_logits, hgrn_norm_gain,
           w_dense_gate, w_dense_up, w_dense_down, w_router, w_moe_gate, w_moe_up, w_moe_down,
           ln_gain, ln_bias):
    depth = w_in.shape[0]
    bsz, s, d = x.shape
    alpha = (2 * depth) ** 0.25

    mods = _ada_all(c, w_ada, b_ada).reshape(depth, 2, bsz, 3, d)
    lb_all = _lower_bounds(hgrn_lb_logits)
    ln_all = jnp.stack([ln_gain, ln_bias], axis=2)

    h = _modulate(x, mods[0, 0])
    for layer in range(depth):
        proj = _in_proj(h.reshape(bsz * s, d), w_in[layer].astype(BF16)).reshape(bsz, s, D_IN_PROJ)
        o_h = _hgrn(proj, lb_all[layer:layer + 1], hgrn_norm_gain[layer:layer + 1])
        o_s = _stick_breaking(proj)
        x, h = _out_proj_ln(o_h, o_s, x, w_out[layer].astype(BF16), mods[layer, 0],
                            ln_all[layer, 0], mods[layer, 1], alpha)
        mod_next = mods[layer + 1, 0] if layer + 1 < depth else None
        j = layer // 2
        if layer % 2 == 0:
            res = _dense_ffn_ln(h, x, w_dense_gate[j].astype(BF16), w_dense_up[j].astype(BF16),
                                w_dense_down[j].astype(BF16), mods[layer, 1], ln_all[layer, 1],
                                mod_next, alpha)
        else:
            res = _moe_ffn_ln(h, x, w_router[j], w_moe_gate[j].astype(BF16), w_moe_up[j].astype(BF16),
                              w_moe_down[j].astype(BF16), mods[layer, 1], ln_all[layer, 1],
                              mod_next, alpha)
        if mod_next is not None:
            x, h = res
        else:
            (x,) = res
    return x
```

```python
import functools
import math

import jax
import jax.numpy as jnp
from jax import lax
from jax.experimental import pallas as pl
from jax.experimental.pallas import tpu as pltpu
from jax.experimental.pallas import tpu_sc as plsc

D_MODEL = 1024
D_HGRN = 512
HGRN_HEADS = 4
HGRN_DH = 128
D_SB = 512
SB_HEADS = 8
SB_DH = 64
D_IN_PROJ = 4 * D_HGRN + 3 * D_SB
N_EXPERTS = 8
LN_EPS = 1e-5
RMS_EPS = 1e-6

LANES = 128
VMEM_LIMIT = 52 * 1024 * 1024

HG_UNIT = 128
HG_BAND = 4
HG_PAD = 8
HG_LEVELS = (64, 32, 16, 8, 4)
SB_BLOCK = 128
SB_SKIP = 110.0
SB_WIN = 3
MOE_TILE = 1024
MOE_FF_TILE = 512
SC_WINDOW = 128

F32 = jnp.float32
BF16 = jnp.bfloat16
U32 = jnp.uint32


def _cparams(sem):
    return pltpu.CompilerParams(dimension_semantics=sem, vmem_limit_bytes=VMEM_LIMIT)


def _dot(a, b):
    return jnp.dot(a, b, preferred_element_type=F32)


def _dot_nt(a, b):
    return lax.dot_general(a, b, (((1,), (1,)), ((), ())), preferred_element_type=F32)


def _dot_tn(a, b):
    return lax.dot_general(a, b, (((0,), (0,)), ((), ())), preferred_element_type=F32)


def _pack_rows(v):
    w = v.shape[-1] // 2
    lo = lax.bitcast_convert_type(v[:, :w].astype(BF16).astype(F32), U32)
    hi = lax.bitcast_convert_type(v[:, w:].astype(BF16).astype(F32), U32)
    return (lo >> 16) | hi


def _unpack_rows(p):
    lo = lax.bitcast_convert_type(p << 16, F32)
    hi = lax.bitcast_convert_type(p & jnp.uint32(0xFFFF0000), F32)
    return jnp.concatenate([lo, hi], axis=-1)


def _sigmoid(x):
    return 1.0 / (1.0 + jnp.exp(-x))


def _silu(x):
    return x * _sigmoid(x)


def _ada_kernel(c_ref, w_ref, b_ref, o_ref):
    sc = _silu(c_ref[...])
    o_ref[0] = jnp.dot(sc, w_ref[0], preferred_element_type=F32,
                       precision=lax.Precision.HIGHEST) + b_ref[0]


def _ada_all(c, w_ada, b_ada):
    n_sub = w_ada.shape[0] * w_ada.shape[1]
    bsz, d = c.shape
    w = w_ada.reshape(n_sub, d, 3 * d)
    b = b_ada.reshape(n_sub, 1, 3 * d)
    tn = 1024
    return pl.pallas_call(
        _ada_kernel,
        grid=(n_sub, 3 * d // tn),
        in_specs=[pl.BlockSpec((bsz, d), lambda s, j: (0, 0)),
                  pl.BlockSpec((1, d, tn), lambda s, j: (s, 0, j)),
                  pl.BlockSpec((1, 1, tn), lambda s, j: (s, 0, j))],
        out_specs=pl.BlockSpec((1, bsz, tn), lambda s, j: (s, 0, j)),
        out_shape=jax.ShapeDtypeStruct((n_sub, bsz, 3 * d), F32),
        compiler_params=_cparams(("parallel", "parallel")),
        name="ada_modulation",
    )(c, w, b)


def _lb_kernel(l_ref, o_ref):
    x = l_ref[...]
    m = jnp.max(x, axis=0, keepdims=True)
    e = jnp.exp(x - m)
    p = e / jnp.sum(e, axis=0, keepdims=True)
    depth = x.shape[0]
    run = jnp.zeros_like(p[0:1])
    rows = []
    for i in range(depth):
        run = run + p[i:i + 1]
        rows.append(run - p[0:1])
    o_ref[...] = jnp.concatenate(rows, axis=0)


def _lower_bounds(lb_logits):
    return pl.pallas_call(
        _lb_kernel,
        out_shape=jax.ShapeDtypeStruct(lb_logits.shape, F32),
        name="hgrn_lower_bounds",
    )(lb_logits)


def _modulate_kernel(x_ref, m_ref, h_ref):
    m = m_ref[0]
    h_ref[0] = (x_ref[0] * (1.0 + m[1:2]) + m[0:1]).astype(BF16)


def _modulate(x, mod):
    bsz, s, d = x.shape
    ts = min(s, 1024)
    return pl.pallas_call(
        _modulate_kernel,
        grid=(bsz, s // ts),
        in_specs=[pl.BlockSpec((1, ts, d), lambda b, i: (b, i, 0)),
                  pl.BlockSpec((1, 3, d), lambda b, i: (b, 0, 0))],
        out_specs=pl.BlockSpec((1, ts, d), lambda b, i: (b, i, 0)),
        out_shape=jax.ShapeDtypeStruct((bsz, s, d), BF16),
        compiler_params=_cparams(("parallel", "parallel")),
        name="modulate_first",
    )(x, mod)


def _inproj_kernel(h_ref, w_ref, o_ref):
    h = h_ref[...]
    n = w_ref.shape[1]
    tn = 512
    for j in range(n // tn):
        o_ref[:, j * tn:(j + 1) * tn] = _dot(h, w_ref[:, j * tn:(j + 1) * tn]).astype(BF16)


def _in_proj(h2d, w_in):
    n, d = h2d.shape
    dout = w_in.shape[1]
    tm = min(n, 512)
    return pl.pallas_call(
        _inproj_kernel,
        grid=(n // tm,),
        in_specs=[pl.BlockSpec((tm, d), lambda i: (i, 0)),
                  pl.BlockSpec((d, dout), lambda i: (0, 0), pipeline_mode=pl.Buffered(1))],
        out_specs=pl.BlockSpec((tm, dout), lambda i: (i, 0)),
        out_shape=jax.ShapeDtypeStruct((n, dout), BF16),
        compiler_params=_cparams(("parallel",)),
        name="in_proj",
    )(h2d, w_in)


def _hgrn_consts():
    t = jnp.arange(HG_UNIT)[:, None]
    s = jnp.arange(HG_UNIT)[None, :]
    tril = (s <= t).astype(F32)
    lv = jnp.full((HG_UNIT, HG_UNIT), -1, jnp.int32)
    for idx, m in enumerate(HG_LEVELS):
        blk = 2 * m
        hit = (t // blk == s // blk) & (t % blk >= m) & (s % blk < m)
        lv = jnp.where(hit, idx, lv)
    return tril, lv


def _hgrn_kernel(q_ref, f_ref, i_ref, g_ref, lb_ref, gain_ref, tril_ref, lv_ref, o_ref,
                 state_ref, kpad_ref, bpad_ref, vpad_ref):
    @pl.when(pl.program_id(2) == 0)
    def _():
        state_ref[...] = jnp.zeros_like(state_ref)

    zpad = jnp.zeros((HG_PAD, HGRN_DH), F32)
    kpad_ref[0:HG_PAD, :] = zpad
    bpad_ref[0:HG_PAD, :] = zpad
    vpad_ref[0:HG_PAD, :] = zpad

    lb = lb_ref[...]
    gain = gain_ref[...]
    n_units = q_ref.shape[1] // HG_UNIT

    def unit(u, carry):
        r0 = pl.multiple_of(u * HG_UNIT, HG_UNIT)
        rows = pl.ds(r0, HG_UNIT)
        qr = q_ref[0, rows, :].astype(F32)
        fl = f_ref[0, rows, :].astype(F32)
        v = i_ref[0, rows, :].astype(F32)
        gg = g_ref[0, rows, :].astype(F32)

        f = lb + (1.0 - lb) * _sigmoid(fl)
        logf = jnp.log(f)
        kk = 1.0 - f
        qq = _silu(qr)
        b = jnp.dot(tril_ref[...], logf, preferred_element_type=F32,
                    precision=lax.Precision.HIGHEST)
        vb = v.astype(BF16)

        state = state_ref[...]
        o = _dot_nt((qq * jnp.exp(b)).astype(BF16), state.astype(BF16))

        lv = lv_ref[...]
        scores = jnp.zeros((HG_UNIT, HG_UNIT), F32)
        for idx, m in enumerate(HG_LEVELS):
            blk = 2 * m
            nb = HG_UNIT // blk
            ref = b.reshape(nb, blk, HGRN_DH)[:, m - 1:m, :]
            ref = jnp.broadcast_to(ref, (nb, blk, HGRN_DH)).reshape(HG_UNIT, HGRN_DH)
            ql = qq * jnp.exp(b - ref)
            kl = kk * jnp.exp(ref - b)
            s_l = _dot_nt(ql.astype(BF16), kl.astype(BF16))
            scores = jnp.where(lv == idx, s_l, scores)
        o = o + _dot(scores.astype(BF16), vb)

        kpad_ref[HG_PAD:, :] = kk
        bpad_ref[HG_PAD:, :] = b
        vpad_ref[HG_PAD:, :] = v
        tmod = jnp.bitwise_and(lax.broadcasted_iota(jnp.int32, (HG_UNIT, HGRN_DH), 0), HG_BAND - 1)
        for d in range(HG_BAND):
            win = pl.ds(HG_PAD - d, HG_UNIT)
            k_d = kpad_ref[win, :]
            b_d = bpad_ref[win, :]
            v_d = vpad_ref[win, :]
            p = qq * k_d * jnp.exp(b - b_d)
            s_d = jnp.sum(p, axis=-1, keepdims=True)
            o = o + jnp.where(tmod >= d, s_d * v_d, 0.0)

        b_last = b[HG_UNIT - 1:HG_UNIT, :]
        kdec = kk * jnp.exp(b_last - b)
        state_ref[...] = state * jnp.exp(b_last) + _dot_tn(vb, kdec.astype(BF16))

        o = o * lax.rsqrt(jnp.mean(o * o, axis=-1, keepdims=True) + RMS_EPS)
        o_ref[0, rows, :] = (o * gain * _silu(gg)).astype(BF16)
        return carry

    lax.fori_loop(0, n_units, unit, 0, unroll=2)


def _hgrn(proj, lb, gain):
    bsz, s, _ = proj.shape
    ts = min(s, 512)
    tril, lv = _hgrn_consts()
    nh = HGRN_HEADS

    def col(off):
        return pl.BlockSpec((1, ts, HGRN_DH), lambda b, h, i, off=off: (b, i, off * nh + h))

    return pl.pallas_call(
        _hgrn_kernel,
        grid=(bsz, nh, s // ts),
        in_specs=[col(0), col(1), col(2), col(3),
                  pl.BlockSpec((1, HGRN_DH), lambda b, h, i: (0, h)),
                  pl.BlockSpec((1, HGRN_DH), lambda b, h, i: (0, h)),
                  pl.BlockSpec((HG_UNIT, HG_UNIT), lambda b, h, i: (0, 0)),
                  pl.BlockSpec((HG_UNIT, HG_UNIT), lambda b, h, i: (0, 0))],
        out_specs=pl.BlockSpec((1, ts, HGRN_DH), lambda b, h, i: (b, i, h)),
        out_shape=jax.ShapeDtypeStruct((bsz, s, D_HGRN), BF16),
        scratch_shapes=[pltpu.VMEM((HGRN_DH, HGRN_DH), F32),
                        pltpu.VMEM((HG_UNIT + HG_PAD, HGRN_DH), F32),
                        pltpu.VMEM((HG_UNIT + HG_PAD, HGRN_DH), F32),
                        pltpu.VMEM((HG_UNIT + HG_PAD, HGRN_DH), F32)],
        compiler_params=_cparams(("parallel", "parallel", "arbitrary")),
        name="hgrn2",
    )(proj, proj, proj, proj, lb, gain, tril, lv)


def _sb_kernel(q_ref, k_ref, v_ref, uo_ref, o_ref, kp_ref, vp_ref, c_ref, acc_ref):
    tq = SB_BLOCK
    nq = q_ref.shape[1] // tq
    pad = (SB_WIN - 1) * tq
    kp_ref[0:pad, :] = jnp.zeros((pad, LANES), BF16)
    vp_ref[0:pad, :] = jnp.zeros((pad, LANES), BF16)
    kp_ref[pad:, :] = k_ref[0]
    vp_ref[pad:, :] = v_ref[0]

    first = lax.broadcasted_iota(jnp.int32, (tq, LANES), 1) < SB_DH
    rowi = lax.broadcasted_iota(jnp.int32, (2 * tq, tq), 0)
    coli = lax.broadcasted_iota(jnp.int32, (2 * tq, tq), 1)
    strict = coli < jnp.bitwise_and(rowi, tq - 1)
    neg_inf = jnp.float32(-jnp.inf)

    def stacked_q(qi):
        rows = pl.ds(pl.multiple_of(qi * tq, tq), tq)
        q = q_ref[0, rows, :] * (SB_DH ** -0.5)
        zero = jnp.zeros_like(q)
        return rows, jnp.concatenate([jnp.where(first, q, zero), jnp.where(first, zero, q)], axis=0)

    def window(q2, p_top, diag, crow, acc):
        start = pl.multiple_of((p_top - (SB_WIN - 1)) * tq, tq)
        kw = kp_ref[pl.ds(start, SB_WIN * tq), :]
        vw = vp_ref[pl.ds(start, SB_WIN * tq), :]
        z = _dot_nt(q2, kw)
        logn = -(jnp.maximum(z, 0.0) + jnp.log(1.0 + jnp.exp(-jnp.abs(z))))
        order = list(reversed(range(SB_WIN)))
        logs, lhs = [], []
        for j in order:
            lj = logn[:, j * tq:(j + 1) * tq]
            if diag and j == SB_WIN - 1:
                lj = jnp.where(strict, lj, 0.0)
            hi = lj.astype(BF16)
            lo = (lj - hi.astype(F32)).astype(BF16)
            logs.append(lj)
            lhs.append(jnp.concatenate([hi, lo], axis=1))
        ct = _dot(jnp.concatenate(lhs, axis=0), uo_ref[...])
        parts = [None] * SB_WIN
        for idx, j in enumerate(order):
            blk = ct[idx * 2 * tq:(idx + 1) * 2 * tq]
            a = jnp.exp(z[:, j * tq:(j + 1) * tq] + logs[idx] + (crow + blk[:, 0:tq]))
            if diag and j == SB_WIN - 1:
                a = jnp.where(strict, a, 0.0)
            parts[j] = a.astype(BF16)
            crow = crow + blk[:, tq:]
        acc = acc + _dot(jnp.concatenate(parts, axis=1), vw)
        return crow, acc

    def emit(rows, acc):
        o_ref[0, rows, :] = jnp.where(first, acc[0:tq], acc[tq:]).astype(BF16)

    def fast(qi, worst):
        rows, q2 = stacked_q(qi)
        zero = jnp.zeros((2 * tq, LANES), F32)
        crow, acc = window(q2, qi + SB_WIN - 1, True, zero, zero)
        emit(rows, acc)
        c_ref[qi] = crow
        acc_ref[qi] = acc
        return jnp.maximum(worst, jnp.where(qi >= SB_WIN, crow, neg_inf))

    worst = lax.fori_loop(0, nq, fast, jnp.full((2 * tq, LANES), neg_inf, F32), unroll=2)

    @pl.when(jnp.max(worst) > -SB_SKIP)
    def _():
        def finish(qi, carry):
            rows, q2 = stacked_q(qi)

            def cond(c):
                p_top, mx, _, _ = c
                return jnp.logical_and(p_top >= SB_WIN - 1, mx > -SB_SKIP)

            def body(c):
                p_top, _, crow, acc = c
                crow, acc = window(q2, p_top, False, crow, acc)
                return p_top - SB_WIN, jnp.max(crow), crow, acc

            crow0 = c_ref[qi]
            _, _, _, acc = lax.while_loop(cond, body, (qi - 1, jnp.max(crow0), crow0, acc_ref[qi]))
            emit(rows, acc)
            return carry

        lax.fori_loop(SB_WIN, nq, finish, 0)


def _stick_breaking(proj):
    bsz, s, _ = proj.shape
    tq = SB_BLOCK
    nq = s // tq
    n_pairs = D_SB // LANES
    base = 4 * D_HGRN // LANES
    jj = jnp.arange(tq)
    upper = (jj[:, None] > jj[None, :]).astype(BF16)
    half = jnp.concatenate([upper, jnp.ones((tq, tq), BF16)], axis=1)
    uo = jnp.concatenate([half, half], axis=0)

    def seq(off):
        return pl.BlockSpec((1, s, LANES), lambda b, p, off=off: (b, 0, base + off * n_pairs + p))

    return pl.pallas_call(
        _sb_kernel,
        grid=(bsz, n_pairs),
        in_specs=[seq(0), seq(1), seq(2),
                  pl.BlockSpec((2 * tq, 2 * tq), lambda b, p: (0, 0))],
        out_specs=pl.BlockSpec((1, s, LANES), lambda b, p: (b, 0, p)),
        out_shape=jax.ShapeDtypeStruct((bsz, s, D_SB), BF16),
        scratch_shapes=[pltpu.VMEM((s + (SB_WIN - 1) * tq, LANES), BF16),
                        pltpu.VMEM((s + (SB_WIN - 1) * tq, LANES), BF16),
                        pltpu.VMEM((nq, 2 * tq, LANES), F32),
                        pltpu.VMEM((nq, 2 * tq, LANES), F32)],
        compiler_params=_cparams(("parallel", "parallel")),
        name="stick_breaking",
    )(proj, proj, proj, uo)


def _residual_ln(x, y, gate, gain, bias, alpha):
    r = alpha * x + gate * y
    mu = jnp.mean(r, axis=-1, keepdims=True)
    rc = r - mu
    var = jnp.mean(rc * rc, axis=-1, keepdims=True)
    return rc * lax.rsqrt(var + LN_EPS) * gain + bias


def _emit(xn, outs, mod_next_ref, packed=False):
    outs[0][0] = xn
    if mod_next_ref is not None:
        mn = mod_next_ref[0]
        hn = xn * (1.0 + mn[1:2]) + mn[0:1]
        outs[1][0] = _pack_rows(hn) if packed else hn.astype(BF16)


def _outproj_kernel(alpha, has_next, packed, oh_ref, os_ref, x_ref, w_ref, mod_ref, ln_ref, *rest):
    mod_next_ref = rest[0] if has_next else None
    outs = rest[1:] if has_next else rest
    y = _dot(oh_ref[0], w_ref[0:D_HGRN, :]) + _dot(os_ref[0], w_ref[D_HGRN:, :])
    ln = ln_ref[...]
    xn = _residual_ln(x_ref[0], y, mod_ref[0][2:3], ln[0:1], ln[1:2], alpha)
    _emit(xn, outs, mod_next_ref, packed)


def _row_specs(bsz, s, d, tm, mod_next, packed=False):
    x_spec = pl.BlockSpec((1, tm, d), lambda b, i: (b, i, 0))
    mod_spec = pl.BlockSpec((1, 3, d), lambda b, i: (b, 0, 0))
    ln_spec = pl.BlockSpec((2, d), lambda b, i: (0, 0))
    out_specs = [pl.BlockSpec((1, tm, d), lambda b, i: (b, i, 0))]
    out_shape = [jax.ShapeDtypeStruct((bsz, s, d), F32)]
    if mod_next is not None:
        dh, dt = (d // 2, U32) if packed else (d, BF16)
        out_specs.append(pl.BlockSpec((1, tm, dh), lambda b, i: (b, i, 0)))
        out_shape.append(jax.ShapeDtypeStruct((bsz, s, dh), dt))
    return x_spec, mod_spec, ln_spec, out_specs, out_shape


def _out_proj_ln(o_h, o_s, x, w_out, mod, ln, mod_next, alpha, packed):
    bsz, s, d = x.shape
    tm = min(s, 512)
    x_spec, mod_spec, ln_spec, out_specs, out_shape = _row_specs(bsz, s, d, tm, mod_next, packed)
    in_specs = [pl.BlockSpec((1, tm, D_HGRN), lambda b, i: (b, i, 0)),
                pl.BlockSpec((1, tm, D_SB), lambda b, i: (b, i, 0)),
                x_spec,
                pl.BlockSpec((d, d), lambda b, i: (0, 0), pipeline_mode=pl.Buffered(1)),
                mod_spec, ln_spec]
    args = [o_h, o_s, x, w_out, mod, ln]
    if mod_next is not None:
        in_specs.append(mod_spec)
        args.append(mod_next)
    return pl.pallas_call(
        functools.partial(_outproj_kernel, alpha, mod_next is not None, packed),
        grid=(bsz, s // tm),
        in_specs=in_specs, out_specs=out_specs, out_shape=out_shape,
        compiler_params=_cparams(("parallel", "parallel")),
        name="out_proj_ln",
    )(*args)


def _dense_kernel(alpha, has_next, n_chunks, h_ref, x_ref, wg_ref, wu_ref, wd_ref, mod_ref, ln_ref, *rest):
    mod_next_ref = rest[0] if has_next else None
    outs = rest[1:] if has_next else rest
    h = h_ref[0]
    ff = wg_ref.shape[1]
    tf = ff // n_chunks
    y = None
    for j in range(n_chunks):
        cols = slice(j * tf, (j + 1) * tf)
        g = _dot(h, wg_ref[:, cols])
        u = _dot(h, wu_ref[:, cols])
        part = _dot((_silu(g) * u).astype(BF16), wd_ref[cols, :])
        y = part if y is None else y + part
    ln = ln_ref[...]
    xn = _residual_ln(x_ref[0], y, mod_ref[0][2:3], ln[0:1], ln[1:2], alpha)
    _emit(xn, outs, mod_next_ref)


def _dense_ffn_ln(h, x, wg, wu, wd, mod, ln, mod_next, alpha):
    bsz, s, d = x.shape
    ff = wg.shape[1]
    tm = min(s, 512)
    x_spec, mod_spec, ln_spec, out_specs, out_shape = _row_specs(bsz, s, d, tm, mod_next)
    once = pl.Buffered(1)
    in_specs = [pl.BlockSpec((1, tm, d), lambda b, i: (b, i, 0)),
                x_spec,
                pl.BlockSpec((d, ff), lambda b, i: (0, 0), pipeline_mode=once),
                pl.BlockSpec((d, ff), lambda b, i: (0, 0), pipeline_mode=once),
                pl.BlockSpec((ff, d), lambda b, i: (0, 0), pipeline_mode=once),
                mod_spec, ln_spec]
    args = [h, x, wg, wu, wd, mod, ln]
    if mod_next is not None:
        in_specs.append(mod_spec)
        args.append(mod_next)
    n_chunks = 2 if (ff // 2) % LANES == 0 else 1
    return pl.pallas_call(
        functools.partial(_dense_kernel, alpha, mod_next is not None, n_chunks),
        grid=(bsz, s // tm),
        in_specs=in_specs, out_specs=out_specs, out_shape=out_shape,
        compiler_params=_cparams(("parallel", "parallel")),
        name="dense_ffn_ln",
    )(*args)


def _router_kernel(h_ref, w_ref, tri_ref, route_ref, count_ref, run_ref):
    @pl.when(pl.program_id(0) == 0)
    def _():
        run_ref[...] = jnp.zeros_like(run_ref)

    tm = h_ref.shape[0]
    lane = lax.broadcasted_iota(jnp.int32, (tm, LANES), 1)
    logits = _dot(_unpack_rows(h_ref[...]).astype(BF16), w_ref[...])
    neg = jnp.float32(-jnp.inf)
    logits = jnp.where(lane < N_EXPERTS, logits, neg)
    m1 = jnp.max(logits, axis=-1, keepdims=True)
    i1 = jnp.min(jnp.where(logits == m1, lane, LANES), axis=-1, keepdims=True)
    rest = jnp.where(lane == i1, neg, logits)
    m2 = jnp.max(rest, axis=-1, keepdims=True)
    i2 = jnp.min(jnp.where(rest == m2, lane, LANES), axis=-1, keepdims=True)
    e21 = jnp.exp(m2 - m1)
    w1 = 1.0 / (1.0 + e21)
    w2 = e21 / (1.0 + e21)

    hot1 = lane == i1
    hot2 = lane == i2
    both = jnp.where(jnp.logical_or(hot1, hot2), 1.0, 0.0)
    before = _dot(tri_ref[...], both.astype(BF16)) + run_ref[...]
    r1 = jnp.sum(jnp.where(hot1, before, 0.0), axis=-1, keepdims=True)
    r2 = jnp.sum(jnp.where(hot2, before, 0.0), axis=-1, keepdims=True)
    run_ref[...] = run_ref[...] + jnp.sum(both, axis=0, keepdims=True)
    count_ref[...] = run_ref[...]

    cols = [i1.astype(F32), i2.astype(F32), r1, r2, w1, w2]
    route = jnp.zeros((tm, LANES), F32)
    for c, val in enumerate(cols):
        route = jnp.where(lane == c, val, route)
    route_ref[...] = route[:, 0:8]


def _router(hp, w_router):
    n, dp = hp.shape
    d = 2 * dp
    tm = min(n, 512)
    w_pad = jnp.zeros((d, LANES), BF16).at[:, :N_EXPERTS].set(w_router.astype(BF16))
    jj = jnp.arange(tm)
    tri = (jj[None, :] < jj[:, None]).astype(BF16)
    return pl.pallas_call(
        _router_kernel,
        grid=(n // tm,),
        in_specs=[pl.BlockSpec((tm, dp), lambda i: (i, 0)),
                  pl.BlockSpec((d, LANES), lambda i: (0, 0)),
                  pl.BlockSpec((tm, tm), lambda i: (0, 0))],
        out_specs=[pl.BlockSpec((tm, 8), lambda i: (i, 0)),
                   pl.BlockSpec((1, LANES), lambda i: (0, 0))],
        out_shape=[jax.ShapeDtypeStruct((n, 8), F32),
                   jax.ShapeDtypeStruct((1, LANES), F32)],
        scratch_shapes=[pltpu.VMEM((1, LANES), F32)],
        compiler_params=_cparams(("arbitrary",)),
        name="moe_router",
    )(hp, w_pad, tri)


def _gmm_kernel(te_ref, nt_ref, x_ref, wg_ref, wu_ref, wd_ref, o_ref, acc_ref, xb_ref):
    i = pl.program_id(0)
    j = pl.program_id(1)
    nj = pl.num_programs(1)
    live = i < nt_ref[0]

    @pl.when(jnp.logical_and(live, j == 0))
    def _():
        acc_ref[...] = jnp.zeros_like(acc_ref)
        xb_ref[...] = _unpack_rows(x_ref[...]).astype(BF16)

    @pl.when(live)
    def _():
        x = xb_ref[...]
        g = _dot(x, wg_ref[0])
        u = _dot(x, wu_ref[0])
        acc_ref[...] += _dot((_silu(g) * u).astype(BF16), wd_ref[0])

    @pl.when(jnp.logical_and(live, j == nj - 1))
    def _():
        o_ref[...] = _pack_rows(acc_ref[...])

    @pl.when(jnp.logical_and(jnp.logical_not(live), j == nj - 1))
    def _():
        o_ref[...] = jnp.zeros_like(o_ref)


def _grouped_swiglu(xs, tile_expert, n_tiles_used, wg, wu, wd):
    n_slots, dp = xs.shape
    d = 2 * dp
    ff = wg.shape[2]
    tm, tf = MOE_TILE, MOE_FF_TILE
    n_tiles = n_slots // tm

    def row_map(i, j, te, nt):
        return (jnp.minimum(i, nt[0] - 1), 0)

    def ff_block(i, j, te, nt):
        return jnp.where(i < nt[0], j, ff // tf - 1)

    grid_spec = pltpu.PrefetchScalarGridSpec(
        num_scalar_prefetch=2,
        grid=(n_tiles, ff // tf),
        in_specs=[pl.BlockSpec((tm, dp), row_map),
                  pl.BlockSpec((1, d, tf), lambda i, j, te, nt: (te[i], 0, ff_block(i, j, te, nt))),
                  pl.BlockSpec((1, d, tf), lambda i, j, te, nt: (te[i], 0, ff_block(i, j, te, nt))),
                  pl.BlockSpec((1, tf, d), lambda i, j, te, nt: (te[i], ff_block(i, j, te, nt), 0))],
        out_specs=pl.BlockSpec((tm, dp), lambda i, j, te, nt: (i, 0)),
        scratch_shapes=[pltpu.VMEM((tm, d), F32), pltpu.VMEM((tm, d), BF16)],
    )
    return pl.pallas_call(
        _gmm_kernel,
        grid_spec=grid_spec,
        out_shape=jax.ShapeDtypeStruct((n_slots, dp), U32),
        compiler_params=_cparams(("arbitrary", "arbitrary")),
        name="moe_grouped_swiglu",
    )(tile_expert, n_tiles_used, xs, wg, wu, wd)


def _combine_kernel(alpha, has_next, ya_ref, yb_ref, r_ref, x_ref, mod_ref, ln_ref, *rest):
    mod_next_ref = rest[0] if has_next else None
    outs = rest[1:] if has_next else rest
    r = r_ref[0]
    y = r[:, 4:5] * _unpack_rows(ya_ref[0, 0]) + r[:, 5:6] * _unpack_rows(yb_ref[0, 0])
    ln = ln_ref[...]
    xn = _residual_ln(x_ref[0], y, mod_ref[0][2:3], ln[0:1], ln[1:2], alpha)
    _emit(xn, outs, mod_next_ref)


def _combine_ln(yab, route, x, mod, ln, mod_next, alpha):
    bsz, s, d = x.shape
    tm = min(s, 512)
    x_spec, mod_spec, ln_spec, out_specs, out_shape = _row_specs(bsz, s, d, tm, mod_next)
    yab = yab.reshape(2, bsz, s, d // 2)
    in_specs = [pl.BlockSpec((1, 1, tm, d // 2), lambda b, i: (0, b, i, 0)),
                pl.BlockSpec((1, 1, tm, d // 2), lambda b, i: (1, b, i, 0)),
                pl.BlockSpec((1, tm, 8), lambda b, i: (b, i, 0)),
                x_spec, mod_spec, ln_spec]
    args = [yab, yab, route.reshape(bsz, s, 8), x, mod, ln]
    if mod_next is not None:
        in_specs.append(mod_spec)
        args.append(mod_next)
    return pl.pallas_call(
        functools.partial(_combine_kernel, alpha, mod_next is not None),
        grid=(bsz, s // tm),
        in_specs=in_specs, out_specs=out_specs, out_shape=out_shape,
        compiler_params=_cparams(("parallel", "parallel")),
        name="moe_combine_ln",
    )(*args)


def _sc_mesh():
    return plsc.VectorSubcoreMesh(core_axis_name="c", subcore_axis_name="s")


def _dispatch_rows(rows, idx, n_out):
    m, w = rows.shape

    @pl.kernel(out_type=jax.ShapeDtypeStruct((n_out, w), rows.dtype), mesh=_sc_mesh(), scratch_types=[])
    def scatter(rows_hbm, idx_hbm, out_hbm):
        def body(rows_vmem, idx0_vmem, idx1_vmem):
            pltpu.sync_copy(rows_vmem, out_hbm.at[idx0_vmem.at[0]])
            pltpu.sync_copy(rows_vmem, out_hbm.at[idx1_vmem.at[0]])

        pltpu.emit_pipeline(
            body, grid=(m // SC_WINDOW,),
            in_specs=[pl.BlockSpec((SC_WINDOW, w), lambda i: (i, 0)),
                      pl.BlockSpec((1, SC_WINDOW), lambda i: (0, i)),
                      pl.BlockSpec((1, SC_WINDOW), lambda i: (1, i))],
            out_specs=[],
            core_axis_name=("c", "s"),
            dimension_semantics=(pltpu.PARALLEL,),
        )(rows_hbm, idx_hbm, idx_hbm)

    return scatter(rows, idx)


def _collect_rows(rows, idx):
    m = idx.shape[1]
    w = rows.shape[1]

    @pl.kernel(out_type=jax.ShapeDtypeStruct((m, w), rows.dtype), mesh=_sc_mesh(), scratch_types=[])
    def gather(rows_hbm, idx_hbm, out_hbm):
        def body(idx_vmem, out_vmem):
            pltpu.sync_copy(rows_hbm.at[idx_vmem.at[0]], out_vmem)

        pltpu.emit_pipeline(
            body, grid=(m // SC_WINDOW,),
            in_specs=[pl.BlockSpec((1, SC_WINDOW), lambda i: (0, i))],
            out_specs=[pl.BlockSpec((SC_WINDOW, w), lambda i: (i, 0))],
            core_axis_name=("c", "s"),
            dimension_semantics=(pltpu.PARALLEL,),
        )(idx_hbm, out_hbm)

    return gather(rows, idx)


def _moe_ffn_ln(h, x, w_router, wg, wu, wd, mod, ln, mod_next, alpha):
    bsz, s, d = x.shape
    n = bsz * s
    hp = h.reshape(n, d // 2)
    route, counts = _router(hp, w_router)

    tm = MOE_TILE
    cnt = counts[0, :N_EXPERTS].astype(jnp.int32)
    padded = ((cnt + tm - 1) // tm) * tm
    ends = jnp.cumsum(padded)
    offs = ends - padded
    e = route[:, 0:2].astype(jnp.int32)
    rank = route[:, 2:4].astype(jnp.int32)
    pos = (jnp.take(offs, e) + rank).T
    n_tiles = (2 * n) // tm + N_EXPERTS
    n_slots = n_tiles * tm
    tile_start = jnp.arange(n_tiles, dtype=jnp.int32) * tm
    tile_expert = jnp.minimum(jnp.sum(tile_start[:, None] >= ends[None, :], axis=1),
                              N_EXPERTS - 1).astype(jnp.int32)
    n_used = (ends[-1] // tm).astype(jnp.int32).reshape(1)

    idx = (2 * pos[:, :, None] + jnp.arange(2, dtype=jnp.int32)).reshape(2, 2 * n)
    xs = _dispatch_rows(hp.reshape(2 * n, d // 4), idx, 2 * n_slots).reshape(n_slots, d // 2)
    ys = _grouped_swiglu(xs, tile_expert, n_used, wg, wu, wd)
    yab = _collect_rows(ys.reshape(2 * n_slots, d // 4), idx.reshape(1, 4 * n)).reshape(2, n, d // 2)
    return _combine_ln(yab, route, x, mod, ln, mod_next, alpha)


def kernel(x, c, w_ada, b_ada, w_in, w_out, hgrn_lb_logits, hgrn_norm_gain,
           w_dense_gate, w_dense_up, w_dense_down, w_router, w_moe_gate, w_moe_up, w_moe_down,
           ln_gain, ln_bias):
    depth = w_in.shape[0]
    bsz, s, d = x.shape
    alpha = (2 * depth) ** 0.25

    mods = _ada_all(c, w_ada, b_ada).reshape(depth, 2, bsz, 3, d)
    lb_all = _lower_bounds(hgrn_lb_logits)
    ln_all = jnp.stack([ln_gain, ln_bias], axis=2)

    h = _modulate(x, mods[0, 0])
    for layer in range(depth):
        proj = _in_proj(h.reshape(bsz * s, d), w_in[layer].astype(BF16)).reshape(bsz, s, D_IN_PROJ)
        o_h = _hgrn(proj, lb_all[layer:layer + 1], hgrn_norm_gain[layer:layer + 1])
        o_s = _stick_breaking(proj)
        x, h = _out_proj_ln(o_h, o_s, x, w_out[layer].astype(BF16), mods[layer, 0],
                            ln_all[layer, 0], mods[layer, 1], alpha, packed=layer % 2 == 1)
        mod_next = mods[layer + 1, 0] if layer + 1 < depth else None
        j = layer // 2
        if layer % 2 == 0:
            res = _dense_ffn_ln(h, x, w_dense_gate[j].astype(BF16), w_dense_up[j].astype(BF16),
                                w_dense_down[j].astype(BF16), mods[layer, 1], ln_all[layer, 1],
                                mod_next, alpha)
        else:
            res = _moe_ffn_ln(h, x, w_router[j], w_moe_gate[j].astype(BF16), w_moe_up[j].astype(BF16),
                              w_moe_down[j].astype(BF16), mods[layer, 1], ln_all[layer, 1],
                              mod_next, alpha)
        if mod_next is not None:
            x, h = res
        else:
            (x,) = res
    return x
```

```python
import functools
import math

import jax
import jax.numpy as jnp
from jax import lax
from jax.experimental import pallas as pl
from jax.experimental.pallas import tpu as pltpu
from jax.experimental.pallas import tpu_sc as plsc

D_MODEL = 1024
D_HGRN = 512
HGRN_HEADS = 4
HGRN_DH = 128
D_SB = 512
SB_HEADS = 8
SB_DH = 64
D_IN_PROJ = 4 * D_HGRN + 3 * D_SB
N_EXPERTS = 8
LN_EPS = 1e-5
RMS_EPS = 1e-6

LANES = 128
VMEM_LIMIT = 52 * 1024 * 1024

HG_UNIT = 128
HG_BAND = 4
HG_PAD = 8
HG_LEVELS = (64, 32, 16, 8, 4)
SB_BLOCK = 128
SB_SKIP = 110.0
SB_WIN = 3
MOE_TILE = 1024
MOE_FF_TILE = 512
SC_WINDOW = 128

F32 = jnp.float32
BF16 = jnp.bfloat16
U32 = jnp.uint32


def _cparams(sem):
    return pltpu.CompilerParams(dimension_semantics=sem, vmem_limit_bytes=VMEM_LIMIT)


def _dot(a, b):
    return jnp.dot(a, b, preferred_element_type=F32)


def _dot_nt(a, b):
    return lax.dot_general(a, b, (((1,), (1,)), ((), ())), preferred_element_type=F32)


def _dot_tn(a, b):
    return lax.dot_general(a, b, (((0,), (0,)), ((), ())), preferred_element_type=F32)


def _pack_rows(v):
    w = v.shape[-1] // 2
    lo = lax.bitcast_convert_type(v[:, :w].astype(BF16).astype(F32), U32)
    hi = lax.bitcast_convert_type(v[:, w:].astype(BF16).astype(F32), U32)
    return (lo >> 16) | hi


def _unpack_rows(p):
    lo = lax.bitcast_convert_type(p << 16, F32)
    hi = lax.bitcast_convert_type(p & jnp.uint32(0xFFFF0000), F32)
    return jnp.concatenate([lo, hi], axis=-1)


def _store_planes(ref, p):
    w = p.shape[-1] // 2
    ref[0] = p[:, :w].reshape(ref.shape[1:])
    ref[1] = p[:, w:].reshape(ref.shape[1:])


def _load_planes(p0, p1):
    return _unpack_rows(jnp.concatenate([p0, p1], axis=-1))


def _sigmoid(x):
    return 1.0 / (1.0 + jnp.exp(-x))


def _silu(x):
    return x * _sigmoid(x)


def _ada_kernel(c_ref, w_ref, b_ref, o_ref):
    sc = _silu(c_ref[...])
    o_ref[0] = jnp.dot(sc, w_ref[0], preferred_element_type=F32,
                       precision=lax.Precision.HIGHEST) + b_ref[0]


def _ada_all(c, w_ada, b_ada):
    n_sub = w_ada.shape[0] * w_ada.shape[1]
    bsz, d = c.shape
    w = w_ada.reshape(n_sub, d, 3 * d)
    b = b_ada.reshape(n_sub, 1, 3 * d)
    tn = 1024
    return pl.pallas_call(
        _ada_kernel,
        grid=(n_sub, 3 * d // tn),
        in_specs=[pl.BlockSpec((bsz, d), lambda s, j: (0, 0)),
                  pl.BlockSpec((1, d, tn), lambda s, j: (s, 0, j)),
                  pl.BlockSpec((1, 1, tn), lambda s, j: (s, 0, j))],
        out_specs=pl.BlockSpec((1, bsz, tn), lambda s, j: (s, 0, j)),
        out_shape=jax.ShapeDtypeStruct((n_sub, bsz, 3 * d), F32),
        compiler_params=_cparams(("parallel", "parallel")),
        name="ada_modulation",
    )(c, w, b)


def _lb_kernel(l_ref, o_ref):
    x = l_ref[...]
    m = jnp.max(x, axis=0, keepdims=True)
    e = jnp.exp(x - m)
    p = e / jnp.sum(e, axis=0, keepdims=True)
    depth = x.shape[0]
    run = jnp.zeros_like(p[0:1])
    rows = []
    for i in range(depth):
        run = run + p[i:i + 1]
        rows.append(run - p[0:1])
    o_ref[...] = jnp.concatenate(rows, axis=0)


def _lower_bounds(lb_logits):
    return pl.pallas_call(
        _lb_kernel,
        out_shape=jax.ShapeDtypeStruct(lb_logits.shape, F32),
        name="hgrn_lower_bounds",
    )(lb_logits)


def _modulate_kernel(x_ref, m_ref, h_ref):
    m = m_ref[0]
    h_ref[0] = (x_ref[0] * (1.0 + m[1:2]) + m[0:1]).astype(BF16)


def _modulate(x, mod):
    bsz, s, d = x.shape
    ts = min(s, 1024)
    return pl.pallas_call(
        _modulate_kernel,
        grid=(bsz, s // ts),
        in_specs=[pl.BlockSpec((1, ts, d), lambda b, i: (b, i, 0)),
                  pl.BlockSpec((1, 3, d), lambda b, i: (b, 0, 0))],
        out_specs=pl.BlockSpec((1, ts, d), lambda b, i: (b, i, 0)),
        out_shape=jax.ShapeDtypeStruct((bsz, s, d), BF16),
        compiler_params=_cparams(("parallel", "parallel")),
        name="modulate_first",
    )(x, mod)


def _inproj_kernel(h_ref, w_ref, o_ref):
    h = h_ref[...]
    n = w_ref.shape[1]
    tn = 512
    for j in range(n // tn):
        o_ref[:, j * tn:(j + 1) * tn] = _dot(h, w_ref[:, j * tn:(j + 1) * tn]).astype(BF16)


def _in_proj(h2d, w_in):
    n, d = h2d.shape
    dout = w_in.shape[1]
    tm = min(n, 512)
    return pl.pallas_call(
        _inproj_kernel,
        grid=(n // tm,),
        in_specs=[pl.BlockSpec((tm, d), lambda i: (i, 0)),
                  pl.BlockSpec((d, dout), lambda i: (0, 0), pipeline_mode=pl.Buffered(1))],
        out_specs=pl.BlockSpec((tm, dout), lambda i: (i, 0)),
        out_shape=jax.ShapeDtypeStruct((n, dout), BF16),
        compiler_params=_cparams(("parallel",)),
        name="in_proj",
    )(h2d, w_in)


def _hgrn_consts():
    t = jnp.arange(HG_UNIT)[:, None]
    s = jnp.arange(HG_UNIT)[None, :]
    tril = (s <= t).astype(F32)
    lv = jnp.full((HG_UNIT, HG_UNIT), -1, jnp.int32)
    for idx, m in enumerate(HG_LEVELS):
        blk = 2 * m
        hit = (t // blk == s // blk) & (t % blk >= m) & (s % blk < m)
        lv = jnp.where(hit, idx, lv)
    return tril, lv


def _hgrn_kernel(q_ref, f_ref, i_ref, g_ref, lb_ref, gain_ref, tril_ref, lv_ref, o_ref,
                 state_ref, kpad_ref, bpad_ref, vpad_ref):
    @pl.when(pl.program_id(2) == 0)
    def _():
        state_ref[...] = jnp.zeros_like(state_ref)

    zpad = jnp.zeros((HG_PAD, HGRN_DH), F32)
    kpad_ref[0:HG_PAD, :] = zpad
    bpad_ref[0:HG_PAD, :] = zpad
    vpad_ref[0:HG_PAD, :] = zpad

    lb = lb_ref[...]
    gain = gain_ref[...]
    n_units = q_ref.shape[1] // HG_UNIT
    rowi = lax.broadcasted_iota(jnp.int32, (HG_UNIT, HGRN_DH), 0)

    def unit(u, carry):
        r0 = pl.multiple_of(u * HG_UNIT, HG_UNIT)
        rows = pl.ds(r0, HG_UNIT)
        qr = q_ref[0, rows, :].astype(F32)
        fl = f_ref[0, rows, :].astype(F32)
        v = i_ref[0, rows, :].astype(F32)
        gg = g_ref[0, rows, :].astype(F32)

        f = lb + (1.0 - lb) * _sigmoid(fl)
        logf = jnp.log(f)
        kk = 1.0 - f
        qq = _silu(qr)
        b = jnp.dot(tril_ref[...], logf, preferred_element_type=F32,
                    precision=lax.Precision.HIGHEST)
        vb = v.astype(BF16)

        state = state_ref[...]
        o = _dot_nt((qq * jnp.exp(b)).astype(BF16), state.astype(BF16))

        lv = lv_ref[...]
        scores = jnp.zeros((HG_UNIT, HG_UNIT), F32)
        for idx, m in enumerate(HG_LEVELS):
            blk = 2 * m
            nb = HG_UNIT // blk
            ref = b.reshape(nb, blk, HGRN_DH)[:, m - 1:m, :]
            ref = jnp.broadcast_to(ref, (nb, blk, HGRN_DH)).reshape(HG_UNIT, HGRN_DH)
            upper = jnp.bitwise_and(rowi, m) != 0
            xl = (jnp.where(upper, qq, kk) * jnp.exp(-jnp.abs(b - ref))).astype(BF16)
            s_l = _dot_nt(xl, xl)
            scores = jnp.where(lv == idx, s_l, scores)
        o = o + _dot(scores.astype(BF16), vb)

        kpad_ref[HG_PAD:, :] = kk
        bpad_ref[HG_PAD:, :] = b
        vpad_ref[HG_PAD:, :] = v
        tmod = jnp.bitwise_and(rowi, HG_BAND - 1)
        for d in range(HG_BAND):
            win = pl.ds(HG_PAD - d, HG_UNIT)
            k_d = kpad_ref[win, :]
            b_d = bpad_ref[win, :]
            v_d = vpad_ref[win, :]
            p = qq * k_d * jnp.exp(b - b_d)
            s_d = jnp.sum(p, axis=-1, keepdims=True)
            o = o + jnp.where(tmod >= d, s_d * v_d, 0.0)

        b_last = b[HG_UNIT - 1:HG_UNIT, :]
        kdec = kk * jnp.exp(b_last - b)
        state_ref[...] = state * jnp.exp(b_last) + _dot_tn(vb, kdec.astype(BF16))

        o = o * lax.rsqrt(jnp.mean(o * o, axis=-1, keepdims=True) + RMS_EPS)
        o_ref[0, rows, :] = (o * gain * _silu(gg)).astype(BF16)
        return carry

    lax.fori_loop(0, n_units, unit, 0, unroll=2)


def _hgrn(proj, lb, gain):
    bsz, s, _ = proj.shape
    ts = min(s, 512)
    tril, lv = _hgrn_consts()
    nh = HGRN_HEADS

    def col(off):
        return pl.BlockSpec((1, ts, HGRN_DH), lambda b, h, i, off=off: (b, i, off * nh + h))

    return pl.pallas_call(
        _hgrn_kernel,
        grid=(bsz, nh, s // ts),
        in_specs=[col(0), col(1), col(2), col(3),
                  pl.BlockSpec((1, HGRN_DH), lambda b, h, i: (0, h)),
                  pl.BlockSpec((1, HGRN_DH), lambda b, h, i: (0, h)),
                  pl.BlockSpec((HG_UNIT, HG_UNIT), lambda b, h, i: (0, 0)),
                  pl.BlockSpec((HG_UNIT, HG_UNIT), lambda b, h, i: (0, 0))],
        out_specs=pl.BlockSpec((1, ts, HGRN_DH), lambda b, h, i: (b, i, h)),
        out_shape=jax.ShapeDtypeStruct((bsz, s, D_HGRN), BF16),
        scratch_shapes=[pltpu.VMEM((HGRN_DH, HGRN_DH), F32),
                        pltpu.VMEM((HG_UNIT + HG_PAD, HGRN_DH), F32),
                        pltpu.VMEM((HG_UNIT + HG_PAD, HGRN_DH), F32),
                        pltpu.VMEM((HG_UNIT + HG_PAD, HGRN_DH), F32)],
        compiler_params=_cparams(("parallel", "parallel", "arbitrary")),
        name="hgrn2",
    )(proj, proj, proj, proj, lb, gain, tril, lv)


def _sb_kernel(q_ref, k_ref, v_ref, uo_ref, o_ref, kp_ref, vp_ref, c_ref, acc_ref):
    tq = SB_BLOCK
    nq = q_ref.shape[1] // tq
    pad = (SB_WIN - 1) * tq
    kp_ref[0:pad, :] = jnp.zeros((pad, LANES), BF16)
    vp_ref[0:pad, :] = jnp.zeros((pad, LANES), BF16)
    kp_ref[pad:, :] = k_ref[0]
    vp_ref[pad:, :] = v_ref[0]

    first = lax.broadcasted_iota(jnp.int32, (tq, LANES), 1) < SB_DH
    rowi = lax.broadcasted_iota(jnp.int32, (2 * tq, tq), 0)
    coli = lax.broadcasted_iota(jnp.int32, (2 * tq, tq), 1)
    strict = coli < jnp.bitwise_and(rowi, tq - 1)
    neg_inf = jnp.float32(-jnp.inf)

    def stacked_q(qi):
        rows = pl.ds(pl.multiple_of(qi * tq, tq), tq)
        q = q_ref[0, rows, :] * (SB_DH ** -0.5)
        zero = jnp.zeros_like(q)
        return rows, jnp.concatenate([jnp.where(first, q, zero), jnp.where(first, zero, q)], axis=0)

    def window(q2, p_top, diag, crow, acc):
        start = pl.multiple_of((p_top - (SB_WIN - 1)) * tq, tq)
        kw = kp_ref[pl.ds(start, SB_WIN * tq), :]
        vw = vp_ref[pl.ds(start, SB_WIN * tq), :]
        z = _dot_nt(q2, kw)
        logn = -(jnp.maximum(z, 0.0) + jnp.log(1.0 + jnp.exp(-jnp.abs(z))))
        order = list(reversed(range(SB_WIN)))
        logs, lhs = [], []
        for j in order:
            lj = logn[:, j * tq:(j + 1) * tq]
            if diag and j == SB_WIN - 1:
                lj = jnp.where(strict, lj, 0.0)
            hi = lj.astype(BF16)
            lo = (lj - hi.astype(F32)).astype(BF16)
            logs.append(lj)
            lhs.append(jnp.concatenate([hi, lo], axis=1))
        ct = _dot(jnp.concatenate(lhs, axis=0), uo_ref[...])
        parts = [None] * SB_WIN
        for idx, j in enumerate(order):
            blk = ct[idx * 2 * tq:(idx + 1) * 2 * tq]
            a = jnp.exp(z[:, j * tq:(j + 1) * tq] + logs[idx] + (crow + blk[:, 0:tq]))
            if diag and j == SB_WIN - 1:
                a = jnp.where(strict, a, 0.0)
            parts[j] = a.astype(BF16)
            crow = crow + blk[:, tq:]
        acc = acc + _dot(jnp.concatenate(parts, axis=1), vw)
        return crow, acc

    def emit(rows, acc):
        o_ref[0, rows, :] = jnp.where(first, acc[0:tq], acc[tq:]).astype(BF16)

    def fast(qi, worst):
        rows, q2 = stacked_q(qi)
        zero = jnp.zeros((2 * tq, LANES), F32)
        crow, acc = window(q2, qi + SB_WIN - 1, True, zero, zero)
        emit(rows, acc)
        c_ref[qi] = crow
        acc_ref[qi] = acc
        return jnp.maximum(worst, jnp.where(qi >= SB_WIN, crow, neg_inf))

    worst = lax.fori_loop(0, nq, fast, jnp.full((2 * tq, LANES), neg_inf, F32), unroll=2)

    @pl.when(jnp.max(worst) > -SB_SKIP)
    def _():
        def finish(qi, carry):
            rows, q2 = stacked_q(qi)

            def cond(c):
                p_top, mx, _, _ = c
                return jnp.logical_and(p_top >= SB_WIN - 1, mx > -SB_SKIP)

            def body(c):
                p_top, _, crow, acc = c
                crow, acc = window(q2, p_top, False, crow, acc)
                return p_top - SB_WIN, jnp.max(crow), crow, acc

            crow0 = c_ref[qi]
            _, _, _, acc = lax.while_loop(cond, body, (qi - 1, jnp.max(crow0), crow0, acc_ref[qi]))
            emit(rows, acc)
            return carry

        lax.fori_loop(SB_WIN, nq, finish, 0)


def _stick_breaking(proj):
    bsz, s, _ = proj.shape
    tq = SB_BLOCK
    nq = s // tq
    n_pairs = D_SB // LANES
    base = 4 * D_HGRN // LANES
    jj = jnp.arange(tq)
    upper = (jj[:, None] > jj[None, :]).astype(BF16)
    half = jnp.concatenate([upper, jnp.ones((tq, tq), BF16)], axis=1)
    uo = jnp.concatenate([half, half], axis=0)

    def seq(off):
        return pl.BlockSpec((1, s, LANES), lambda b, p, off=off: (b, 0, base + off * n_pairs + p))

    return pl.pallas_call(
        _sb_kernel,
        grid=(bsz, n_pairs),
        in_specs=[seq(0), seq(1), seq(2),
                  pl.BlockSpec((2 * tq, 2 * tq), lambda b, p: (0, 0))],
        out_specs=pl.BlockSpec((1, s, LANES), lambda b, p: (b, 0, p)),
        out_shape=jax.ShapeDtypeStruct((bsz, s, D_SB), BF16),
        scratch_shapes=[pltpu.VMEM((s + (SB_WIN - 1) * tq, LANES), BF16),
                        pltpu.VMEM((s + (SB_WIN - 1) * tq, LANES), BF16),
                        pltpu.VMEM((nq, 2 * tq, LANES), F32),
                        pltpu.VMEM((nq, 2 * tq, LANES), F32)],
        compiler_params=_cparams(("parallel", "parallel")),
        name="stick_breaking",
    )(proj, proj, proj, uo)


def _residual_ln(x, y, gate, gain, bias, alpha):
    r = alpha * x + gate * y
    mu = jnp.mean(r, axis=-1, keepdims=True)
    rc = r - mu
    var = jnp.mean(rc * rc, axis=-1, keepdims=True)
    return rc * lax.rsqrt(var + LN_EPS) * gain + bias


def _emit(xn, outs, mod_next_ref, packed=False):
    outs[0][0] = xn
    if mod_next_ref is not None:
        mn = mod_next_ref[0]
        hn = xn * (1.0 + mn[1:2]) + mn[0:1]
        if packed:
            _store_planes(outs[1], _pack_rows(hn))
        else:
            outs[1][0] = hn.astype(BF16)


def _outproj_kernel(alpha, has_next, packed, oh_ref, os_ref, x_ref, w_ref, mod_ref, ln_ref, *rest):
    mod_next_ref = rest[0] if has_next else None
    outs = rest[1:] if has_next else rest
    y = _dot(oh_ref[0], w_ref[0:D_HGRN, :]) + _dot(os_ref[0], w_ref[D_HGRN:, :])
    ln = ln_ref[...]
    xn = _residual_ln(x_ref[0], y, mod_ref[0][2:3], ln[0:1], ln[1:2], alpha)
    _emit(xn, outs, mod_next_ref, packed)


def _row_specs(bsz, s, d, tm, mod_next, packed=False):
    x_spec = pl.BlockSpec((1, tm, d), lambda b, i: (b, i, 0))
    mod_spec = pl.BlockSpec((1, 3, d), lambda b, i: (b, 0, 0))
    ln_spec = pl.BlockSpec((2, d), lambda b, i: (0, 0))
    out_specs = [pl.BlockSpec((1, tm, d), lambda b, i: (b, i, 0))]
    out_shape = [jax.ShapeDtypeStruct((bsz, s, d), F32)]
    if mod_next is not None and packed:
        out_specs.append(pl.BlockSpec((2, 1, tm, d // 4), lambda b, i: (0, b, i, 0)))
        out_shape.append(jax.ShapeDtypeStruct((2, bsz, s, d // 4), U32))
    elif mod_next is not None:
        out_specs.append(pl.BlockSpec((1, tm, d), lambda b, i: (b, i, 0)))
        out_shape.append(jax.ShapeDtypeStruct((bsz, s, d), BF16))
    return x_spec, mod_spec, ln_spec, out_specs, out_shape


def _out_proj_ln(o_h, o_s, x, w_out, mod, ln, mod_next, alpha, packed):
    bsz, s, d = x.shape
    tm = min(s, 512)
    x_spec, mod_spec, ln_spec, out_specs, out_shape = _row_specs(bsz, s, d, tm, mod_next, packed)
    in_specs = [pl.BlockSpec((1, tm, D_HGRN), lambda b, i: (b, i, 0)),
                pl.BlockSpec((1, tm, D_SB), lambda b, i: (b, i, 0)),
                x_spec,
                pl.BlockSpec((d, d), lambda b, i: (0, 0), pipeline_mode=pl.Buffered(1)),
                mod_spec, ln_spec]
    args = [o_h, o_s, x, w_out, mod, ln]
    if mod_next is not None:
        in_specs.append(mod_spec)
        args.append(mod_next)
    return pl.pallas_call(
        functools.partial(_outproj_kernel, alpha, mod_next is not None, packed),
        grid=(bsz, s // tm),
        in_specs=in_specs, out_specs=out_specs, out_shape=out_shape,
        compiler_params=_cparams(("parallel", "parallel")),
        name="out_proj_ln",
    )(*args)


def _dense_kernel(alpha, has_next, n_chunks, h_ref, x_ref, wg_ref, wu_ref, wd_ref, mod_ref, ln_ref, *rest):
    mod_next_ref = rest[0] if has_next else None
    outs = rest[1:] if has_next else rest
    h = h_ref[0]
    ff = wg_ref.shape[1]
    tf = ff // n_chunks
    y = None
    for j in range(n_chunks):
        cols = slice(j * tf, (j + 1) * tf)
        g = _dot(h, wg_ref[:, cols])
        u = _dot(h, wu_ref[:, cols])
        part = _dot((_silu(g) * u).astype(BF16), wd_ref[cols, :])
        y = part if y is None else y + part
    ln = ln_ref[...]
    xn = _residual_ln(x_ref[0], y, mod_ref[0][2:3], ln[0:1], ln[1:2], alpha)
    _emit(xn, outs, mod_next_ref)


def _dense_ffn_ln(h, x, wg, wu, wd, mod, ln, mod_next, alpha):
    bsz, s, d = x.shape
    ff = wg.shape[1]
    tm = min(s, 512)
    x_spec, mod_spec, ln_spec, out_specs, out_shape = _row_specs(bsz, s, d, tm, mod_next)
    once = pl.Buffered(1)
    in_specs = [pl.BlockSpec((1, tm, d), lambda b, i: (b, i, 0)),
                x_spec,
                pl.BlockSpec((d, ff), lambda b, i: (0, 0), pipeline_mode=once),
                pl.BlockSpec((d, ff), lambda b, i: (0, 0), pipeline_mode=once),
                pl.BlockSpec((ff, d), lambda b, i: (0, 0), pipeline_mode=once),
                mod_spec, ln_spec]
    args = [h, x, wg, wu, wd, mod, ln]
    if mod_next is not None:
        in_specs.append(mod_spec)
        args.append(mod_next)
    n_chunks = 2 if (ff // 2) % LANES == 0 else 1
    return pl.pallas_call(
        functools.partial(_dense_kernel, alpha, mod_next is not None, n_chunks),
        grid=(bsz, s // tm),
        in_specs=in_specs, out_specs=out_specs, out_shape=out_shape,
        compiler_params=_cparams(("parallel", "parallel")),
        name="dense_ffn_ln",
    )(*args)


def _router_kernel(h_ref, w_ref, tri_ref, route_ref, count_ref, run_ref):
    @pl.when(pl.program_id(0) == 0)
    def _():
        run_ref[...] = jnp.zeros_like(run_ref)

    tm = h_ref.shape[1]
    lane = lax.broadcasted_iota(jnp.int32, (tm, LANES), 1)
    logits = _dot(_load_planes(h_ref[0], h_ref[1]).astype(BF16), w_ref[...])
    neg = jnp.float32(-jnp.inf)
    logits = jnp.where(lane < N_EXPERTS, logits, neg)
    m1 = jnp.max(logits, axis=-1, keepdims=True)
    i1 = jnp.min(jnp.where(logits == m1, lane, LANES), axis=-1, keepdims=True)
    rest = jnp.where(lane == i1, neg, logits)
    m2 = jnp.max(rest, axis=-1, keepdims=True)
    i2 = jnp.min(jnp.where(rest == m2, lane, LANES), axis=-1, keepdims=True)
    e21 = jnp.exp(m2 - m1)
    w1 = 1.0 / (1.0 + e21)
    w2 = e21 / (1.0 + e21)

    hot1 = lane == i1
    hot2 = lane == i2
    both = jnp.where(jnp.logical_or(hot1, hot2), 1.0, 0.0)
    before = _dot(tri_ref[...], both.astype(BF16)) + run_ref[...]
    r1 = jnp.sum(jnp.where(hot1, before, 0.0), axis=-1, keepdims=True)
    r2 = jnp.sum(jnp.where(hot2, before, 0.0), axis=-1, keepdims=True)
    run_ref[...] = run_ref[...] + jnp.sum(both, axis=0, keepdims=True)
    count_ref[...] = run_ref[...]

    cols = [i1.astype(F32), i2.astype(F32), r1, r2, w1, w2]
    route = jnp.zeros((tm, LANES), F32)
    for c, val in enumerate(cols):
        route = jnp.where(lane == c, val, route)
    route_ref[...] = route[:, 0:8]


def _router(hp, w_router):
    _, n, pw = hp.shape
    d = 4 * pw
    tm = min(n, 512)
    w_pad = jnp.zeros((d, LANES), BF16).at[:, :N_EXPERTS].set(w_router.astype(BF16))
    jj = jnp.arange(tm)
    tri = (jj[None, :] < jj[:, None]).astype(BF16)
    return pl.pallas_call(
        _router_kernel,
        grid=(n // tm,),
        in_specs=[pl.BlockSpec((2, tm, pw), lambda i: (0, i, 0)),
                  pl.BlockSpec((d, LANES), lambda i: (0, 0)),
                  pl.BlockSpec((tm, tm), lambda i: (0, 0))],
        out_specs=[pl.BlockSpec((tm, 8), lambda i: (i, 0)),
                   pl.BlockSpec((1, LANES), lambda i: (0, 0))],
        out_shape=[jax.ShapeDtypeStruct((n, 8), F32),
                   jax.ShapeDtypeStruct((1, LANES), F32)],
        scratch_shapes=[pltpu.VMEM((1, LANES), F32)],
        compiler_params=_cparams(("arbitrary",)),
        name="moe_router",
    )(hp, w_pad, tri)


def _gmm_kernel(te_ref, nt_ref, x_ref, wg_ref, wu_ref, wd_ref, o_ref, acc_ref, xb_ref):
    i = pl.program_id(0)
    j = pl.program_id(1)
    nj = pl.num_programs(1)
    live = i < nt_ref[0]

    @pl.when(jnp.logical_and(live, j == 0))
    def _():
        acc_ref[...] = jnp.zeros_like(acc_ref)
        xb_ref[...] = _load_planes(x_ref[0], x_ref[1]).astype(BF16)

    @pl.when(live)
    def _():
        x = xb_ref[...]
        g = _dot(x, wg_ref[0, 0].astype(BF16))
        u = _dot(x, wu_ref[0, 0].astype(BF16))
        acc_ref[...] += _dot((_silu(g) * u).astype(BF16), wd_ref[0, 0].astype(BF16))

    @pl.when(jnp.logical_and(live, j == nj - 1))
    def _():
        _store_planes(o_ref, _pack_rows(acc_ref[...]))

    @pl.when(jnp.logical_and(jnp.logical_not(live), j == nj - 1))
    def _():
        o_ref[...] = jnp.zeros_like(o_ref)


def _grouped_swiglu(xs, tile_expert, n_tiles_used, wg, wu, wd, layer):
    _, n_slots, pw = xs.shape
    d = 4 * pw
    ff = wg.shape[3]
    tm, tf = MOE_TILE, MOE_FF_TILE
    n_tiles = n_slots // tm

    def row_map(i, j, te, nt):
        return (0, jnp.minimum(i, nt[0] - 1), 0)

    def ff_block(i, j, te, nt):
        return jnp.where(i < nt[0], j, ff // tf - 1)

    grid_spec = pltpu.PrefetchScalarGridSpec(
        num_scalar_prefetch=2,
        grid=(n_tiles, ff // tf),
        in_specs=[pl.BlockSpec((2, tm, pw), row_map),
                  pl.BlockSpec((1, 1, d, tf), lambda i, j, te, nt: (layer, te[i], 0, ff_block(i, j, te, nt))),
                  pl.BlockSpec((1, 1, d, tf), lambda i, j, te, nt: (layer, te[i], 0, ff_block(i, j, te, nt))),
                  pl.BlockSpec((1, 1, tf, d), lambda i, j, te, nt: (layer, te[i], ff_block(i, j, te, nt), 0))],
        out_specs=pl.BlockSpec((2, tm, pw), lambda i, j, te, nt: (0, i, 0)),
        scratch_shapes=[pltpu.VMEM((tm, d), F32), pltpu.VMEM((tm, d), BF16)],
    )
    return pl.pallas_call(
        _gmm_kernel,
        grid_spec=grid_spec,
        out_shape=jax.ShapeDtypeStruct((2, n_slots, pw), U32),
        compiler_params=_cparams(("arbitrary", "arbitrary")),
        name="moe_grouped_swiglu",
    )(tile_expert, n_tiles_used, xs, wg, wu, wd)


def _combine_kernel(alpha, has_next, ya_ref, yb_ref, r_ref, x_ref, mod_ref, ln_ref, *rest):
    mod_next_ref = rest[0] if has_next else None
    outs = rest[1:] if has_next else rest
    r = r_ref[0]
    y = (r[:, 4:5] * _load_planes(ya_ref[0, 0, 0], ya_ref[0, 1, 0])
         + r[:, 5:6] * _load_planes(yb_ref[0, 0, 0], yb_ref[0, 1, 0]))
    ln = ln_ref[...]
    xn = _residual_ln(x_ref[0], y, mod_ref[0][2:3], ln[0:1], ln[1:2], alpha)
    _emit(xn, outs, mod_next_ref)


def _combine_ln(yab, route, x, mod, ln, mod_next, alpha):
    bsz, s, d = x.shape
    tm = min(s, 512)
    x_spec, mod_spec, ln_spec, out_specs, out_shape = _row_specs(bsz, s, d, tm, mod_next)
    yab = yab.reshape(2, 2, bsz, s, d // 4)
    in_specs = [pl.BlockSpec((1, 2, 1, tm, d // 4), lambda b, i: (0, 0, b, i, 0)),
                pl.BlockSpec((1, 2, 1, tm, d // 4), lambda b, i: (1, 0, b, i, 0)),
                pl.BlockSpec((1, tm, 8), lambda b, i: (b, i, 0)),
                x_spec, mod_spec, ln_spec]
    args = [yab, yab, route.reshape(bsz, s, 8), x, mod, ln]
    if mod_next is not None:
        in_specs.append(mod_spec)
        args.append(mod_next)
    return pl.pallas_call(
        functools.partial(_combine_kernel, alpha, mod_next is not None),
        grid=(bsz, s // tm),
        in_specs=in_specs, out_specs=out_specs, out_shape=out_shape,
        compiler_params=_cparams(("parallel", "parallel")),
        name="moe_combine_ln",
    )(*args)


def _sc_mesh():
    return plsc.VectorSubcoreMesh(core_axis_name="c", subcore_axis_name="s")


def _dispatch_rows(rows, idx, n_out):
    m, w = rows.shape

    @pl.kernel(out_type=jax.ShapeDtypeStruct((n_out, w), rows.dtype), mesh=_sc_mesh(), scratch_types=[])
    def scatter(rows_hbm, idx_hbm, out_hbm):
        def body(rows_vmem, idx0_vmem, idx1_vmem):
            pltpu.sync_copy(rows_vmem, out_hbm.at[idx0_vmem.at[0]])
            pltpu.sync_copy(rows_vmem, out_hbm.at[idx1_vmem.at[0]])

        pltpu.emit_pipeline(
            body, grid=(m // SC_WINDOW,),
            in_specs=[pl.BlockSpec((SC_WINDOW, w), lambda i: (i, 0)),
                      pl.BlockSpec((1, SC_WINDOW), lambda i: (0, i)),
                      pl.BlockSpec((1, SC_WINDOW), lambda i: (1, i))],
            out_specs=[],
            core_axis_name=("c", "s"),
            dimension_semantics=(pltpu.PARALLEL,),
        )(rows_hbm, idx_hbm, idx_hbm)

    return scatter(rows, idx)


def _collect_rows(rows, idx):
    m = idx.shape[1]
    w = rows.shape[1]

    @pl.kernel(out_type=jax.ShapeDtypeStruct((m, w), rows.dtype), mesh=_sc_mesh(), scratch_types=[])
    def gather(rows_hbm, idx_hbm, out_hbm):
        def body(idx_vmem, out_vmem):
            pltpu.sync_copy(rows_hbm.at[idx_vmem.at[0]], out_vmem)

        pltpu.emit_pipeline(
            body, grid=(m // SC_WINDOW,),
            in_specs=[pl.BlockSpec((1, SC_WINDOW), lambda i: (0, i))],
            out_specs=[pl.BlockSpec((SC_WINDOW, w), lambda i: (i, 0))],
            core_axis_name=("c", "s"),
            dimension_semantics=(pltpu.PARALLEL,),
        )(idx_hbm, out_hbm)

    return gather(rows, idx)


def _moe_ffn_ln(h, x, w_router, wg, wu, wd, layer, mod, ln, mod_next, alpha):
    bsz, s, d = x.shape
    n = bsz * s
    hp = h.reshape(2, n, d // 4)
    route, counts = _router(hp, w_router)

    tm = MOE_TILE
    cnt = counts[0, :N_EXPERTS].astype(jnp.int32)
    padded = ((cnt + tm - 1) // tm) * tm
    ends = jnp.cumsum(padded)
    offs = ends - padded
    e = route[:, 0:2].astype(jnp.int32)
    rank = route[:, 2:4].astype(jnp.int32)
    pos = (jnp.take(offs, e) + rank).T
    n_tiles = (2 * n) // tm + N_EXPERTS
    n_slots = n_tiles * tm
    tile_start = jnp.arange(n_tiles, dtype=jnp.int32) * tm
    tile_expert = jnp.minimum(jnp.sum(tile_start[:, None] >= ends[None, :], axis=1),
                              N_EXPERTS - 1).astype(jnp.int32)
    n_used = (ends[-1] // tm).astype(jnp.int32).reshape(1)

    idx = jnp.concatenate([pos, pos + n_slots], axis=1)
    xs = _dispatch_rows(hp.reshape(2 * n, d // 4), idx, 2 * n_slots).reshape(2, n_slots, d // 4)
    ys = _grouped_swiglu(xs, tile_expert, n_used, wg, wu, wd, layer)
    yab = _collect_rows(ys.reshape(2 * n_slots, d // 4), idx.reshape(1, 4 * n)).reshape(2, 2, n, d // 4)
    return _combine_ln(yab, route, x, mod, ln, mod_next, alpha)


def kernel(x, c, w_ada, b_ada, w_in, w_out, hgrn_lb_logits, hgrn_norm_gain,
           w_dense_gate, w_dense_up, w_dense_down, w_router, w_moe_gate, w_moe_up, w_moe_down,
           ln_gain, ln_bias):
    depth = w_in.shape[0]
    bsz, s, d = x.shape
    alpha = (2 * depth) ** 0.25

    mods = _ada_all(c, w_ada, b_ada).reshape(depth, 2, bsz, 3, d)
    lb_all = _lower_bounds(hgrn_lb_logits)
    ln_all = jnp.stack([ln_gain, ln_bias], axis=2)

    h = _modulate(x, mods[0, 0])
    for layer in range(depth):
        proj = _in_proj(h.reshape(bsz * s, d), w_in[layer].astype(BF16)).reshape(bsz, s, D_IN_PROJ)
        o_h = _hgrn(proj, lb_all[layer:layer + 1], hgrn_norm_gain[layer:layer + 1])
        o_s = _stick_breaking(proj)
        x, h = _out_proj_ln(o_h, o_s, x, w_out[layer].astype(BF16), mods[layer, 0],
                            ln_all[layer, 0], mods[layer, 1], alpha, packed=layer % 2 == 1)
        mod_next = mods[layer + 1, 0] if layer + 1 < depth else None
        j = layer // 2
        if layer % 2 == 0:
            res = _dense_ffn_ln(h, x, w_dense_gate[j].astype(BF16), w_dense_up[j].astype(BF16),
                                w_dense_down[j].astype(BF16), mods[layer, 1], ln_all[layer, 1],
                                mod_next, alpha)
        else:
            res = _moe_ffn_ln(h, x, w_router[j], w_moe_gate, w_moe_up, w_moe_down, j,
                              mods[layer, 1], ln_all[layer, 1], mod_next, alpha)
        if mod_next is not None:
            x, h = res
        else:
            (x,) = res
    return x
```

```python
import functools
import math

import jax
import jax.numpy as jnp
from jax import lax
from jax.experimental import pallas as pl
from jax.experimental.pallas import tpu as pltpu
from jax.experimental.pallas import tpu_sc as plsc

D_MODEL = 1024
D_HGRN = 512
HGRN_HEADS = 4
HGRN_DH = 128
D_SB = 512
SB_HEADS = 8
SB_DH = 64
D_IN_PROJ = 4 * D_HGRN + 3 * D_SB
N_EXPERTS = 8
LN_EPS = 1e-5
RMS_EPS = 1e-6

LOG2E = math.log2(math.e)
LANES = 128
VMEM_LIMIT = 52 * 1024 * 1024

HG_UNIT = 128
HG_BAND = 4
HG_PAD = 8
HG_LEVELS = (64, 32, 16, 8, 4)
SB_BLOCK = 128
SB_SKIP = 110.0 * LOG2E
SB_GROUP = 4
SB_WIN = 3
MOE_TILE = 1024
MOE_FF_TILE = 512
SC_WINDOW = 128

F32 = jnp.float32
BF16 = jnp.bfloat16
U32 = jnp.uint32


def _cparams(sem):
    return pltpu.CompilerParams(dimension_semantics=sem, vmem_limit_bytes=VMEM_LIMIT)


def _dot(a, b):
    return jnp.dot(a, b, preferred_element_type=F32)


def _dot_nt(a, b):
    return lax.dot_general(a, b, (((1,), (1,)), ((), ())), preferred_element_type=F32)


def _dot_tn(a, b):
    return lax.dot_general(a, b, (((0,), (0,)), ((), ())), preferred_element_type=F32)


def _pack_rows(v):
    w = v.shape[-1] // 2
    lo = lax.bitcast_convert_type(v[:, :w].astype(BF16).astype(F32), U32)
    hi = lax.bitcast_convert_type(v[:, w:].astype(BF16).astype(F32), U32)
    return (lo >> 16) | hi


def _unpack_rows(p):
    lo = lax.bitcast_convert_type(p << 16, F32)
    hi = lax.bitcast_convert_type(p & jnp.uint32(0xFFFF0000), F32)
    return jnp.concatenate([lo, hi], axis=-1)


def _store_planes(ref, p):
    w = p.shape[-1] // 2
    ref[0] = p[:, :w].reshape(ref.shape[1:])
    ref[1] = p[:, w:].reshape(ref.shape[1:])


def _load_planes(p0, p1):
    return _unpack_rows(jnp.concatenate([p0, p1], axis=-1))


def _sigmoid(x):
    return 1.0 / (1.0 + jnp.exp(-x))


def _silu(x):
    return x * _sigmoid(x)


def _ada_kernel(c_ref, w_ref, b_ref, o_ref):
    sc = _silu(c_ref[...])
    o_ref[0] = jnp.dot(sc, w_ref[0], preferred_element_type=F32,
                       precision=lax.Precision.HIGHEST) + b_ref[0]


def _ada_all(c, w_ada, b_ada):
    n_sub = w_ada.shape[0] * w_ada.shape[1]
    bsz, d = c.shape
    w = w_ada.reshape(n_sub, d, 3 * d)
    b = b_ada.reshape(n_sub, 1, 3 * d)
    tn = 1024
    return pl.pallas_call(
        _ada_kernel,
        grid=(n_sub, 3 * d // tn),
        in_specs=[pl.BlockSpec((bsz, d), lambda s, j: (0, 0)),
                  pl.BlockSpec((1, d, tn), lambda s, j: (s, 0, j)),
                  pl.BlockSpec((1, 1, tn), lambda s, j: (s, 0, j))],
        out_specs=pl.BlockSpec((1, bsz, tn), lambda s, j: (s, 0, j)),
        out_shape=jax.ShapeDtypeStruct((n_sub, bsz, 3 * d), F32),
        compiler_params=_cparams(("parallel", "parallel")),
        name="ada_modulation",
    )(c, w, b)


def _lb_kernel(l_ref, o_ref):
    x = l_ref[...]
    m = jnp.max(x, axis=0, keepdims=True)
    e = jnp.exp(x - m)
    p = e / jnp.sum(e, axis=0, keepdims=True)
    depth = x.shape[0]
    run = jnp.zeros_like(p[0:1])
    rows = []
    for i in range(depth):
        run = run + p[i:i + 1]
        rows.append(run - p[0:1])
    o_ref[...] = jnp.concatenate(rows, axis=0)


def _lower_bounds(lb_logits):
    return pl.pallas_call(
        _lb_kernel,
        out_shape=jax.ShapeDtypeStruct(lb_logits.shape, F32),
        name="hgrn_lower_bounds",
    )(lb_logits)


def _modulate_kernel(x_ref, m_ref, h_ref):
    m = m_ref[0]
    h_ref[0] = (x_ref[0] * (1.0 + m[1:2]) + m[0:1]).astype(BF16)


def _modulate(x, mod):
    bsz, s, d = x.shape
    ts = min(s, 1024)
    return pl.pallas_call(
        _modulate_kernel,
        grid=(bsz, s // ts),
        in_specs=[pl.BlockSpec((1, ts, d), lambda b, i: (b, i, 0)),
                  pl.BlockSpec((1, 3, d), lambda b, i: (b, 0, 0))],
        out_specs=pl.BlockSpec((1, ts, d), lambda b, i: (b, i, 0)),
        out_shape=jax.ShapeDtypeStruct((bsz, s, d), BF16),
        compiler_params=_cparams(("parallel", "parallel")),
        name="modulate_first",
    )(x, mod)


def _inproj_kernel(h_ref, w_ref, o_ref):
    h = h_ref[...]
    n = w_ref.shape[1]
    tn = 512
    for j in range(n // tn):
        o_ref[:, j * tn:(j + 1) * tn] = _dot(h, w_ref[:, j * tn:(j + 1) * tn]).astype(BF16)


def _in_proj(h2d, w_in):
    n, d = h2d.shape
    dout = w_in.shape[1]
    tm = min(n, 512)
    return pl.pallas_call(
        _inproj_kernel,
        grid=(n // tm,),
        in_specs=[pl.BlockSpec((tm, d), lambda i: (i, 0)),
                  pl.BlockSpec((d, dout), lambda i: (0, 0), pipeline_mode=pl.Buffered(1))],
        out_specs=pl.BlockSpec((tm, dout), lambda i: (i, 0)),
        out_shape=jax.ShapeDtypeStruct((n, dout), BF16),
        compiler_params=_cparams(("parallel",)),
        name="in_proj",
    )(h2d, w_in)


def _hgrn_consts():
    t = jnp.arange(HG_UNIT)[:, None]
    s = jnp.arange(HG_UNIT)[None, :]
    tril = (s <= t).astype(BF16)
    lv = jnp.full((HG_UNIT, HG_UNIT), -1, jnp.int32)
    for idx, m in enumerate(HG_LEVELS):
        blk = 2 * m
        hit = (t // blk == s // blk) & (t % blk >= m) & (s % blk < m)
        lv = jnp.where(hit, idx, lv)
    return tril, lv


def _hgrn_kernel(q_ref, f_ref, i_ref, g_ref, lb_ref, gain_ref, tril_ref, lv_ref, o_ref,
                 states_ref, kpads_ref, bpads_ref, vpads_ref):
    @pl.when(pl.program_id(1) == 0)
    def _():
        states_ref[...] = jnp.zeros_like(states_ref)

    heads = range(HGRN_HEADS)
    zpad = jnp.zeros((HGRN_HEADS, HG_PAD, HGRN_DH), F32)
    kpads_ref[:, 0:HG_PAD, :] = zpad
    bpads_ref[:, 0:HG_PAD, :] = zpad
    vpads_ref[:, 0:HG_PAD, :] = zpad

    n_units = q_ref.shape[1] // HG_UNIT
    rowi = lax.broadcasted_iota(jnp.int32, (HG_UNIT, HGRN_DH), 0)
    tmod = jnp.bitwise_and(rowi, HG_BAND - 1)

    def unit(u, carry):
        rows = pl.ds(pl.multiple_of(u * HG_UNIT, HG_UNIT), HG_UNIT)
        lanes = [slice(h * HGRN_DH, (h + 1) * HGRN_DH) for h in heads]

        kk, qq, v, vb, b3 = [], [], [], [], []
        for h in heads:
            lb = lb_ref[:, lanes[h]]
            f = lb + (1.0 - lb) * _sigmoid(f_ref[0, rows, lanes[h]].astype(F32))
            logf = jnp.log2(f)
            kk.append(1.0 - f)
            qq.append(_silu(q_ref[0, rows, lanes[h]].astype(F32)))
            v.append(i_ref[0, rows, lanes[h]].astype(F32))
            vb.append(i_ref[0, rows, lanes[h]])
            p0 = logf.astype(BF16)
            r1 = logf - p0.astype(F32)
            p1 = r1.astype(BF16)
            p2 = (r1 - p1.astype(F32)).astype(BF16)
            b3.append(_dot(tril_ref[...], jnp.concatenate([p0, p1, p2], axis=1)))
        b = [t[:, 0:HGRN_DH] + t[:, HGRN_DH:2 * HGRN_DH] + t[:, 2 * HGRN_DH:] for t in b3]

        state = [states_ref[h] for h in heads]
        o = [_dot_nt((qq[h] * jnp.exp2(b[h])).astype(BF16), state[h].astype(BF16)) for h in heads]

        lv = lv_ref[...]
        scores = [jnp.zeros((HG_UNIT, HG_UNIT), F32) for _ in heads]
        for idx, m in enumerate(HG_LEVELS):
            blk = 2 * m
            nb = HG_UNIT // blk
            upper = jnp.bitwise_and(rowi, m) != 0
            xl = []
            for h in heads:
                ref = b[h].reshape(nb, blk, HGRN_DH)[:, m - 1:m, :]
                ref = jnp.broadcast_to(ref, (nb, blk, HGRN_DH)).reshape(HG_UNIT, HGRN_DH)
                xl.append((jnp.where(upper, qq[h], kk[h]) * jnp.exp2(-jnp.abs(b[h] - ref))).astype(BF16))
            s_l = [_dot_nt(x, x) for x in xl]
            scores = [jnp.where(lv == idx, s_l[h], scores[h]) for h in heads]
        intra = [_dot(scores[h].astype(BF16), vb[h]) for h in heads]

        for h in heads:
            kpads_ref[h, HG_PAD:, :] = kk[h]
            bpads_ref[h, HG_PAD:, :] = b[h]
            vpads_ref[h, HG_PAD:, :] = v[h]
        band = [jnp.zeros((HG_UNIT, HGRN_DH), F32) for _ in heads]
        for d in range(HG_BAND):
            win = pl.ds(HG_PAD - d, HG_UNIT)
            for h in heads:
                p = qq[h] * kpads_ref[h, win, :] * jnp.exp2(b[h] - bpads_ref[h, win, :])
                s_d = jnp.sum(p, axis=-1, keepdims=True)
                band[h] = band[h] + jnp.where(tmod >= d, s_d * vpads_ref[h, win, :], 0.0)

        for h in heads:
            b_last = b[h][HG_UNIT - 1:HG_UNIT, :]
            kdec = kk[h] * jnp.exp2(b_last - b[h])
            states_ref[h] = state[h] * jnp.exp2(b_last) + _dot_tn(vb[h], kdec.astype(BF16))
        for h in heads:
            oh = o[h] + intra[h] + band[h]
            oh = oh * lax.rsqrt(jnp.mean(oh * oh, axis=-1, keepdims=True) + RMS_EPS)
            gg = g_ref[0, rows, lanes[h]].astype(F32)
            o_ref[0, rows, lanes[h]] = (oh * gain_ref[:, lanes[h]] * _silu(gg)).astype(BF16)
        return carry

    lax.fori_loop(0, n_units, unit, 0)


def _hgrn(proj, lb, gain):
    bsz, s, _ = proj.shape
    ts = min(s, 512)
    tril, lv = _hgrn_consts()

    def col(off):
        return pl.BlockSpec((1, ts, D_HGRN), lambda b, i, off=off: (b, i, off))

    return pl.pallas_call(
        _hgrn_kernel,
        grid=(bsz, s // ts),
        in_specs=[col(0), col(1), col(2), col(3),
                  pl.BlockSpec((1, D_HGRN), lambda b, i: (0, 0)),
                  pl.BlockSpec((1, D_HGRN), lambda b, i: (0, 0)),
                  pl.BlockSpec((HG_UNIT, HG_UNIT), lambda b, i: (0, 0)),
                  pl.BlockSpec((HG_UNIT, HG_UNIT), lambda b, i: (0, 0))],
        out_specs=pl.BlockSpec((1, ts, D_HGRN), lambda b, i: (b, i, 0)),
        out_shape=jax.ShapeDtypeStruct((bsz, s, D_HGRN), BF16),
        scratch_shapes=[pltpu.VMEM((HGRN_HEADS, HGRN_DH, HGRN_DH), F32),
                        pltpu.VMEM((HGRN_HEADS, HG_UNIT + HG_PAD, HGRN_DH), F32),
                        pltpu.VMEM((HGRN_HEADS, HG_UNIT + HG_PAD, HGRN_DH), F32),
                        pltpu.VMEM((HGRN_HEADS, HG_UNIT + HG_PAD, HGRN_DH), F32)],
        compiler_params=_cparams(("parallel", "arbitrary")),
        name="hgrn2",
    )(proj, proj, proj, proj, lb, gain, tril, lv)


def _sb_kernel(q_ref, k_ref, v_ref, uo_ref, o_ref, kp_ref, vp_ref, c_ref, acc_ref):
    tq = SB_BLOCK
    nq = q_ref.shape[1] // tq
    pad = (SB_WIN - 1) * tq
    kp_ref[0:pad, :] = jnp.zeros((pad, LANES), BF16)
    vp_ref[0:pad, :] = jnp.zeros((pad, LANES), BF16)
    kp_ref[pad:, :] = k_ref[0]
    vp_ref[pad:, :] = v_ref[0]

    first = lax.broadcasted_iota(jnp.int32, (tq, LANES), 1) < SB_DH
    rowi = lax.broadcasted_iota(jnp.int32, (2 * tq, tq), 0)
    coli = lax.broadcasted_iota(jnp.int32, (2 * tq, tq), 1)
    strict = coli < jnp.bitwise_and(rowi, tq - 1)
    neg_inf = jnp.float32(-jnp.inf)

    def stacked_q(qi):
        rows = pl.ds(pl.multiple_of(qi * tq, tq), tq)
        q = q_ref[0, rows, :] * (SB_DH ** -0.5)
        zero = jnp.zeros_like(q)
        return rows, jnp.concatenate([jnp.where(first, q, zero), jnp.where(first, zero, q)], axis=0)

    def window(q2, p_top, diag, crow, acc):
        (crow, acc), = windows([(q2, p_top, crow, acc)], diag)
        return crow, acc

    def windows(items, diag):
        order = list(reversed(range(SB_WIN)))
        zs, cts = [], []
        for q2, p_top, _, _ in items:
            start = pl.multiple_of((p_top - (SB_WIN - 1)) * tq, tq)
            zs.append(_dot_nt(q2, kp_ref[pl.ds(start, SB_WIN * tq), :]) * LOG2E)
        for z in zs:
            logn = -(jnp.maximum(z, 0.0) + jnp.log2(1.0 + jnp.exp2(-jnp.abs(z))))
            lhs = []
            for j in order:
                lj = logn[:, j * tq:(j + 1) * tq]
                if diag and j == SB_WIN - 1:
                    lj = jnp.where(strict, lj, 0.0)
                hi = lj.astype(BF16)
                lo = (lj - hi.astype(F32)).astype(BF16)
                lhs.append(jnp.concatenate([hi, lo], axis=1))
            cts.append(_dot(jnp.concatenate(lhs, axis=0), uo_ref[...]))
        outs = []
        for (q2, p_top, crow, acc), z, ct in zip(items, zs, cts):
            start = pl.multiple_of((p_top - (SB_WIN - 1)) * tq, tq)
            parts = [None] * SB_WIN
            for idx, j in enumerate(order):
                blk = ct[idx * 2 * tq:(idx + 1) * 2 * tq]
                a = jnp.exp2(z[:, j * tq:(j + 1) * tq] + (crow + blk))
                if diag and j == SB_WIN - 1:
                    a = jnp.where(strict, a, 0.0)
                parts[j] = a.astype(BF16)
                crow = crow + jnp.broadcast_to(blk[:, 0:1], blk.shape)
            acc = acc + _dot(jnp.concatenate(parts, axis=1), vp_ref[pl.ds(start, SB_WIN * tq), :])
            outs.append((crow, acc))
        return outs

    def emit(rows, acc):
        o_ref[0, rows, :] = jnp.where(first, acc[0:tq], acc[tq:]).astype(BF16)

    def fast(g, worst):
        zero = jnp.zeros((2 * tq, LANES), F32)
        qis = [g * SB_GROUP + t for t in range(SB_GROUP)]
        prep = [stacked_q(qi) for qi in qis]
        res = windows([(q2, qi + SB_WIN - 1, zero, zero) for qi, (_, q2) in zip(qis, prep)], True)
        for qi, (rows, _), (crow, acc) in zip(qis, prep, res):
            emit(rows, acc)
            c_ref[qi] = crow
            acc_ref[qi] = acc
            worst = jnp.maximum(worst, jnp.where(qi >= SB_WIN, crow, neg_inf))
        return worst

    worst = lax.fori_loop(0, nq // SB_GROUP, fast, jnp.full((2 * tq, LANES), neg_inf, F32))

    @pl.when(jnp.max(worst) > -SB_SKIP)
    def _():
        def finish(qi, carry):
            rows, q2 = stacked_q(qi)

            def cond(c):
                p_top, mx, _, _ = c
                return jnp.logical_and(p_top >= SB_WIN - 1, mx > -SB_SKIP)

            def body(c):
                p_top, _, crow, acc = c
                crow, acc = window(q2, p_top, False, crow, acc)
                return p_top - SB_WIN, jnp.max(crow), crow, acc

            crow0 = c_ref[qi]
            _, _, _, acc = lax.while_loop(cond, body, (qi - 1, jnp.max(crow0), crow0, acc_ref[qi]))
            emit(rows, acc)
            return carry

        lax.fori_loop(SB_WIN, nq, finish, 0)


def _stick_breaking(proj):
    bsz, s, _ = proj.shape
    tq = SB_BLOCK
    nq = s // tq
    assert nq % SB_GROUP == 0
    n_pairs = D_SB // LANES
    base = 4 * D_HGRN // LANES
    jj = jnp.arange(tq)
    suffix = (jj[:, None] >= jj[None, :]).astype(BF16)
    uo = jnp.concatenate([suffix, suffix], axis=0)

    def seq(off):
        return pl.BlockSpec((1, s, LANES), lambda b, p, off=off: (b, 0, base + off * n_pairs + p))

    return pl.pallas_call(
        _sb_kernel,
        grid=(bsz, n_pairs),
        in_specs=[seq(0), seq(1), seq(2),
                  pl.BlockSpec((2 * tq, tq), lambda b, p: (0, 0))],
        out_specs=pl.BlockSpec((1, s, LANES), lambda b, p: (b, 0, p)),
        out_shape=jax.ShapeDtypeStruct((bsz, s, D_SB), BF16),
        scratch_shapes=[pltpu.VMEM((s + (SB_WIN - 1) * tq, LANES), BF16),
                        pltpu.VMEM((s + (SB_WIN - 1) * tq, LANES), BF16),
                        pltpu.VMEM((nq, 2 * tq, LANES), F32),
                        pltpu.VMEM((nq, 2 * tq, LANES), F32)],
        compiler_params=_cparams(("parallel", "parallel")),
        name="stick_breaking",
    )(proj, proj, proj, uo)


def _residual_ln(x, y, gate, gain, bias, alpha):
    r = alpha * x + gate * y
    mu = jnp.mean(r, axis=-1, keepdims=True)
    rc = r - mu
    var = jnp.mean(rc * rc, axis=-1, keepdims=True)
    return rc * lax.rsqrt(var + LN_EPS) * gain + bias


def _emit(xn, outs, mod_next_ref, packed=False):
    outs[0][0] = xn
    if mod_next_ref is not None:
        mn = mod_next_ref[0]
        hn = xn * (1.0 + mn[1:2]) + mn[0:1]
        if packed:
            _store_planes(outs[1], _pack_rows(hn))
        else:
            outs[1][0] = hn.astype(BF16)


def _outproj_kernel(alpha, has_next, packed, oh_ref, os_ref, x_ref, w_ref, mod_ref, ln_ref, *rest):
    mod_next_ref = rest[0] if has_next else None
    outs = rest[1:] if has_next else rest
    y = _dot(oh_ref[0], w_ref[0:D_HGRN, :]) + _dot(os_ref[0], w_ref[D_HGRN:, :])
    ln = ln_ref[...]
    xn = _residual_ln(x_ref[0], y, mod_ref[0][2:3], ln[0:1], ln[1:2], alpha)
    _emit(xn, outs, mod_next_ref, packed)


def _row_specs(bsz, s, d, tm, mod_next, packed=False):
    x_spec = pl.BlockSpec((1, tm, d), lambda b, i: (b, i, 0))
    mod_spec = pl.BlockSpec((1, 3, d), lambda b, i: (b, 0, 0))
    ln_spec = pl.BlockSpec((2, d), lambda b, i: (0, 0))
    out_specs = [pl.BlockSpec((1, tm, d), lambda b, i: (b, i, 0))]
    out_shape = [jax.ShapeDtypeStruct((bsz, s, d), F32)]
    if mod_next is not None and packed:
        out_specs.append(pl.BlockSpec((2, 1, tm, d // 4), lambda b, i: (0, b, i, 0)))
        out_shape.append(jax.ShapeDtypeStruct((2, bsz, s, d // 4), U32))
    elif mod_next is not None:
        out_specs.append(pl.BlockSpec((1, tm, d), lambda b, i: (b, i, 0)))
        out_shape.append(jax.ShapeDtypeStruct((bsz, s, d), BF16))
    return x_spec, mod_spec, ln_spec, out_specs, out_shape


def _out_proj_ln(o_h, o_s, x, w_out, mod, ln, mod_next, alpha, packed):
    bsz, s, d = x.shape
    tm = min(s, 512)
    x_spec, mod_spec, ln_spec, out_specs, out_shape = _row_specs(bsz, s, d, tm, mod_next, packed)
    in_specs = [pl.BlockSpec((1, tm, D_HGRN), lambda b, i: (b, i, 0)),
                pl.BlockSpec((1, tm, D_SB), lambda b, i: (b, i, 0)),
                x_spec,
                pl.BlockSpec((d, d), lambda b, i: (0, 0), pipeline_mode=pl.Buffered(1)),
                mod_spec, ln_spec]
    args = [o_h, o_s, x, w_out, mod, ln]
    if mod_next is not None:
        in_specs.append(mod_spec)
        args.append(mod_next)
    return pl.pallas_call(
        functools.partial(_outproj_kernel, alpha, mod_next is not None, packed),
        grid=(bsz, s // tm),
        in_specs=in_specs, out_specs=out_specs, out_shape=out_shape,
        compiler_params=_cparams(("parallel", "parallel")),
        name="out_proj_ln",
    )(*args)


def _dense_kernel(alpha, has_next, n_chunks, h_ref, x_ref, wg_ref, wu_ref, wd_ref, mod_ref, ln_ref, *rest):
    mod_next_ref = rest[0] if has_next else None
    outs = rest[1:] if has_next else rest
    h = h_ref[0]
    ff = wg_ref.shape[1]
    tf = ff // n_chunks
    y = None
    for j in range(n_chunks):
        cols = slice(j * tf, (j + 1) * tf)
        g = _dot(h, wg_ref[:, cols])
        u = _dot(h, wu_ref[:, cols])
        part = _dot((_silu(g) * u).astype(BF16), wd_ref[cols, :])
        y = part if y is None else y + part
    ln = ln_ref[...]
    xn = _residual_ln(x_ref[0], y, mod_ref[0][2:3], ln[0:1], ln[1:2], alpha)
    _emit(xn, outs, mod_next_ref)


def _dense_ffn_ln(h, x, wg, wu, wd, mod, ln, mod_next, alpha):
    bsz, s, d = x.shape
    ff = wg.shape[1]
    tm = min(s, 512)
    x_spec, mod_spec, ln_spec, out_specs, out_shape = _row_specs(bsz, s, d, tm, mod_next)
    once = pl.Buffered(1)
    in_specs = [pl.BlockSpec((1, tm, d), lambda b, i: (b, i, 0)),
                x_spec,
                pl.BlockSpec((d, ff), lambda b, i: (0, 0), pipeline_mode=once),
                pl.BlockSpec((d, ff), lambda b, i: (0, 0), pipeline_mode=once),
                pl.BlockSpec((ff, d), lambda b, i: (0, 0), pipeline_mode=once),
                mod_spec, ln_spec]
    args = [h, x, wg, wu, wd, mod, ln]
    if mod_next is not None:
        in_specs.append(mod_spec)
        args.append(mod_next)
    n_chunks = 2 if (ff // 2) % LANES == 0 else 1
    return pl.pallas_call(
        functools.partial(_dense_kernel, alpha, mod_next is not None, n_chunks),
        grid=(bsz, s // tm),
        in_specs=in_specs, out_specs=out_specs, out_shape=out_shape,
        compiler_params=_cparams(("parallel", "parallel")),
        name="dense_ffn_ln",
    )(*args)


def _router_kernel(h_ref, w_ref, tri_ref, route_ref, count_ref, run_ref):
    @pl.when(pl.program_id(0) == 0)
    def _():
        run_ref[...] = jnp.zeros_like(run_ref)

    tm = h_ref.shape[1]
    lane = lax.broadcasted_iota(jnp.int32, (tm, LANES), 1)
    logits = _dot(_load_planes(h_ref[0], h_ref[1]).astype(BF16), w_ref[...])
    neg = jnp.float32(-jnp.inf)
    logits = jnp.where(lane < N_EXPERTS, logits, neg)
    m1 = jnp.max(logits, axis=-1, keepdims=True)
    i1 = jnp.min(jnp.where(logits == m1, lane, LANES), axis=-1, keepdims=True)
    rest = jnp.where(lane == i1, neg, logits)
    m2 = jnp.max(rest, axis=-1, keepdims=True)
    i2 = jnp.min(jnp.where(rest == m2, lane, LANES), axis=-1, keepdims=True)
    e21 = jnp.exp(m2 - m1)
    w1 = 1.0 / (1.0 + e21)
    w2 = e21 / (1.0 + e21)

    hot1 = lane == i1
    hot2 = lane == i2
    both = jnp.where(jnp.logical_or(hot1, hot2), 1.0, 0.0)
    before = _dot(tri_ref[...], both.astype(BF16)) + run_ref[...]
    r1 = jnp.sum(jnp.where(hot1, before, 0.0), axis=-1, keepdims=True)
    r2 = jnp.sum(jnp.where(hot2, before, 0.0), axis=-1, keepdims=True)
    run_ref[...] = run_ref[...] + jnp.sum(both, axis=0, keepdims=True)
    count_ref[...] = run_ref[...]

    cols = [i1.astype(F32), i2.astype(F32), r1, r2, w1, w2]
    route = jnp.zeros((tm, LANES), F32)
    for c, val in enumerate(cols):
        route = jnp.where(lane == c, val, route)
    route_ref[...] = route[:, 0:8]


def _router(hp, w_router):
    _, n, pw = hp.shape
    d = 4 * pw
    tm = min(n, 512)
    w_pad = jnp.zeros((d, LANES), BF16).at[:, :N_EXPERTS].set(w_router.astype(BF16))
    jj = jnp.arange(tm)
    tri = (jj[None, :] < jj[:, None]).astype(BF16)
    return pl.pallas_call(
        _router_kernel,
        grid=(n // tm,),
        in_specs=[pl.BlockSpec((2, tm, pw), lambda i: (0, i, 0)),
                  pl.BlockSpec((d, LANES), lambda i: (0, 0)),
                  pl.BlockSpec((tm, tm), lambda i: (0, 0))],
        out_specs=[pl.BlockSpec((tm, 8), lambda i: (i, 0)),
                   pl.BlockSpec((1, LANES), lambda i: (0, 0))],
        out_shape=[jax.ShapeDtypeStruct((n, 8), F32),
                   jax.ShapeDtypeStruct((1, LANES), F32)],
        scratch_shapes=[pltpu.VMEM((1, LANES), F32)],
        compiler_params=_cparams(("arbitrary",)),
        name="moe_router",
    )(hp, w_pad, tri)


def _gmm_kernel(te_ref, nt_ref, x_ref, wg_ref, wu_ref, wd_ref, o_ref, acc_ref, xb_ref):
    i = pl.program_id(0)
    j = pl.program_id(1)
    nj = pl.num_programs(1)
    live = i < nt_ref[0]

    @pl.when(jnp.logical_and(live, j == 0))
    def _():
        acc_ref[...] = jnp.zeros_like(acc_ref)
        xb_ref[...] = _load_planes(x_ref[0], x_ref[1]).astype(BF16)

    @pl.when(live)
    def _():
        x = xb_ref[...]
        g = _dot(x, wg_ref[0, 0].astype(BF16))
        u = _dot(x, wu_ref[0, 0].astype(BF16))
        acc_ref[...] += _dot((_silu(g) * u).astype(BF16), wd_ref[0, 0].astype(BF16))

    @pl.when(jnp.logical_and(live, j == nj - 1))
    def _():
        _store_planes(o_ref, _pack_rows(acc_ref[...]))

    @pl.when(jnp.logical_and(jnp.logical_not(live), j == nj - 1))
    def _():
        o_ref[...] = jnp.zeros_like(o_ref)


def _grouped_swiglu(xs, tile_expert, n_tiles_used, wg, wu, wd, layer):
    _, n_slots, pw = xs.shape
    d = 4 * pw
    ff = wg.shape[3]
    tm, tf = MOE_TILE, MOE_FF_TILE
    n_tiles = n_slots // tm

    def row_map(i, j, te, nt):
        return (0, jnp.minimum(i, nt[0] - 1), 0)

    def ff_block(i, j, te, nt):
        return jnp.where(i < nt[0], j, ff // tf - 1)

    grid_spec = pltpu.PrefetchScalarGridSpec(
        num_scalar_prefetch=2,
        grid=(n_tiles, ff // tf),
        in_specs=[pl.BlockSpec((2, tm, pw), row_map),
                  pl.BlockSpec((1, 1, d, tf), lambda i, j, te, nt: (layer, te[i], 0, ff_block(i, j, te, nt))),
                  pl.BlockSpec((1, 1, d, tf), lambda i, j, te, nt: (layer, te[i], 0, ff_block(i, j, te, nt))),
                  pl.BlockSpec((1, 1, tf, d), lambda i, j, te, nt: (layer, te[i], ff_block(i, j, te, nt), 0))],
        out_specs=pl.BlockSpec((2, tm, pw), lambda i, j, te, nt: (0, i, 0)),
        scratch_shapes=[pltpu.VMEM((tm, d), F32), pltpu.VMEM((tm, d), BF16)],
    )
    return pl.pallas_call(
        _gmm_kernel,
        grid_spec=grid_spec,
        out_shape=jax.ShapeDtypeStruct((2, n_slots, pw), U32),
        compiler_params=_cparams(("arbitrary", "arbitrary")),
        name="moe_grouped_swiglu",
    )(tile_expert, n_tiles_used, xs, wg, wu, wd)


def _combine_kernel(alpha, has_next, ya_ref, yb_ref, r_ref, x_ref, mod_ref, ln_ref, *rest):
    mod_next_ref = rest[0] if has_next else None
    outs = rest[1:] if has_next else rest
    r = r_ref[0]
    y = (r[:, 4:5] * _load_planes(ya_ref[0, 0, 0], ya_ref[0, 1, 0])
         + r[:, 5:6] * _load_planes(yb_ref[0, 0, 0], yb_ref[0, 1, 0]))
    ln = ln_ref[...]
    xn = _residual_ln(x_ref[0], y, mod_ref[0][2:3], ln[0:1], ln[1:2], alpha)
    _emit(xn, outs, mod_next_ref)


def _combine_ln(yab, route, x, mod, ln, mod_next, alpha):
    bsz, s, d = x.shape
    tm = min(s, 512)
    x_spec, mod_spec, ln_spec, out_specs, out_shape = _row_specs(bsz, s, d, tm, mod_next)
    yab = yab.reshape(2, 2, bsz, s, d // 4)
    in_specs = [pl.BlockSpec((1, 2, 1, tm, d // 4), lambda b, i: (0, 0, b, i, 0)),
                pl.BlockSpec((1, 2, 1, tm, d // 4), lambda b, i: (1, 0, b, i, 0)),
                pl.BlockSpec((1, tm, 8), lambda b, i: (b, i, 0)),
                x_spec, mod_spec, ln_spec]
    args = [yab, yab, route.reshape(bsz, s, 8), x, mod, ln]
    if mod_next is not None:
        in_specs.append(mod_spec)
        args.append(mod_next)
    return pl.pallas_call(
        functools.partial(_combine_kernel, alpha, mod_next is not None),
        grid=(bsz, s // tm),
        in_specs=in_specs, out_specs=out_specs, out_shape=out_shape,
        compiler_params=_cparams(("parallel", "parallel")),
        name="moe_combine_ln",
    )(*args)


def _sc_mesh():
    return plsc.VectorSubcoreMesh(core_axis_name="c", subcore_axis_name="s")


def _dispatch_rows(rows, idx, n_out):
    m, w = rows.shape

    @pl.kernel(out_type=jax.ShapeDtypeStruct((n_out, w), rows.dtype), mesh=_sc_mesh(), scratch_types=[])
    def scatter(rows_hbm, idx_hbm, out_hbm):
        def body(rows_vmem, idx0_vmem, idx1_vmem):
            pltpu.sync_copy(rows_vmem, out_hbm.at[idx0_vmem.at[0]])
            pltpu.sync_copy(rows_vmem, out_hbm.at[idx1_vmem.at[0]])

        pltpu.emit_pipeline(
            body, grid=(m // SC_WINDOW,),
            in_specs=[pl.BlockSpec((SC_WINDOW, w), lambda i: (i, 0)),
                      pl.BlockSpec((1, SC_WINDOW), lambda i: (0, i)),
                      pl.BlockSpec((1, SC_WINDOW), lambda i: (1, i))],
            out_specs=[],
            core_axis_name=("c", "s"),
            dimension_semantics=(pltpu.PARALLEL,),
        )(rows_hbm, idx_hbm, idx_hbm)

    return scatter(rows, idx)


def _collect_rows(rows, idx):
    m = idx.shape[1]
    w = rows.shape[1]

    @pl.kernel(out_type=jax.ShapeDtypeStruct((m, w), rows.dtype), mesh=_sc_mesh(), scratch_types=[])
    def gather(rows_hbm, idx_hbm, out_hbm):
        def body(idx_vmem, out_vmem):
            pltpu.sync_copy(rows_hbm.at[idx_vmem.at[0]], out_vmem)

        pltpu.emit_pipeline(
            body, grid=(m // SC_WINDOW,),
            in_specs=[pl.BlockSpec((1, SC_WINDOW), lambda i: (0, i))],
            out_specs=[pl.BlockSpec((SC_WINDOW, w), lambda i: (i, 0))],
            core_axis_name=("c", "s"),
            dimension_semantics=(pltpu.PARALLEL,),
        )(idx_hbm, out_hbm)

    return gather(rows, idx)


def _moe_ffn_ln(h, x, w_router, wg, wu, wd, layer, mod, ln, mod_next, alpha):
    bsz, s, d = x.shape
    n = bsz * s
    hp = h.reshape(2, n, d // 4)
    route, counts = _router(hp, w_router)

    tm = MOE_TILE
    cnt = counts[0, :N_EXPERTS].astype(jnp.int32)
    padded = ((cnt + tm - 1) // tm) * tm
    ends = jnp.cumsum(padded)
    offs = ends - padded
    e = route[:, 0:2].astype(jnp.int32)
    rank = route[:, 2:4].astype(jnp.int32)
    pos = (jnp.take(offs, e) + rank).T
    n_tiles = (2 * n) // tm + N_EXPERTS
    n_slots = n_tiles * tm
    tile_start = jnp.arange(n_tiles, dtype=jnp.int32) * tm
    tile_expert = jnp.minimum(jnp.sum(tile_start[:, None] >= ends[None, :], axis=1),
                              N_EXPERTS - 1).astype(jnp.int32)
    n_used = (ends[-1] // tm).astype(jnp.int32).reshape(1)

    idx = jnp.concatenate([pos, pos + n_slots], axis=1)
    xs = _dispatch_rows(hp.reshape(2 * n, d // 4), idx, 2 * n_slots).reshape(2, n_slots, d // 4)
    ys = _grouped_swiglu(xs, tile_expert, n_used, wg, wu, wd, layer)
    yab = _collect_rows(ys.reshape(2 * n_slots, d // 4), idx.reshape(1, 4 * n)).reshape(2, 2, n, d // 4)
    return _combine_ln(yab, route, x, mod, ln, mod_next, alpha)


def kernel(x, c, w_ada, b_ada, w_in, w_out, hgrn_lb_logits, hgrn_norm_gain,
           w_dense_gate, w_dense_up, w_dense_down, w_router, w_moe_gate, w_moe_up, w_moe_down,
           ln_gain, ln_bias):
    depth = w_in.shape[0]
    bsz, s, d = x.shape
    alpha = (2 * depth) ** 0.25

    mods = _ada_all(c, w_ada, b_ada).reshape(depth, 2, bsz, 3, d)
    lb_all = _lower_bounds(hgrn_lb_logits)
    ln_all = jnp.stack([ln_gain, ln_bias], axis=2)

    h = _modulate(x, mods[0, 0])
    for layer in range(depth):
        proj = _in_proj(h.reshape(bsz * s, d), w_in[layer].astype(BF16)).reshape(bsz, s, D_IN_PROJ)
        o_h = _hgrn(proj, lb_all[layer:layer + 1], hgrn_norm_gain[layer:layer + 1])
        o_s = _stick_breaking(proj)
        x, h = _out_proj_ln(o_h, o_s, x, w_out[layer].astype(BF16), mods[layer, 0],
                            ln_all[layer, 0], mods[layer, 1], alpha, packed=layer % 2 == 1)
        mod_next = mods[layer + 1, 0] if layer + 1 < depth else None
        j = layer // 2
        if layer % 2 == 0:
            res = _dense_ffn_ln(h, x, w_dense_gate[j].astype(BF16), w_dense_up[j].astype(BF16),
                                w_dense_down[j].astype(BF16), mods[layer, 1], ln_all[layer, 1],
                                mod_next, alpha)
        else:
            res = _moe_ffn_ln(h, x, w_router[j], w_moe_gate, w_moe_up, w_moe_down, j,
                              mods[layer, 1], ln_all[layer, 1], mod_next, alpha)
        if mod_next is not None:
            x, h = res
        else:
            (x,) = res
    return x
```

```python
import functools
import math

import jax
import jax.numpy as jnp
from jax import lax
from jax.experimental import pallas as pl
from jax.experimental.pallas import tpu as pltpu
from jax.experimental.pallas import tpu_sc as plsc

D_MODEL = 1024
D_HGRN = 512
HGRN_HEADS = 4
HGRN_DH = 128
D_SB = 512
SB_HEADS = 8
SB_DH = 64
D_IN_PROJ = 4 * D_HGRN + 3 * D_SB
N_EXPERTS = 8
LN_EPS = 1e-5
RMS_EPS = 1e-6

LOG2E = math.log2(math.e)
LANES = 128
VMEM_LIMIT = 52 * 1024 * 1024

HG_UNIT = 128
HG_BAND = 4
HG_PAD = 8
HG_LEVELS = (64, 32, 16, 8, 4)
SB_BLOCK = 128
SB_SKIP = 110.0 * LOG2E
SB_GROUP = 4
SB_WIN = 3
MOE_TILE = 1024
MOE_FF_TILE = 896
SC_WINDOW = 128

F32 = jnp.float32
BF16 = jnp.bfloat16
U32 = jnp.uint32


def _cparams(sem):
    return pltpu.CompilerParams(dimension_semantics=sem, vmem_limit_bytes=VMEM_LIMIT)


def _dot(a, b):
    return jnp.dot(a, b, preferred_element_type=F32)


def _dot_nt(a, b):
    return lax.dot_general(a, b, (((1,), (1,)), ((), ())), preferred_element_type=F32)


def _dot_tn(a, b):
    return lax.dot_general(a, b, (((0,), (0,)), ((), ())), preferred_element_type=F32)


def _pack_rows(v):
    w = v.shape[-1] // 2
    lo = lax.bitcast_convert_type(v[:, :w].astype(BF16).astype(F32), U32)
    hi = lax.bitcast_convert_type(v[:, w:].astype(BF16).astype(F32), U32)
    return (lo >> 16) | hi


def _unpack_rows(p):
    lo = lax.bitcast_convert_type(p << 16, F32)
    hi = lax.bitcast_convert_type(p & jnp.uint32(0xFFFF0000), F32)
    return jnp.concatenate([lo, hi], axis=-1)


def _store_planes(ref, p):
    w = p.shape[-1] // 2
    ref[0] = p[:, :w].reshape(ref.shape[1:])
    ref[1] = p[:, w:].reshape(ref.shape[1:])


def _load_planes(p0, p1):
    return _unpack_rows(jnp.concatenate([p0, p1], axis=-1))


def _sigmoid(x):
    return 1.0 / (1.0 + jnp.exp(-x))


def _silu(x):
    return x * _sigmoid(x)


def _ada_kernel(c_ref, w_ref, b_ref, o_ref):
    sc = _silu(c_ref[...])
    o_ref[0] = jnp.dot(sc, w_ref[0], preferred_element_type=F32,
                       precision=lax.Precision.HIGHEST) + b_ref[0]


def _ada_all(c, w_ada, b_ada):
    n_sub = w_ada.shape[0] * w_ada.shape[1]
    bsz, d = c.shape
    w = w_ada.reshape(n_sub, d, 3 * d)
    b = b_ada.reshape(n_sub, 1, 3 * d)
    tn = 1024
    return pl.pallas_call(
        _ada_kernel,
        grid=(n_sub, 3 * d // tn),
        in_specs=[pl.BlockSpec((bsz, d), lambda s, j: (0, 0)),
                  pl.BlockSpec((1, d, tn), lambda s, j: (s, 0, j)),
                  pl.BlockSpec((1, 1, tn), lambda s, j: (s, 0, j))],
        out_specs=pl.BlockSpec((1, bsz, tn), lambda s, j: (s, 0, j)),
        out_shape=jax.ShapeDtypeStruct((n_sub, bsz, 3 * d), F32),
        compiler_params=_cparams(("parallel", "parallel")),
        name="ada_modulation",
    )(c, w, b)


def _lb_kernel(l_ref, o_ref):
    x = l_ref[...]
    m = jnp.max(x, axis=0, keepdims=True)
    e = jnp.exp(x - m)
    p = e / jnp.sum(e, axis=0, keepdims=True)
    depth = x.shape[0]
    run = jnp.zeros_like(p[0:1])
    rows = []
    for i in range(depth):
        run = run + p[i:i + 1]
        rows.append(run - p[0:1])
    o_ref[...] = jnp.concatenate(rows, axis=0)


def _lower_bounds(lb_logits):
    return pl.pallas_call(
        _lb_kernel,
        out_shape=jax.ShapeDtypeStruct(lb_logits.shape, F32),
        name="hgrn_lower_bounds",
    )(lb_logits)


def _modulate_kernel(x_ref, m_ref, h_ref):
    m = m_ref[0]
    h_ref[0] = (x_ref[0] * (1.0 + m[1:2]) + m[0:1]).astype(BF16)


def _modulate(x, mod):
    bsz, s, d = x.shape
    ts = min(s, 1024)
    return pl.pallas_call(
        _modulate_kernel,
        grid=(bsz, s // ts),
        in_specs=[pl.BlockSpec((1, ts, d), lambda b, i: (b, i, 0)),
                  pl.BlockSpec((1, 3, d), lambda b, i: (b, 0, 0))],
        out_specs=pl.BlockSpec((1, ts, d), lambda b, i: (b, i, 0)),
        out_shape=jax.ShapeDtypeStruct((bsz, s, d), BF16),
        compiler_params=_cparams(("parallel", "parallel")),
        name="modulate_first",
    )(x, mod)


def _inproj_kernel(h_ref, w_ref, oh_ref, os_ref):
    h = h_ref[...]
    tn = 512
    nh = oh_ref.shape[1] // tn
    for j in range(nh):
        oh_ref[:, j * tn:(j + 1) * tn] = _dot(h, w_ref[:, j * tn:(j + 1) * tn]).astype(BF16)
    per = tn // LANES
    for j in range(os_ref.shape[0] // per):
        blk = _dot(h, w_ref[:, (nh + j) * tn:(nh + j + 1) * tn]).astype(BF16)
        for c in range(per):
            os_ref[j * per + c] = blk[:, c * LANES:(c + 1) * LANES]


def _in_proj(h2d, w_in):
    n, d = h2d.shape
    dout = w_in.shape[1]
    dh = 4 * D_HGRN
    n_slabs = (dout - dh) // LANES
    tm = min(n, 512)
    return pl.pallas_call(
        _inproj_kernel,
        grid=(n // tm,),
        in_specs=[pl.BlockSpec((tm, d), lambda i: (i, 0)),
                  pl.BlockSpec((d, dout), lambda i: (0, 0), pipeline_mode=pl.Buffered(1))],
        out_specs=[pl.BlockSpec((tm, dh), lambda i: (i, 0)),
                   pl.BlockSpec((n_slabs, tm, LANES), lambda i: (0, i, 0))],
        out_shape=[jax.ShapeDtypeStruct((n, dh), BF16),
                   jax.ShapeDtypeStruct((n_slabs, n, LANES), BF16)],
        compiler_params=_cparams(("parallel",)),
        name="in_proj",
    )(h2d, w_in)


def _hgrn_consts():
    t = jnp.arange(HG_UNIT)[:, None]
    s = jnp.arange(HG_UNIT)[None, :]
    tril = (s <= t).astype(BF16)
    lv = jnp.full((HG_UNIT, HG_UNIT), -1, jnp.int32)
    for idx, m in enumerate(HG_LEVELS):
        blk = 2 * m
        hit = (t // blk == s // blk) & (t % blk >= m) & (s % blk < m)
        lv = jnp.where(hit, idx, lv)
    return tril, lv


def _hgrn_kernel(q_ref, f_ref, i_ref, g_ref, lb_ref, gain_ref, tril_ref, lv_ref, o_ref,
                 states_ref, kpads_ref, bpads_ref, vpads_ref):
    @pl.when(pl.program_id(1) == 0)
    def _():
        states_ref[...] = jnp.zeros_like(states_ref)

    heads = range(HGRN_HEADS)
    zpad = jnp.zeros((HGRN_HEADS, HG_PAD, HGRN_DH), F32)
    kpads_ref[:, 0:HG_PAD, :] = zpad
    bpads_ref[:, 0:HG_PAD, :] = zpad
    vpads_ref[:, 0:HG_PAD, :] = zpad

    n_units = q_ref.shape[1] // HG_UNIT
    rowi = lax.broadcasted_iota(jnp.int32, (HG_UNIT, HGRN_DH), 0)
    tmod = jnp.bitwise_and(rowi, HG_BAND - 1)

    def unit(u, carry):
        rows = pl.ds(pl.multiple_of(u * HG_UNIT, HG_UNIT), HG_UNIT)
        lanes = [slice(h * HGRN_DH, (h + 1) * HGRN_DH) for h in heads]

        kk, qq, v, vb, b3 = [], [], [], [], []
        for h in heads:
            lb = lb_ref[:, lanes[h]]
            f = lb + (1.0 - lb) * _sigmoid(f_ref[0, rows, lanes[h]].astype(F32))
            logf = jnp.log2(f)
            kk.append(1.0 - f)
            qq.append(_silu(q_ref[0, rows, lanes[h]].astype(F32)))
            v.append(i_ref[0, rows, lanes[h]].astype(F32))
            vb.append(i_ref[0, rows, lanes[h]])
            p0 = logf.astype(BF16)
            r1 = logf - p0.astype(F32)
            p1 = r1.astype(BF16)
            p2 = (r1 - p1.astype(F32)).astype(BF16)
            b3.append(_dot(tril_ref[...], jnp.concatenate([p0, p1, p2], axis=1)))
        b = [t[:, 0:HGRN_DH] + t[:, HGRN_DH:2 * HGRN_DH] + t[:, 2 * HGRN_DH:] for t in b3]

        state = [states_ref[h] for h in heads]
        o = [_dot_nt((qq[h] * jnp.exp2(b[h])).astype(BF16), state[h].astype(BF16)) for h in heads]

        lv = lv_ref[...]
        scores = [jnp.zeros((HG_UNIT, HG_UNIT), F32) for _ in heads]
        for idx, m in enumerate(HG_LEVELS):
            blk = 2 * m
            nb = HG_UNIT // blk
            upper = jnp.bitwise_and(rowi, m) != 0
            xl = []
            for h in heads:
                ref = b[h].reshape(nb, blk, HGRN_DH)[:, m - 1:m, :]
                ref = jnp.broadcast_to(ref, (nb, blk, HGRN_DH)).reshape(HG_UNIT, HGRN_DH)
                xl.append((jnp.where(upper, qq[h], kk[h]) * jnp.exp2(-jnp.abs(b[h] - ref))).astype(BF16))
            s_l = [_dot_nt(x, x) for x in xl]
            scores = [jnp.where(lv == idx, s_l[h], scores[h]) for h in heads]
        intra = [_dot(scores[h].astype(BF16), vb[h]) for h in heads]

        for h in heads:
            kpads_ref[h, HG_PAD:, :] = kk[h]
            bpads_ref[h, HG_PAD:, :] = b[h]
            vpads_ref[h, HG_PAD:, :] = v[h]
        band = [jnp.zeros((HG_UNIT, HGRN_DH), F32) for _ in heads]
        for d in range(HG_BAND):
            win = pl.ds(HG_PAD - d, HG_UNIT)
            for h in heads:
                p = qq[h] * kpads_ref[h, win, :] * jnp.exp2(b[h] - bpads_ref[h, win, :])
                s_d = jnp.sum(p, axis=-1, keepdims=True)
                band[h] = band[h] + jnp.where(tmod >= d, s_d * vpads_ref[h, win, :], 0.0)

        for h in heads:
            b_last = b[h][HG_UNIT - 1:HG_UNIT, :]
            kdec = kk[h] * jnp.exp2(b_last - b[h])
            states_ref[h] = state[h] * jnp.exp2(b_last) + _dot_tn(vb[h], kdec.astype(BF16))
        for h in heads:
            oh = o[h] + intra[h] + band[h]
            oh = oh * lax.rsqrt(jnp.mean(oh * oh, axis=-1, keepdims=True) + RMS_EPS)
            gg = g_ref[0, rows, lanes[h]].astype(F32)
            o_ref[0, rows, lanes[h]] = (oh * gain_ref[:, lanes[h]] * _silu(gg)).astype(BF16)
        return carry

    lax.fori_loop(0, n_units, unit, 0)


def _hgrn(proj, lb, gain):
    bsz, s, _ = proj.shape
    ts = min(s, 512)
    tril, lv = _hgrn_consts()

    def col(off):
        return pl.BlockSpec((1, ts, D_HGRN), lambda b, i, off=off: (b, i, off))

    return pl.pallas_call(
        _hgrn_kernel,
        grid=(bsz, s // ts),
        in_specs=[col(0), col(1), col(2), col(3),
                  pl.BlockSpec((1, D_HGRN), lambda b, i: (0, 0)),
                  pl.BlockSpec((1, D_HGRN), lambda b, i: (0, 0)),
                  pl.BlockSpec((HG_UNIT, HG_UNIT), lambda b, i: (0, 0)),
                  pl.BlockSpec((HG_UNIT, HG_UNIT), lambda b, i: (0, 0))],
        out_specs=pl.BlockSpec((1, ts, D_HGRN), lambda b, i: (b, i, 0)),
        out_shape=jax.ShapeDtypeStruct((bsz, s, D_HGRN), BF16),
        scratch_shapes=[pltpu.VMEM((HGRN_HEADS, HGRN_DH, HGRN_DH), F32),
                        pltpu.VMEM((HGRN_HEADS, HG_UNIT + HG_PAD, HGRN_DH), F32),
                        pltpu.VMEM((HGRN_HEADS, HG_UNIT + HG_PAD, HGRN_DH), F32),
                        pltpu.VMEM((HGRN_HEADS, HG_UNIT + HG_PAD, HGRN_DH), F32)],
        compiler_params=_cparams(("parallel", "arbitrary")),
        name="hgrn2",
    )(proj, proj, proj, proj, lb, gain, tril, lv)


def _sb_kernel(q_ref, k_ref, v_ref, uo_ref, o_ref, kp_ref, vp_ref, c_ref, acc_ref):
    tq = SB_BLOCK
    nq = q_ref.shape[2] // tq
    pad = (SB_WIN - 1) * tq
    kp_ref[0:pad, :] = jnp.zeros((pad, LANES), BF16)
    vp_ref[0:pad, :] = jnp.zeros((pad, LANES), BF16)
    kp_ref[pad:, :] = k_ref[0, 0]
    vp_ref[pad:, :] = v_ref[0, 0]

    first = lax.broadcasted_iota(jnp.int32, (tq, LANES), 1) < SB_DH
    rowi = lax.broadcasted_iota(jnp.int32, (2 * tq, tq), 0)
    coli = lax.broadcasted_iota(jnp.int32, (2 * tq, tq), 1)
    strict = coli < jnp.bitwise_and(rowi, tq - 1)
    neg_inf = jnp.float32(-jnp.inf)

    def stacked_q(qi):
        rows = pl.ds(pl.multiple_of(qi * tq, tq), tq)
        q = q_ref[0, 0, rows, :] * (SB_DH ** -0.5)
        zero = jnp.zeros_like(q)
        return rows, jnp.concatenate([jnp.where(first, q, zero), jnp.where(first, zero, q)], axis=0)

    def window(q2, p_top, diag, crow, acc):
        (crow, acc), = windows([(q2, p_top, crow, acc)], diag)
        return crow, acc

    def windows(items, diag):
        order = list(reversed(range(SB_WIN)))
        zs, cts = [], []
        for q2, p_top, _, _ in items:
            start = pl.multiple_of((p_top - (SB_WIN - 1)) * tq, tq)
            zs.append(_dot_nt(q2, kp_ref[pl.ds(start, SB_WIN * tq), :]) * LOG2E)
        for z in zs:
            logn = -(jnp.maximum(z, 0.0) + jnp.log2(1.0 + jnp.exp2(-jnp.abs(z))))
            lhs = []
            for j in order:
                lj = logn[:, j * tq:(j + 1) * tq]
                if diag and j == SB_WIN - 1:
                    lj = jnp.where(strict, lj, 0.0)
                hi = lj.astype(BF16)
                lo = (lj - hi.astype(F32)).astype(BF16)
                lhs.append(jnp.concatenate([hi, lo], axis=1))
            cts.append(_dot(jnp.concatenate(lhs, axis=0), uo_ref[...]))
        outs = []
        for (q2, p_top, crow, acc), z, ct in zip(items, zs, cts):
            start = pl.multiple_of((p_top - (SB_WIN - 1)) * tq, tq)
            parts = [None] * SB_WIN
            for idx, j in enumerate(order):
                blk = ct[idx * 2 * tq:(idx + 1) * 2 * tq]
                a = jnp.exp2(z[:, j * tq:(j + 1) * tq] + (crow + blk))
                if diag and j == SB_WIN - 1:
                    a = jnp.where(strict, a, 0.0)
                parts[j] = a.astype(BF16)
                crow = crow + jnp.broadcast_to(blk[:, 0:1], blk.shape)
            acc = acc + _dot(jnp.concatenate(parts, axis=1), vp_ref[pl.ds(start, SB_WIN * tq), :])
            outs.append((crow, acc))
        return outs

    def emit(rows, acc):
        o_ref[0, 0, rows, :] = jnp.where(first, acc[0:tq], acc[tq:]).astype(BF16)

    def fast(g, worst):
        zero = jnp.zeros((2 * tq, LANES), F32)
        qis = [g * SB_GROUP + t for t in range(SB_GROUP)]
        prep = [stacked_q(qi) for qi in qis]
        res = windows([(q2, qi + SB_WIN - 1, zero, zero) for qi, (_, q2) in zip(qis, prep)], True)
        for qi, (rows, _), (crow, acc) in zip(qis, prep, res):
            emit(rows, acc)
            c_ref[qi] = crow
            acc_ref[qi] = acc
            worst = jnp.maximum(worst, jnp.where(qi >= SB_WIN, crow, neg_inf))
        return worst

    worst = lax.fori_loop(0, nq // SB_GROUP, fast, jnp.full((2 * tq, LANES), neg_inf, F32))

    @pl.when(jnp.max(worst) > -SB_SKIP)
    def _():
        def finish(qi, carry):
            rows, q2 = stacked_q(qi)

            def cond(c):
                p_top, mx, _, _ = c
                return jnp.logical_and(p_top >= SB_WIN - 1, mx > -SB_SKIP)

            def body(c):
                p_top, _, crow, acc = c
                crow, acc = window(q2, p_top, False, crow, acc)
                return p_top - SB_WIN, jnp.max(crow), crow, acc

            crow0 = c_ref[qi]
            _, _, _, acc = lax.while_loop(cond, body, (qi - 1, jnp.max(crow0), crow0, acc_ref[qi]))
            emit(rows, acc)
            return carry

        lax.fori_loop(SB_WIN, nq, finish, 0)


def _stick_breaking(qkv):
    _, bsz, s, _ = qkv.shape
    tq = SB_BLOCK
    nq = s // tq
    assert nq % SB_GROUP == 0
    n_pairs = D_SB // LANES
    jj = jnp.arange(tq)
    suffix = (jj[:, None] >= jj[None, :]).astype(BF16)
    uo = jnp.concatenate([suffix, suffix], axis=0)

    def seq(off):
        return pl.BlockSpec((1, 1, s, LANES), lambda b, p, off=off: (off * n_pairs + p, b, 0, 0))

    return pl.pallas_call(
        _sb_kernel,
        grid=(bsz, n_pairs),
        in_specs=[seq(0), seq(1), seq(2),
                  pl.BlockSpec((2 * tq, tq), lambda b, p: (0, 0))],
        out_specs=pl.BlockSpec((1, 1, s, LANES), lambda b, p: (p, b, 0, 0)),
        out_shape=jax.ShapeDtypeStruct((n_pairs, bsz, s, LANES), BF16),
        scratch_shapes=[pltpu.VMEM((s + (SB_WIN - 1) * tq, LANES), BF16),
                        pltpu.VMEM((s + (SB_WIN - 1) * tq, LANES), BF16),
                        pltpu.VMEM((nq, 2 * tq, LANES), F32),
                        pltpu.VMEM((nq, 2 * tq, LANES), F32)],
        compiler_params=_cparams(("parallel", "parallel")),
        name="stick_breaking",
    )(qkv, qkv, qkv, uo)


def _residual_ln(x, y, gate, gain, bias, alpha):
    r = alpha * x + gate * y
    mu = jnp.mean(r, axis=-1, keepdims=True)
    rc = r - mu
    var = jnp.mean(rc * rc, axis=-1, keepdims=True)
    return rc * lax.rsqrt(var + LN_EPS) * gain + bias


def _sb_lanes(os_ref):
    return jnp.concatenate([os_ref[p, 0] for p in range(os_ref.shape[0])], axis=-1)


def _emit(xn, outs, mod_next_ref, packed=False):
    outs[0][0] = xn
    if mod_next_ref is not None:
        mn = mod_next_ref[0]
        hn = xn * (1.0 + mn[1:2]) + mn[0:1]
        if packed:
            _store_planes(outs[1], _pack_rows(hn))
        else:
            outs[1][0] = hn.astype(BF16)


def _outproj_kernel(alpha, has_next, packed, oh_ref, os_ref, x_ref, w_ref, mod_ref, ln_ref, *rest):
    mod_next_ref = rest[0] if has_next else None
    outs = rest[1:] if has_next else rest
    y = _dot(oh_ref[0], w_ref[0:D_HGRN, :]) + _dot(_sb_lanes(os_ref), w_ref[D_HGRN:, :])
    ln = ln_ref[...]
    xn = _residual_ln(x_ref[0], y, mod_ref[0][2:3], ln[0:1], ln[1:2], alpha)
    _emit(xn, outs, mod_next_ref, packed)


def _row_specs(bsz, s, d, tm, mod_next, packed=False):
    x_spec = pl.BlockSpec((1, tm, d), lambda b, i: (b, i, 0))
    mod_spec = pl.BlockSpec((1, 3, d), lambda b, i: (b, 0, 0))
    ln_spec = pl.BlockSpec((2, d), lambda b, i: (0, 0))
    out_specs = [pl.BlockSpec((1, tm, d), lambda b, i: (b, i, 0))]
    out_shape = [jax.ShapeDtypeStruct((bsz, s, d), F32)]
    if mod_next is not None and packed:
        out_specs.append(pl.BlockSpec((2, 1, tm, d // 4), lambda b, i: (0, b, i, 0)))
        out_shape.append(jax.ShapeDtypeStruct((2, bsz, s, d // 4), U32))
    elif mod_next is not None:
        out_specs.append(pl.BlockSpec((1, tm, d), lambda b, i: (b, i, 0)))
        out_shape.append(jax.ShapeDtypeStruct((bsz, s, d), BF16))
    return x_spec, mod_spec, ln_spec, out_specs, out_shape


def _out_proj_ln(o_h, o_s, x, w_out, mod, ln, mod_next, alpha, packed=True):
    bsz, s, d = x.shape
    tm = min(s, 512)
    x_spec, mod_spec, ln_spec, out_specs, out_shape = _row_specs(bsz, s, d, tm, mod_next, packed)
    in_specs = [pl.BlockSpec((1, tm, D_HGRN), lambda b, i: (b, i, 0)),
                pl.BlockSpec((D_SB // LANES, 1, tm, LANES), lambda b, i: (0, b, i, 0)),
                x_spec,
                pl.BlockSpec((d, d), lambda b, i: (0, 0), pipeline_mode=pl.Buffered(1)),
                mod_spec, ln_spec]
    args = [o_h, o_s, x, w_out, mod, ln]
    if mod_next is not None:
        in_specs.append(mod_spec)
        args.append(mod_next)
    return pl.pallas_call(
        functools.partial(_outproj_kernel, alpha, mod_next is not None, packed),
        grid=(bsz, s // tm),
        in_specs=in_specs, out_specs=out_specs, out_shape=out_shape,
        compiler_params=_cparams(("parallel", "parallel")),
        name="out_proj_ln",
    )(*args)


def _mix_dense_kernel(alpha, has_next, n_chunks, oh_ref, os_ref, x_ref, wo_ref, wg_ref, wu_ref, wd_ref,
                      mod0_ref, mod1_ref, ln0_ref, ln1_ref, *rest):
    mod_next_ref = rest[0] if has_next else None
    outs = rest[1:] if has_next else rest
    y = _dot(oh_ref[0], wo_ref[0:D_HGRN, :]) + _dot(_sb_lanes(os_ref), wo_ref[D_HGRN:, :])
    ln0 = ln0_ref[...]
    x1 = _residual_ln(x_ref[0], y, mod0_ref[0][2:3], ln0[0:1], ln0[1:2], alpha)
    m1 = mod1_ref[0]
    h = (x1 * (1.0 + m1[1:2]) + m1[0:1]).astype(BF16)
    ff = wg_ref.shape[1]
    tf = ff // n_chunks
    y = None
    for j in range(n_chunks):
        cols = slice(j * tf, (j + 1) * tf)
        g = _dot(h, wg_ref[:, cols])
        u = _dot(h, wu_ref[:, cols])
        part = _dot((_silu(g) * u).astype(BF16), wd_ref[cols, :])
        y = part if y is None else y + part
    ln1 = ln1_ref[...]
    x2 = _residual_ln(x1, y, m1[2:3], ln1[0:1], ln1[1:2], alpha)
    _emit(x2, outs, mod_next_ref)


def _mix_dense_ln(o_h, o_s, x, w_out, wg, wu, wd, mod0, mod1, ln0, ln1, mod_next, alpha):
    bsz, s, d = x.shape
    ff = wg.shape[1]
    tm = min(s, 512)
    x_spec, mod_spec, ln_spec, out_specs, out_shape = _row_specs(bsz, s, d, tm, mod_next)
    once = pl.Buffered(1)
    in_specs = [pl.BlockSpec((1, tm, D_HGRN), lambda b, i: (b, i, 0)),
                pl.BlockSpec((D_SB // LANES, 1, tm, LANES), lambda b, i: (0, b, i, 0)),
                x_spec,
                pl.BlockSpec((d, d), lambda b, i: (0, 0), pipeline_mode=once),
                pl.BlockSpec((d, ff), lambda b, i: (0, 0), pipeline_mode=once),
                pl.BlockSpec((d, ff), lambda b, i: (0, 0), pipeline_mode=once),
                pl.BlockSpec((ff, d), lambda b, i: (0, 0), pipeline_mode=once),
                mod_spec, mod_spec, ln_spec, ln_spec]
    args = [o_h, o_s, x, w_out, wg, wu, wd, mod0, mod1, ln0, ln1]
    if mod_next is not None:
        in_specs.append(mod_spec)
        args.append(mod_next)
    n_chunks = 2 if (ff // 2) % LANES == 0 else 1
    return pl.pallas_call(
        functools.partial(_mix_dense_kernel, alpha, mod_next is not None, n_chunks),
        grid=(bsz, s // tm),
        in_specs=in_specs, out_specs=out_specs, out_shape=out_shape,
        compiler_params=_cparams(("parallel", "parallel")),
        name="out_proj_dense_ffn_ln",
    )(*args)


def _router_kernel(h_ref, w_ref, tri_ref, route_ref, count_ref, run_ref):
    @pl.when(pl.program_id(0) == 0)
    def _():
        run_ref[...] = jnp.zeros_like(run_ref)

    tm = h_ref.shape[1]
    lane = lax.broadcasted_iota(jnp.int32, (tm, LANES), 1)
    logits = _dot(_load_planes(h_ref[0], h_ref[1]).astype(BF16), w_ref[...])
    neg = jnp.float32(-jnp.inf)
    logits = jnp.where(lane < N_EXPERTS, logits, neg)
    m1 = jnp.max(logits, axis=-1, keepdims=True)
    i1 = jnp.min(jnp.where(logits == m1, lane, LANES), axis=-1, keepdims=True)
    rest = jnp.where(lane == i1, neg, logits)
    m2 = jnp.max(rest, axis=-1, keepdims=True)
    i2 = jnp.min(jnp.where(rest == m2, lane, LANES), axis=-1, keepdims=True)
    e21 = jnp.exp(m2 - m1)
    w1 = 1.0 / (1.0 + e21)
    w2 = e21 / (1.0 + e21)

    hot1 = lane == i1
    hot2 = lane == i2
    both = jnp.where(jnp.logical_or(hot1, hot2), 1.0, 0.0)
    before = _dot(tri_ref[...], both.astype(BF16)) + run_ref[...]
    r1 = jnp.sum(jnp.where(hot1, before, 0.0), axis=-1, keepdims=True)
    r2 = jnp.sum(jnp.where(hot2, before, 0.0), axis=-1, keepdims=True)
    run_ref[...] = run_ref[...] + jnp.sum(both, axis=0, keepdims=True)
    count_ref[...] = run_ref[...]

    cols = [i1.astype(F32), i2.astype(F32), r1, r2, w1, w2]
    route = jnp.zeros((tm, LANES), F32)
    for c, val in enumerate(cols):
        route = jnp.where(lane == c, val, route)
    route_ref[...] = route[:, 0:8]


def _router(hp, w_router):
    _, n, pw = hp.shape
    d = 4 * pw
    tm = min(n, 512)
    w_pad = jnp.zeros((d, LANES), BF16).at[:, :N_EXPERTS].set(w_router.astype(BF16))
    jj = jnp.arange(tm)
    tri = (jj[None, :] < jj[:, None]).astype(BF16)
    return pl.pallas_call(
        _router_kernel,
        grid=(n // tm,),
        in_specs=[pl.BlockSpec((2, tm, pw), lambda i: (0, i, 0)),
                  pl.BlockSpec((d, LANES), lambda i: (0, 0)),
                  pl.BlockSpec((tm, tm), lambda i: (0, 0))],
        out_specs=[pl.BlockSpec((tm, 8), lambda i: (i, 0)),
                   pl.BlockSpec((1, LANES), lambda i: (0, 0))],
        out_shape=[jax.ShapeDtypeStruct((n, 8), F32),
                   jax.ShapeDtypeStruct((1, LANES), F32)],
        scratch_shapes=[pltpu.VMEM((1, LANES), F32)],
        compiler_params=_cparams(("arbitrary",)),
        name="moe_router",
    )(hp, w_pad, tri)


def _gmm_kernel(te_ref, nt_ref, x_ref, wg_ref, wu_ref, wd_ref, o_ref, acc_ref, xb_ref):
    i = pl.program_id(0)
    j = pl.program_id(1)
    nj = pl.num_programs(1)
    live = i < nt_ref[0]

    @pl.when(jnp.logical_and(live, j == 0))
    def _():
        acc_ref[...] = jnp.zeros_like(acc_ref)
        xb_ref[...] = _load_planes(x_ref[0], x_ref[1]).astype(BF16)

    @pl.when(live)
    def _():
        x = xb_ref[...]
        g = _dot(x, wg_ref[0, 0].astype(BF16))
        u = _dot(x, wu_ref[0, 0].astype(BF16))
        acc_ref[...] += _dot((_silu(g) * u).astype(BF16), wd_ref[0, 0].astype(BF16))

    @pl.when(jnp.logical_and(live, j == nj - 1))
    def _():
        _store_planes(o_ref, _pack_rows(acc_ref[...]))

    @pl.when(jnp.logical_and(jnp.logical_not(live), j == nj - 1))
    def _():
        o_ref[...] = jnp.zeros_like(o_ref)


def _grouped_swiglu(xs, tile_expert, n_tiles_used, wg, wu, wd, layer):
    _, n_slots, pw = xs.shape
    d = 4 * pw
    ff = wg.shape[3]
    tm, tf = MOE_TILE, MOE_FF_TILE
    n_tiles = n_slots // tm

    def row_map(i, j, te, nt):
        return (0, jnp.minimum(i, nt[0] - 1), 0)

    def ff_block(i, j, te, nt):
        return jnp.where(i < nt[0], j, ff // tf - 1)

    grid_spec = pltpu.PrefetchScalarGridSpec(
        num_scalar_prefetch=2,
        grid=(n_tiles, ff // tf),
        in_specs=[pl.BlockSpec((2, tm, pw), row_map),
                  pl.BlockSpec((1, 1, d, tf), lambda i, j, te, nt: (layer, te[i], 0, ff_block(i, j, te, nt))),
                  pl.BlockSpec((1, 1, d, tf), lambda i, j, te, nt: (layer, te[i], 0, ff_block(i, j, te, nt))),
                  pl.BlockSpec((1, 1, tf, d), lambda i, j, te, nt: (layer, te[i], ff_block(i, j, te, nt), 0))],
        out_specs=pl.BlockSpec((2, tm, pw), lambda i, j, te, nt: (0, i, 0)),
        scratch_shapes=[pltpu.VMEM((tm, d), F32), pltpu.VMEM((tm, d), BF16)],
    )
    return pl.pallas_call(
        _gmm_kernel,
        grid_spec=grid_spec,
        out_shape=jax.ShapeDtypeStruct((2, n_slots, pw), U32),
        compiler_params=_cparams(("arbitrary", "arbitrary")),
        name="moe_grouped_swiglu",
    )(tile_expert, n_tiles_used, xs, wg, wu, wd)


def _combine_kernel(alpha, has_next, ya_ref, yb_ref, r_ref, x_ref, mod_ref, ln_ref, *rest):
    mod_next_ref = rest[0] if has_next else None
    outs = rest[1:] if has_next else rest
    r = r_ref[0]
    y = (r[:, 4:5] * _load_planes(ya_ref[0, 0, 0], ya_ref[0, 1, 0])
         + r[:, 5:6] * _load_planes(yb_ref[0, 0, 0], yb_ref[0, 1, 0]))
    ln = ln_ref[...]
    xn = _residual_ln(x_ref[0], y, mod_ref[0][2:3], ln[0:1], ln[1:2], alpha)
    _emit(xn, outs, mod_next_ref)


def _combine_ln(yab, route, x, mod, ln, mod_next, alpha):
    bsz, s, d = x.shape
    tm = min(s, 512)
    x_spec, mod_spec, ln_spec, out_specs, out_shape = _row_specs(bsz, s, d, tm, mod_next)
    yab = yab.reshape(2, 2, bsz, s, d // 4)
    in_specs = [pl.BlockSpec((1, 2, 1, tm, d // 4), lambda b, i: (0, 0, b, i, 0)),
                pl.BlockSpec((1, 2, 1, tm, d // 4), lambda b, i: (1, 0, b, i, 0)),
                pl.BlockSpec((1, tm, 8), lambda b, i: (b, i, 0)),
                x_spec, mod_spec, ln_spec]
    args = [yab, yab, route.reshape(bsz, s, 8), x, mod, ln]
    if mod_next is not None:
        in_specs.append(mod_spec)
        args.append(mod_next)
    return pl.pallas_call(
        functools.partial(_combine_kernel, alpha, mod_next is not None),
        grid=(bsz, s // tm),
        in_specs=in_specs, out_specs=out_specs, out_shape=out_shape,
        compiler_params=_cparams(("parallel", "parallel")),
        name="moe_combine_ln",
    )(*args)


def _sc_mesh():
    return plsc.VectorSubcoreMesh(core_axis_name="c", subcore_axis_name="s")


def _dispatch_rows(rows, idx, n_out):
    m, w = rows.shape

    @pl.kernel(out_type=jax.ShapeDtypeStruct((n_out, w), rows.dtype), mesh=_sc_mesh(), scratch_types=[])
    def scatter(rows_hbm, idx_hbm, out_hbm):
        def body(rows_vmem, idx0_vmem, idx1_vmem):
            pltpu.sync_copy(rows_vmem, out_hbm.at[idx0_vmem.at[0]])
            pltpu.sync_copy(rows_vmem, out_hbm.at[idx1_vmem.at[0]])

        pltpu.emit_pipeline(
            body, grid=(m // SC_WINDOW,),
            in_specs=[pl.BlockSpec((SC_WINDOW, w), lambda i: (i, 0)),
                      pl.BlockSpec((1, SC_WINDOW), lambda i: (0, i)),
                      pl.BlockSpec((1, SC_WINDOW), lambda i: (1, i))],
            out_specs=[],
            core_axis_name=("c", "s"),
            dimension_semantics=(pltpu.PARALLEL,),
        )(rows_hbm, idx_hbm, idx_hbm)

    return scatter(rows, idx)


def _collect_rows(rows, idx):
    m = idx.shape[1]
    w = rows.shape[1]

    @pl.kernel(out_type=jax.ShapeDtypeStruct((m, w), rows.dtype), mesh=_sc_mesh(), scratch_types=[])
    def gather(rows_hbm, idx_hbm, out_hbm):
        def body(idx_vmem, out_vmem):
            pltpu.sync_copy(rows_hbm.at[idx_vmem.at[0]], out_vmem)

        pltpu.emit_pipeline(
            body, grid=(m // SC_WINDOW,),
            in_specs=[pl.BlockSpec((1, SC_WINDOW), lambda i: (0, i))],
            out_specs=[pl.BlockSpec((SC_WINDOW, w), lambda i: (i, 0))],
            core_axis_name=("c", "s"),
            dimension_semantics=(pltpu.PARALLEL,),
        )(idx_hbm, out_hbm)

    return gather(rows, idx)


def _moe_ffn_ln(h, x, w_router, wg, wu, wd, layer, mod, ln, mod_next, alpha):
    bsz, s, d = x.shape
    n = bsz * s
    hp = h.reshape(2, n, d // 4)
    route, counts = _router(hp, w_router)

    tm = MOE_TILE
    cnt = counts[0, :N_EXPERTS].astype(jnp.int32)
    padded = ((cnt + tm - 1) // tm) * tm
    ends = jnp.cumsum(padded)
    offs = ends - padded
    e = route[:, 0:2].astype(jnp.int32)
    rank = route[:, 2:4].astype(jnp.int32)
    pos = (jnp.take(offs, e) + rank).T
    n_tiles = (2 * n) // tm + N_EXPERTS
    n_slots = n_tiles * tm
    tile_start = jnp.arange(n_tiles, dtype=jnp.int32) * tm
    tile_expert = jnp.minimum(jnp.sum(tile_start[:, None] >= ends[None, :], axis=1),
                              N_EXPERTS - 1).astype(jnp.int32)
    n_used = (ends[-1] // tm).astype(jnp.int32).reshape(1)

    idx = jnp.concatenate([pos, pos + n_slots], axis=1)
    xs = _dispatch_rows(hp.reshape(2 * n, d // 4), idx, 2 * n_slots).reshape(2, n_slots, d // 4)
    ys = _grouped_swiglu(xs, tile_expert, n_used, wg, wu, wd, layer)
    yab = _collect_rows(ys.reshape(2 * n_slots, d // 4), idx.reshape(1, 4 * n)).reshape(2, 2, n, d // 4)
    return _combine_ln(yab, route, x, mod, ln, mod_next, alpha)


def kernel(x, c, w_ada, b_ada, w_in, w_out, hgrn_lb_logits, hgrn_norm_gain,
           w_dense_gate, w_dense_up, w_dense_down, w_router, w_moe_gate, w_moe_up, w_moe_down,
           ln_gain, ln_bias):
    depth = w_in.shape[0]
    bsz, s, d = x.shape
    alpha = (2 * depth) ** 0.25

    mods = _ada_all(c, w_ada, b_ada).reshape(depth, 2, bsz, 3, d)
    lb_all = _lower_bounds(hgrn_lb_logits)
    ln_all = jnp.stack([ln_gain, ln_bias], axis=2)

    h = _modulate(x, mods[0, 0])
    for layer in range(depth):
        proj_h, proj_s = _in_proj(h.reshape(bsz * s, d), w_in[layer].astype(BF16))
        o_h = _hgrn(proj_h.reshape(bsz, s, 4 * D_HGRN), lb_all[layer:layer + 1],
                    hgrn_norm_gain[layer:layer + 1])
        o_s = _stick_breaking(proj_s.reshape(-1, bsz, s, LANES))
        mod_next = mods[layer + 1, 0] if layer + 1 < depth else None
        j = layer // 2
        if layer % 2 == 0:
            res = _mix_dense_ln(o_h, o_s, x, w_out[layer].astype(BF16), w_dense_gate[j].astype(BF16),
                                w_dense_up[j].astype(BF16), w_dense_down[j].astype(BF16),
                                mods[layer, 0], mods[layer, 1], ln_all[layer, 0], ln_all[layer, 1],
                                mod_next, alpha)
        else:
            x, h = _out_proj_ln(o_h, o_s, x, w_out[layer].astype(BF16), mods[layer, 0],
                                ln_all[layer, 0], mods[layer, 1], alpha)
            res = _moe_ffn_ln(h, x, w_router[j], w_moe_gate, w_moe_up, w_moe_down, j,
                              mods[layer, 1], ln_all[layer, 1], mod_next, alpha)
        if mod_next is not None:
            x, h = res
        else:
            (x,) = res
    return x
```

```python
import functools
import math

import jax
import jax.numpy as jnp
from jax import lax
from jax.experimental import pallas as pl
from jax.experimental.pallas import tpu as pltpu
from jax.experimental.pallas import tpu_sc as plsc

D_MODEL = 1024
D_HGRN = 512
HGRN_HEADS = 4
HGRN_DH = 128
D_SB = 512
SB_HEADS = 8
SB_DH = 64
D_IN_PROJ = 4 * D_HGRN + 3 * D_SB
N_EXPERTS = 8
LN_EPS = 1e-5
RMS_EPS = 1e-6

LOG2E = math.log2(math.e)
LANES = 128
VMEM_LIMIT = 52 * 1024 * 1024

HG_UNIT = 128
HG_BAND = 4
HG_PAD = 8
HG_LEVELS = (64, 32, 16, 8, 4)
SB_BLOCK = 128
SB_SKIP = 110.0 * LOG2E
SB_GROUP = 4
SB_WIN = 3
MOE_TILE = 1024
MOE_FF_TILE = 512
SC_WINDOW = 128

F32 = jnp.float32
BF16 = jnp.bfloat16
U32 = jnp.uint32


def _cparams(sem):
    return pltpu.CompilerParams(dimension_semantics=sem, vmem_limit_bytes=VMEM_LIMIT)


def _dot(a, b):
    return jnp.dot(a, b, preferred_element_type=F32)


def _dot_nt(a, b):
    return lax.dot_general(a, b, (((1,), (1,)), ((), ())), preferred_element_type=F32)


def _dot_tn(a, b):
    return lax.dot_general(a, b, (((0,), (0,)), ((), ())), preferred_element_type=F32)


def _pack_rows(v):
    w = v.shape[-1] // 2
    lo = lax.bitcast_convert_type(v[:, :w].astype(BF16).astype(F32), U32)
    hi = lax.bitcast_convert_type(v[:, w:].astype(BF16).astype(F32), U32)
    return (lo >> 16) | hi


def _unpack_rows(p):
    lo = lax.bitcast_convert_type(p << 16, F32)
    hi = lax.bitcast_convert_type(p & jnp.uint32(0xFFFF0000), F32)
    return jnp.concatenate([lo, hi], axis=-1)


def _store_planes(ref, p):
    w = p.shape[-1] // 2
    ref[0] = p[:, :w].reshape(ref.shape[1:])
    ref[1] = p[:, w:].reshape(ref.shape[1:])


def _load_planes(p0, p1):
    return _unpack_rows(jnp.concatenate([p0, p1], axis=-1))


def _sigmoid(x):
    return 1.0 / (1.0 + jnp.exp(-x))


def _silu(x):
    return x * _sigmoid(x)


def _ada_kernel(c_ref, w_ref, b_ref, o_ref):
    sc = _silu(c_ref[...])
    o_ref[0] = jnp.dot(sc, w_ref[0], preferred_element_type=F32,
                       precision=lax.Precision.HIGHEST) + b_ref[0]


def _ada_all(c, w_ada, b_ada):
    n_sub = w_ada.shape[0] * w_ada.shape[1]
    bsz, d = c.shape
    w = w_ada.reshape(n_sub, d, 3 * d)
    b = b_ada.reshape(n_sub, 1, 3 * d)
    tn = 1024
    return pl.pallas_call(
        _ada_kernel,
        grid=(n_sub, 3 * d // tn),
        in_specs=[pl.BlockSpec((bsz, d), lambda s, j: (0, 0)),
                  pl.BlockSpec((1, d, tn), lambda s, j: (s, 0, j)),
                  pl.BlockSpec((1, 1, tn), lambda s, j: (s, 0, j))],
        out_specs=pl.BlockSpec((1, bsz, tn), lambda s, j: (s, 0, j)),
        out_shape=jax.ShapeDtypeStruct((n_sub, bsz, 3 * d), F32),
        compiler_params=_cparams(("parallel", "parallel")),
        name="ada_modulation",
    )(c, w, b)


def _lb_kernel(l_ref, o_ref):
    x = l_ref[...]
    m = jnp.max(x, axis=0, keepdims=True)
    e = jnp.exp(x - m)
    p = e / jnp.sum(e, axis=0, keepdims=True)
    depth = x.shape[0]
    run = jnp.zeros_like(p[0:1])
    rows = []
    for i in range(depth):
        run = run + p[i:i + 1]
        rows.append(run - p[0:1])
    o_ref[...] = jnp.concatenate(rows, axis=0)


def _lower_bounds(lb_logits):
    return pl.pallas_call(
        _lb_kernel,
        out_shape=jax.ShapeDtypeStruct(lb_logits.shape, F32),
        name="hgrn_lower_bounds",
    )(lb_logits)


def _modulate_kernel(x_ref, m_ref, h_ref):
    m = m_ref[0]
    h_ref[0] = (x_ref[0] * (1.0 + m[1:2]) + m[0:1]).astype(BF16)


def _modulate(x, mod):
    bsz, s, d = x.shape
    ts = min(s, 1024)
    return pl.pallas_call(
        _modulate_kernel,
        grid=(bsz, s // ts),
        in_specs=[pl.BlockSpec((1, ts, d), lambda b, i: (b, i, 0)),
                  pl.BlockSpec((1, 3, d), lambda b, i: (b, 0, 0))],
        out_specs=pl.BlockSpec((1, ts, d), lambda b, i: (b, i, 0)),
        out_shape=jax.ShapeDtypeStruct((bsz, s, d), BF16),
        compiler_params=_cparams(("parallel", "parallel")),
        name="modulate_first",
    )(x, mod)


def _inproj_kernel(h_ref, w_ref, oh_ref, os_ref):
    h = h_ref[...]
    tn = 512
    nh = oh_ref.shape[1] // tn
    for j in range(nh):
        oh_ref[:, j * tn:(j + 1) * tn] = _dot(h, w_ref[:, j * tn:(j + 1) * tn]).astype(BF16)
    per = tn // LANES
    for j in range(os_ref.shape[0] // per):
        blk = _dot(h, w_ref[:, (nh + j) * tn:(nh + j + 1) * tn]).astype(BF16)
        for c in range(per):
            os_ref[j * per + c] = blk[:, c * LANES:(c + 1) * LANES]


def _in_proj(h2d, w_in):
    n, d = h2d.shape
    dout = w_in.shape[1]
    dh = 4 * D_HGRN
    n_slabs = (dout - dh) // LANES
    tm = min(n, 512)
    return pl.pallas_call(
        _inproj_kernel,
        grid=(n // tm,),
        in_specs=[pl.BlockSpec((tm, d), lambda i: (i, 0)),
                  pl.BlockSpec((d, dout), lambda i: (0, 0), pipeline_mode=pl.Buffered(1))],
        out_specs=[pl.BlockSpec((tm, dh), lambda i: (i, 0)),
                   pl.BlockSpec((n_slabs, tm, LANES), lambda i: (0, i, 0))],
        out_shape=[jax.ShapeDtypeStruct((n, dh), BF16),
                   jax.ShapeDtypeStruct((n_slabs, n, LANES), BF16)],
        compiler_params=_cparams(("parallel",)),
        name="in_proj",
    )(h2d, w_in)


def _hgrn_consts():
    t = jnp.arange(HG_UNIT)[:, None]
    s = jnp.arange(HG_UNIT)[None, :]
    tril = (s <= t).astype(BF16)
    lv = jnp.full((HG_UNIT, HG_UNIT), -1, jnp.int32)
    for idx, m in enumerate(HG_LEVELS):
        blk = 2 * m
        hit = (t // blk == s // blk) & (t % blk >= m) & (s % blk < m)
        lv = jnp.where(hit, idx, lv)
    return tril, lv


def _hgrn_kernel(q_ref, f_ref, i_ref, g_ref, lb_ref, gain_ref, tril_ref, lv_ref, o_ref,
                 states_ref, kpads_ref, bpads_ref, vpads_ref):
    @pl.when(pl.program_id(1) == 0)
    def _():
        states_ref[...] = jnp.zeros_like(states_ref)

    heads = range(HGRN_HEADS)
    zpad = jnp.zeros((HGRN_HEADS, HG_PAD, HGRN_DH), F32)
    kpads_ref[:, 0:HG_PAD, :] = zpad
    bpads_ref[:, 0:HG_PAD, :] = zpad
    vpads_ref[:, 0:HG_PAD, :] = zpad

    n_units = q_ref.shape[1] // HG_UNIT
    rowi = lax.broadcasted_iota(jnp.int32, (HG_UNIT, HGRN_DH), 0)
    tmod = jnp.bitwise_and(rowi, HG_BAND - 1)

    def unit(u, carry):
        rows = pl.ds(pl.multiple_of(u * HG_UNIT, HG_UNIT), HG_UNIT)
        lanes = [slice(h * HGRN_DH, (h + 1) * HGRN_DH) for h in heads]

        kk, qq, v, vb, b3 = [], [], [], [], []
        for h in heads:
            lb = lb_ref[:, lanes[h]]
            f = lb + (1.0 - lb) * _sigmoid(f_ref[0, rows, lanes[h]].astype(F32))
            logf = jnp.log2(f)
            kk.append(1.0 - f)
            qq.append(_silu(q_ref[0, rows, lanes[h]].astype(F32)))
            v.append(i_ref[0, rows, lanes[h]].astype(F32))
            vb.append(i_ref[0, rows, lanes[h]])
            p0 = logf.astype(BF16)
            r1 = logf - p0.astype(F32)
            p1 = r1.astype(BF16)
            p2 = (r1 - p1.astype(F32)).astype(BF16)
            b3.append(_dot(tril_ref[...], jnp.concatenate([p0, p1, p2], axis=1)))
        b = [t[:, 0:HGRN_DH] + t[:, HGRN_DH:2 * HGRN_DH] + t[:, 2 * HGRN_DH:] for t in b3]

        state = [states_ref[h] for h in heads]
        o = [_dot_nt((qq[h] * jnp.exp2(b[h])).astype(BF16), state[h].astype(BF16)) for h in heads]

        lv = lv_ref[...]
        scores = [jnp.zeros((HG_UNIT, HG_UNIT), F32) for _ in heads]
        for idx, m in enumerate(HG_LEVELS):
            blk = 2 * m
            nb = HG_UNIT // blk
            upper = jnp.bitwise_and(rowi, m) != 0
            xl = []
            for h in heads:
                ref = b[h].reshape(nb, blk, HGRN_DH)[:, m - 1:m, :]
                ref = jnp.broadcast_to(ref, (nb, blk, HGRN_DH)).reshape(HG_UNIT, HGRN_DH)
                xl.append((jnp.where(upper, qq[h], kk[h]) * jnp.exp2(-jnp.abs(b[h] - ref))).astype(BF16))
            s_l = [_dot_nt(x, x) for x in xl]
            scores = [jnp.where(lv == idx, s_l[h], scores[h]) for h in heads]
        intra = [_dot(scores[h].astype(BF16), vb[h]) for h in heads]

        for h in heads:
            kpads_ref[h, HG_PAD:, :] = kk[h]
            bpads_ref[h, HG_PAD:, :] = b[h]
            vpads_ref[h, HG_PAD:, :] = v[h]
        band = [jnp.zeros((HG_UNIT, HGRN_DH), F32) for _ in heads]
        for d in range(HG_BAND):
            win = pl.ds(HG_PAD - d, HG_UNIT)
            for h in heads:
                p = qq[h] * kpads_ref[h, win, :] * jnp.exp2(b[h] - bpads_ref[h, win, :])
                s_d = jnp.sum(p, axis=-1, keepdims=True)
                band[h] = band[h] + jnp.where(tmod >= d, s_d * vpads_ref[h, win, :], 0.0)

        for h in heads:
            b_last = b[h][HG_UNIT - 1:HG_UNIT, :]
            kdec = kk[h] * jnp.exp2(b_last - b[h])
            states_ref[h] = state[h] * jnp.exp2(b_last) + _dot_tn(vb[h], kdec.astype(BF16))
        for h in heads:
            oh = o[h] + intra[h] + band[h]
            oh = oh * lax.rsqrt(jnp.mean(oh * oh, axis=-1, keepdims=True) + RMS_EPS)
            gg = g_ref[0, rows, lanes[h]].astype(F32)
            o_ref[0, rows, lanes[h]] = (oh * gain_ref[:, lanes[h]] * _silu(gg)).astype(BF16)
        return carry

    lax.fori_loop(0, n_units, unit, 0)


def _hgrn(proj, lb, gain):
    bsz, s, _ = proj.shape
    ts = min(s, 512)
    tril, lv = _hgrn_consts()

    def col(off):
        return pl.BlockSpec((1, ts, D_HGRN), lambda b, i, off=off: (b, i, off))

    return pl.pallas_call(
        _hgrn_kernel,
        grid=(bsz, s // ts),
        in_specs=[col(0), col(1), col(2), col(3),
                  pl.BlockSpec((1, D_HGRN), lambda b, i: (0, 0)),
                  pl.BlockSpec((1, D_HGRN), lambda b, i: (0, 0)),
                  pl.BlockSpec((HG_UNIT, HG_UNIT), lambda b, i: (0, 0)),
                  pl.BlockSpec((HG_UNIT, HG_UNIT), lambda b, i: (0, 0))],
        out_specs=pl.BlockSpec((1, ts, D_HGRN), lambda b, i: (b, i, 0)),
        out_shape=jax.ShapeDtypeStruct((bsz, s, D_HGRN), BF16),
        scratch_shapes=[pltpu.VMEM((HGRN_HEADS, HGRN_DH, HGRN_DH), F32),
                        pltpu.VMEM((HGRN_HEADS, HG_UNIT + HG_PAD, HGRN_DH), F32),
                        pltpu.VMEM((HGRN_HEADS, HG_UNIT + HG_PAD, HGRN_DH), F32),
                        pltpu.VMEM((HGRN_HEADS, HG_UNIT + HG_PAD, HGRN_DH), F32)],
        compiler_params=_cparams(("parallel", "arbitrary")),
        name="hgrn2",
    )(proj, proj, proj, proj, lb, gain, tril, lv)


def _sb_kernel(q_ref, k_ref, v_ref, uo_ref, o_ref, kp_ref, vp_ref, c_ref, acc_ref, more_ref):
    tq = SB_BLOCK
    nq = q_ref.shape[2] // tq
    pad = (SB_WIN - 1) * tq
    kp_ref[0:pad, :] = jnp.zeros((pad, LANES), BF16)
    vp_ref[0:pad, :] = jnp.zeros((pad, LANES), BF16)
    kp_ref[pad:, :] = k_ref[0, 0]
    vp_ref[pad:, :] = v_ref[0, 0]

    first = lax.broadcasted_iota(jnp.int32, (tq, LANES), 1) < SB_DH
    rowi = lax.broadcasted_iota(jnp.int32, (2 * tq, tq), 0)
    coli = lax.broadcasted_iota(jnp.int32, (2 * tq, tq), 1)
    strict = coli < jnp.bitwise_and(rowi, tq - 1)

    def stacked_q(qi):
        rows = pl.ds(pl.multiple_of(qi * tq, tq), tq)
        q = q_ref[0, 0, rows, :] * (SB_DH ** -0.5)
        zero = jnp.zeros_like(q)
        return rows, jnp.concatenate([jnp.where(first, q, zero), jnp.where(first, zero, q)], axis=0)

    def window(q2, p_top, diag, crow, acc):
        (crow, acc), = windows([(q2, p_top, crow, acc)], diag)
        return crow, acc

    def windows(items, diag):
        order = list(reversed(range(SB_WIN)))
        zs, cts = [], []
        for q2, p_top, _, _ in items:
            start = pl.multiple_of((p_top - (SB_WIN - 1)) * tq, tq)
            zs.append(_dot_nt(q2, kp_ref[pl.ds(start, SB_WIN * tq), :]) * LOG2E)
        for z in zs:
            logn = -(jnp.maximum(z, 0.0) + jnp.log2(1.0 + jnp.exp2(-jnp.abs(z))))
            lhs = []
            for j in order:
                lj = logn[:, j * tq:(j + 1) * tq]
                if diag and j == SB_WIN - 1:
                    lj = jnp.where(strict, lj, 0.0)
                hi = lj.astype(BF16)
                lo = (lj - hi.astype(F32)).astype(BF16)
                lhs.append(jnp.concatenate([hi, lo], axis=1))
            cts.append(_dot(jnp.concatenate(lhs, axis=0), uo_ref[...]))
        outs = []
        for (q2, p_top, crow, acc), z, ct in zip(items, zs, cts):
            start = pl.multiple_of((p_top - (SB_WIN - 1)) * tq, tq)
            parts = [None] * SB_WIN
            for idx, j in enumerate(order):
                blk = ct[idx * 2 * tq:(idx + 1) * 2 * tq]
                a = jnp.exp2(z[:, j * tq:(j + 1) * tq] + (crow + blk))
                if diag and j == SB_WIN - 1:
                    a = jnp.where(strict, a, 0.0)
                parts[j] = a.astype(BF16)
                crow = crow + jnp.broadcast_to(blk[:, 0:1], blk.shape)
            acc = acc + _dot(jnp.concatenate(parts, axis=1), vp_ref[pl.ds(start, SB_WIN * tq), :])
            outs.append((crow, acc))
        return outs

    def emit(rows, acc):
        o_ref[0, 0, rows, :] = jnp.where(first, acc[0:tq], acc[tq:]).astype(BF16)

    def fast(g, carry):
        zero = jnp.zeros((2 * tq, LANES), F32)
        qis = [g * SB_GROUP + t for t in range(SB_GROUP)]
        prep = [stacked_q(qi) for qi in qis]
        res = windows([(q2, qi + SB_WIN - 1, zero, zero) for qi, (_, q2) in zip(qis, prep)], True)
        for qi, (rows, _), (crow, acc) in zip(qis, prep, res):
            emit(rows, acc)
            c_ref[qi] = crow
            acc_ref[qi] = acc
            unfinished = jnp.logical_and(qi >= SB_WIN, jnp.max(crow) > -SB_SKIP)
            more_ref[qi] = unfinished.astype(jnp.int32)
        return carry

    lax.fori_loop(0, nq // SB_GROUP, fast, 0)

    def finish(qi, carry):
        @pl.when(more_ref[qi] != 0)
        def _():
            rows, q2 = stacked_q(qi)

            def cond(c):
                p_top, mx, _, _ = c
                return jnp.logical_and(p_top >= SB_WIN - 1, mx > -SB_SKIP)

            def body(c):
                p_top, _, crow, acc = c
                crow, acc = window(q2, p_top, False, crow, acc)
                return p_top - SB_WIN, jnp.max(crow), crow, acc

            crow0 = c_ref[qi]
            _, _, _, acc = lax.while_loop(cond, body, (qi - 1, jnp.max(crow0), crow0, acc_ref[qi]))
            emit(rows, acc)
        return carry

    lax.fori_loop(SB_WIN, nq, finish, 0)


def _stick_breaking(qkv):
    _, bsz, s, _ = qkv.shape
    tq = SB_BLOCK
    nq = s // tq
    assert nq % SB_GROUP == 0
    n_pairs = D_SB // LANES
    jj = jnp.arange(tq)
    suffix = (jj[:, None] >= jj[None, :]).astype(BF16)
    uo = jnp.concatenate([suffix, suffix], axis=0)

    def seq(off):
        return pl.BlockSpec((1, 1, s, LANES), lambda b, p, off=off: (off * n_pairs + p, b, 0, 0))

    return pl.pallas_call(
        _sb_kernel,
        grid=(bsz, n_pairs),
        in_specs=[seq(0), seq(1), seq(2),
                  pl.BlockSpec((2 * tq, tq), lambda b, p: (0, 0))],
        out_specs=pl.BlockSpec((1, 1, s, LANES), lambda b, p: (p, b, 0, 0)),
        out_shape=jax.ShapeDtypeStruct((n_pairs, bsz, s, LANES), BF16),
        scratch_shapes=[pltpu.VMEM((s + (SB_WIN - 1) * tq, LANES), BF16),
                        pltpu.VMEM((s + (SB_WIN - 1) * tq, LANES), BF16),
                        pltpu.VMEM((nq, 2 * tq, LANES), F32),
                        pltpu.VMEM((nq, 2 * tq, LANES), F32),
                        pltpu.SMEM((nq,), jnp.int32)],
        compiler_params=_cparams(("parallel", "parallel")),
        name="stick_breaking",
    )(qkv, qkv, qkv, uo)


def _residual_ln(x, y, gate, gain, bias, alpha):
    r = alpha * x + gate * y
    mu = jnp.mean(r, axis=-1, keepdims=True)
    rc = r - mu
    var = jnp.mean(rc * rc, axis=-1, keepdims=True)
    return rc * lax.rsqrt(var + LN_EPS) * gain + bias


def _sb_lanes(os_ref):
    return jnp.concatenate([os_ref[p, 0] for p in range(os_ref.shape[0])], axis=-1)


def _emit(xn, outs, mod_next_ref, packed=False):
    outs[0][0] = xn
    if mod_next_ref is not None:
        mn = mod_next_ref[0]
        hn = xn * (1.0 + mn[1:2]) + mn[0:1]
        if packed:
            _store_planes(outs[1], _pack_rows(hn))
        else:
            outs[1][0] = hn.astype(BF16)


def _outproj_kernel(alpha, has_next, packed, oh_ref, os_ref, x_ref, w_ref, mod_ref, ln_ref, *rest):
    mod_next_ref = rest[0] if has_next else None
    outs = rest[1:] if has_next else rest
    y = _dot(oh_ref[0], w_ref[0:D_HGRN, :]) + _dot(_sb_lanes(os_ref), w_ref[D_HGRN:, :])
    ln = ln_ref[...]
    xn = _residual_ln(x_ref[0], y, mod_ref[0][2:3], ln[0:1], ln[1:2], alpha)
    _emit(xn, outs, mod_next_ref, packed)


def _row_specs(bsz, s, d, tm, mod_next, packed=False):
    x_spec = pl.BlockSpec((1, tm, d), lambda b, i: (b, i, 0))
    mod_spec = pl.BlockSpec((1, 3, d), lambda b, i: (b, 0, 0))
    ln_spec = pl.BlockSpec((2, d), lambda b, i: (0, 0))
    out_specs = [pl.BlockSpec((1, tm, d), lambda b, i: (b, i, 0))]
    out_shape = [jax.ShapeDtypeStruct((bsz, s, d), F32)]
    if mod_next is not None and packed:
        out_specs.append(pl.BlockSpec((2, 1, tm, d // 4), lambda b, i: (0, b, i, 0)))
        out_shape.append(jax.ShapeDtypeStruct((2, bsz, s, d // 4), U32))
    elif mod_next is not None:
        out_specs.append(pl.BlockSpec((1, tm, d), lambda b, i: (b, i, 0)))
        out_shape.append(jax.ShapeDtypeStruct((bsz, s, d), BF16))
    return x_spec, mod_spec, ln_spec, out_specs, out_shape


def _out_proj_ln(o_h, o_s, x, w_out, mod, ln, mod_next, alpha, packed=True):
    bsz, s, d = x.shape
    tm = min(s, 512)
    x_spec, mod_spec, ln_spec, out_specs, out_shape = _row_specs(bsz, s, d, tm, mod_next, packed)
    in_specs = [pl.BlockSpec((1, tm, D_HGRN), lambda b, i: (b, i, 0)),
                pl.BlockSpec((D_SB // LANES, 1, tm, LANES), lambda b, i: (0, b, i, 0)),
                x_spec,
                pl.BlockSpec((d, d), lambda b, i: (0, 0), pipeline_mode=pl.Buffered(1)),
                mod_spec, ln_spec]
    args = [o_h, o_s, x, w_out, mod, ln]
    if mod_next is not None:
        in_specs.append(mod_spec)
        args.append(mod_next)
    return pl.pallas_call(
        functools.partial(_outproj_kernel, alpha, mod_next is not None, packed),
        grid=(bsz, s // tm),
        in_specs=in_specs, out_specs=out_specs, out_shape=out_shape,
        compiler_params=_cparams(("parallel", "parallel")),
        name="out_proj_ln",
    )(*args)


def _mix_dense_kernel(alpha, has_next, n_chunks, oh_ref, os_ref, x_ref, wo_ref, wg_ref, wu_ref, wd_ref,
                      mod0_ref, mod1_ref, ln0_ref, ln1_ref, *rest):
    mod_next_ref = rest[0] if has_next else None
    outs = rest[1:] if has_next else rest
    y = _dot(oh_ref[0], wo_ref[0:D_HGRN, :]) + _dot(_sb_lanes(os_ref), wo_ref[D_HGRN:, :])
    ln0 = ln0_ref[...]
    x1 = _residual_ln(x_ref[0], y, mod0_ref[0][2:3], ln0[0:1], ln0[1:2], alpha)
    m1 = mod1_ref[0]
    h = (x1 * (1.0 + m1[1:2]) + m1[0:1]).astype(BF16)
    ff = wg_ref.shape[1]
    tf = ff // n_chunks
    y = None
    for j in range(n_chunks):
        cols = slice(j * tf, (j + 1) * tf)
        g = _dot(h, wg_ref[:, cols])
        u = _dot(h, wu_ref[:, cols])
        part = _dot((_silu(g) * u).astype(BF16), wd_ref[cols, :])
        y = part if y is None else y + part
    ln1 = ln1_ref[...]
    x2 = _residual_ln(x1, y, m1[2:3], ln1[0:1], ln1[1:2], alpha)
    _emit(x2, outs, mod_next_ref)


def _mix_dense_ln(o_h, o_s, x, w_out, wg, wu, wd, mod0, mod1, ln0, ln1, mod_next, alpha):
    bsz, s, d = x.shape
    ff = wg.shape[1]
    tm = min(s, 512)
    x_spec, mod_spec, ln_spec, out_specs, out_shape = _row_specs(bsz, s, d, tm, mod_next)
    once = pl.Buffered(1)
    in_specs = [pl.BlockSpec((1, tm, D_HGRN), lambda b, i: (b, i, 0)),
                pl.BlockSpec((D_SB // LANES, 1, tm, LANES), lambda b, i: (0, b, i, 0)),
                x_spec,
                pl.BlockSpec((d, d), lambda b, i: (0, 0), pipeline_mode=once),
                pl.BlockSpec((d, ff), lambda b, i: (0, 0), pipeline_mode=once),
                pl.BlockSpec((d, ff), lambda b, i: (0, 0), pipeline_mode=once),
                pl.BlockSpec((ff, d), lambda b, i: (0, 0), pipeline_mode=once),
                mod_spec, mod_spec, ln_spec, ln_spec]
    args = [o_h, o_s, x, w_out, wg, wu, wd, mod0, mod1, ln0, ln1]
    if mod_next is not None:
        in_specs.append(mod_spec)
        args.append(mod_next)
    n_chunks = 2 if (ff // 2) % LANES == 0 else 1
    return pl.pallas_call(
        functools.partial(_mix_dense_kernel, alpha, mod_next is not None, n_chunks),
        grid=(bsz, s // tm),
        in_specs=in_specs, out_specs=out_specs, out_shape=out_shape,
        compiler_params=_cparams(("parallel", "parallel")),
        name="out_proj_dense_ffn_ln",
    )(*args)


def _router_kernel(h_ref, w_ref, tri_ref, route_ref, count_ref, run_ref):
    @pl.when(pl.program_id(0) == 0)
    def _():
        run_ref[...] = jnp.zeros_like(run_ref)

    tm = h_ref.shape[1]
    lane = lax.broadcasted_iota(jnp.int32, (tm, LANES), 1)
    logits = _dot(_load_planes(h_ref[0], h_ref[1]).astype(BF16), w_ref[...])
    neg = jnp.float32(-jnp.inf)
    logits = jnp.where(lane < N_EXPERTS, logits, neg)
    m1 = jnp.max(logits, axis=-1, keepdims=True)
    i1 = jnp.min(jnp.where(logits == m1, lane, LANES), axis=-1, keepdims=True)
    rest = jnp.where(lane == i1, neg, logits)
    m2 = jnp.max(rest, axis=-1, keepdims=True)
    i2 = jnp.min(jnp.where(rest == m2, lane, LANES), axis=-1, keepdims=True)
    e21 = jnp.exp(m2 - m1)
    w1 = 1.0 / (1.0 + e21)
    w2 = e21 / (1.0 + e21)

    hot1 = lane == i1
    hot2 = lane == i2
    both = jnp.where(jnp.logical_or(hot1, hot2), 1.0, 0.0)
    before = _dot(tri_ref[...], both.astype(BF16)) + run_ref[...]
    r1 = jnp.sum(jnp.where(hot1, before, 0.0), axis=-1, keepdims=True)
    r2 = jnp.sum(jnp.where(hot2, before, 0.0), axis=-1, keepdims=True)
    run_ref[...] = run_ref[...] + jnp.sum(both, axis=0, keepdims=True)
    count_ref[...] = run_ref[...]

    cols = [i1.astype(F32), i2.astype(F32), r1, r2, w1, w2]
    route = jnp.zeros((tm, LANES), F32)
    for c, val in enumerate(cols):
        route = jnp.where(lane == c, val, route)
    route_ref[...] = route[:, 0:8]


def _router(hp, w_router):
    _, n, pw = hp.shape
    d = 4 * pw
    tm = min(n, 512)
    w_pad = jnp.zeros((d, LANES), BF16).at[:, :N_EXPERTS].set(w_router.astype(BF16))
    jj = jnp.arange(tm)
    tri = (jj[None, :] < jj[:, None]).astype(BF16)
    return pl.pallas_call(
        _router_kernel,
        grid=(n // tm,),
        in_specs=[pl.BlockSpec((2, tm, pw), lambda i: (0, i, 0)),
                  pl.BlockSpec((d, LANES), lambda i: (0, 0)),
                  pl.BlockSpec((tm, tm), lambda i: (0, 0))],
        out_specs=[pl.BlockSpec((tm, 8), lambda i: (i, 0)),
                   pl.BlockSpec((1, LANES), lambda i: (0, 0))],
        out_shape=[jax.ShapeDtypeStruct((n, 8), F32),
                   jax.ShapeDtypeStruct((1, LANES), F32)],
        scratch_shapes=[pltpu.VMEM((1, LANES), F32)],
        compiler_params=_cparams(("arbitrary",)),
        name="moe_router",
    )(hp, w_pad, tri)


def _gmm_kernel(te_ref, nt_ref, x_ref, wg_ref, wu_ref, wd_ref, o_ref, acc_ref, xb_ref):
    i = pl.program_id(0)
    j = pl.program_id(1)
    nj = pl.num_programs(1)
    live = i < nt_ref[0]

    @pl.when(jnp.logical_and(live, j == 0))
    def _():
        acc_ref[...] = jnp.zeros_like(acc_ref)
        xb_ref[...] = _load_planes(x_ref[0], x_ref[1]).astype(BF16)

    @pl.when(live)
    def _():
        x = xb_ref[...]
        g = _dot(x, wg_ref[0, 0].astype(BF16))
        u = _dot(x, wu_ref[0, 0].astype(BF16))
        acc_ref[...] += _dot((_silu(g) * u).astype(BF16), wd_ref[0, 0].astype(BF16))

    @pl.when(jnp.logical_and(live, j == nj - 1))
    def _():
        _store_planes(o_ref, _pack_rows(acc_ref[...]))

    @pl.when(jnp.logical_and(jnp.logical_not(live), j == nj - 1))
    def _():
        o_ref[...] = jnp.zeros_like(o_ref)


def _grouped_swiglu(xs, tile_expert, n_tiles_used, wg, wu, wd, layer):
    _, n_slots, pw = xs.shape
    d = 4 * pw
    ff = wg.shape[3]
    tm, tf = MOE_TILE, MOE_FF_TILE
    n_tiles = n_slots // tm

    def row_map(i, j, te, nt):
        return (0, jnp.minimum(i, nt[0] - 1), 0)

    def ff_block(i, j, te, nt):
        return jnp.where(i < nt[0], j, ff // tf - 1)

    grid_spec = pltpu.PrefetchScalarGridSpec(
        num_scalar_prefetch=2,
        grid=(n_tiles, ff // tf),
        in_specs=[pl.BlockSpec((2, tm, pw), row_map),
                  pl.BlockSpec((1, 1, d, tf), lambda i, j, te, nt: (layer, te[i], 0, ff_block(i, j, te, nt))),
                  pl.BlockSpec((1, 1, d, tf), lambda i, j, te, nt: (layer, te[i], 0, ff_block(i, j, te, nt))),
                  pl.BlockSpec((1, 1, tf, d), lambda i, j, te, nt: (layer, te[i], ff_block(i, j, te, nt), 0))],
        out_specs=pl.BlockSpec((2, tm, pw), lambda i, j, te, nt: (0, i, 0)),
        scratch_shapes=[pltpu.VMEM((tm, d), F32), pltpu.VMEM((tm, d), BF16)],
    )
    return pl.pallas_call(
        _gmm_kernel,
        grid_spec=grid_spec,
        out_shape=jax.ShapeDtypeStruct((2, n_slots, pw), U32),
        compiler_params=_cparams(("arbitrary", "arbitrary")),
        name="moe_grouped_swiglu",
    )(tile_expert, n_tiles_used, xs, wg, wu, wd)


def _combine_kernel(alpha, has_next, ya_ref, yb_ref, r_ref, x_ref, mod_ref, ln_ref, *rest):
    mod_next_ref = rest[0] if has_next else None
    outs = rest[1:] if has_next else rest
    r = r_ref[0]
    y = (r[:, 4:5] * _load_planes(ya_ref[0, 0, 0], ya_ref[0, 1, 0])
         + r[:, 5:6] * _load_planes(yb_ref[0, 0, 0], yb_ref[0, 1, 0]))
    ln = ln_ref[...]
    xn = _residual_ln(x_ref[0], y, mod_ref[0][2:3], ln[0:1], ln[1:2], alpha)
    _emit(xn, outs, mod_next_ref)


def _combine_ln(yab, route, x, mod, ln, mod_next, alpha):
    bsz, s, d = x.shape
    tm = min(s, 512)
    x_spec, mod_spec, ln_spec, out_specs, out_shape = _row_specs(bsz, s, d, tm, mod_next)
    yab = yab.reshape(2, 2, bsz, s, d // 4)
    in_specs = [pl.BlockSpec((1, 2, 1, tm, d // 4), lambda b, i: (0, 0, b, i, 0)),
                pl.BlockSpec((1, 2, 1, tm, d // 4), lambda b, i: (1, 0, b, i, 0)),
                pl.BlockSpec((1, tm, 8), lambda b, i: (b, i, 0)),
                x_spec, mod_spec, ln_spec]
    args = [yab, yab, route.reshape(bsz, s, 8), x, mod, ln]
    if mod_next is not None:
        in_specs.append(mod_spec)
        args.append(mod_next)
    return pl.pallas_call(
        functools.partial(_combine_kernel, alpha, mod_next is not None),
        grid=(bsz, s // tm),
        in_specs=in_specs, out_specs=out_specs, out_shape=out_shape,
        compiler_params=_cparams(("parallel", "parallel")),
        name="moe_combine_ln",
    )(*args)


def _sc_mesh():
    return plsc.VectorSubcoreMesh(core_axis_name="c", subcore_axis_name="s")


def _dispatch_rows(rows, idx, n_out):
    m, w = rows.shape

    @pl.kernel(out_type=jax.ShapeDtypeStruct((n_out, w), rows.dtype), mesh=_sc_mesh(), scratch_types=[])
    def scatter(rows_hbm, idx_hbm, out_hbm):
        def body(rows_vmem, idx0_vmem, idx1_vmem):
            pltpu.sync_copy(rows_vmem, out_hbm.at[idx0_vmem.at[0]])
            pltpu.sync_copy(rows_vmem, out_hbm.at[idx1_vmem.at[0]])

        pltpu.emit_pipeline(
            body, grid=(m // SC_WINDOW,),
            in_specs=[pl.BlockSpec((SC_WINDOW, w), lambda i: (i, 0)),
                      pl.BlockSpec((1, SC_WINDOW), lambda i: (0, i)),
                      pl.BlockSpec((1, SC_WINDOW), lambda i: (1, i))],
            out_specs=[],
            core_axis_name=("c", "s"),
            dimension_semantics=(pltpu.PARALLEL,),
        )(rows_hbm, idx_hbm, idx_hbm)

    return scatter(rows, idx)


def _collect_rows(rows, idx):
    m = idx.shape[1]
    w = rows.shape[1]

    @pl.kernel(out_type=jax.ShapeDtypeStruct((m, w), rows.dtype), mesh=_sc_mesh(), scratch_types=[])
    def gather(rows_hbm, idx_hbm, out_hbm):
        def body(idx_vmem, out_vmem):
            pltpu.sync_copy(rows_hbm.at[idx_vmem.at[0]], out_vmem)

        pltpu.emit_pipeline(
            body, grid=(m // SC_WINDOW,),
            in_specs=[pl.BlockSpec((1, SC_WINDOW), lambda i: (0, i))],
            out_specs=[pl.BlockSpec((SC_WINDOW, w), lambda i: (i, 0))],
            core_axis_name=("c", "s"),
            dimension_semantics=(pltpu.PARALLEL,),
        )(idx_hbm, out_hbm)

    return gather(rows, idx)


def _moe_ffn_ln(h, x, w_router, wg, wu, wd, layer, mod, ln, mod_next, alpha):
    bsz, s, d = x.shape
    n = bsz * s
    hp = h.reshape(2, n, d // 4)
    route, counts = _router(hp, w_router)

    tm = MOE_TILE
    cnt = counts[0, :N_EXPERTS].astype(jnp.int32)
    padded = ((cnt + tm - 1) // tm) * tm
    ends = jnp.cumsum(padded)
    offs = ends - padded
    e = route[:, 0:2].astype(jnp.int32)
    rank = route[:, 2:4].astype(jnp.int32)
    pos = (jnp.take(offs, e) + rank).T
    n_tiles = (2 * n) // tm + N_EXPERTS
    n_slots = n_tiles * tm
    tile_start = jnp.arange(n_tiles, dtype=jnp.int32) * tm
    tile_expert = jnp.minimum(jnp.sum(tile_start[:, None] >= ends[None, :], axis=1),
                              N_EXPERTS - 1).astype(jnp.int32)
    n_used = (ends[-1] // tm).astype(jnp.int32).reshape(1)

    idx = jnp.concatenate([pos, pos + n_slots], axis=1)
    xs = _dispatch_rows(hp.reshape(2 * n, d // 4), idx, 2 * n_slots).reshape(2, n_slots, d // 4)
    ys = _grouped_swiglu(xs, tile_expert, n_used, wg, wu, wd, layer)
    yab = _collect_rows(ys.reshape(2 * n_slots, d // 4), idx.reshape(1, 4 * n)).reshape(2, 2, n, d // 4)
    return _combine_ln(yab, route, x, mod, ln, mod_next, alpha)


def kernel(x, c, w_ada, b_ada, w_in, w_out, hgrn_lb_logits, hgrn_norm_gain,
           w_dense_gate, w_dense_up, w_dense_down, w_router, w_moe_gate, w_moe_up, w_moe_down,
           ln_gain, ln_bias):
    depth = w_in.shape[0]
    bsz, s, d = x.shape
    alpha = (2 * depth) ** 0.25

    mods = _ada_all(c, w_ada, b_ada).reshape(depth, 2, bsz, 3, d)
    lb_all = _lower_bounds(hgrn_lb_logits)
    ln_all = jnp.stack([ln_gain, ln_bias], axis=2)

    h = _modulate(x, mods[0, 0])
    for layer in range(depth):
        proj_h, proj_s = _in_proj(h.reshape(bsz * s, d), w_in[layer].astype(BF16))
        o_h = _hgrn(proj_h.reshape(bsz, s, 4 * D_HGRN), lb_all[layer:layer + 1],
                    hgrn_norm_gain[layer:layer + 1])
        o_s = _stick_breaking(proj_s.reshape(-1, bsz, s, LANES))
        mod_next = mods[layer + 1, 0] if layer + 1 < depth else None
        j = layer // 2
        if layer % 2 == 0:
            res = _mix_dense_ln(o_h, o_s, x, w_out[layer].astype(BF16), w_dense_gate[j].astype(BF16),
                                w_dense_up[j].astype(BF16), w_dense_down[j].astype(BF16),
                                mods[layer, 0], mods[layer, 1], ln_all[layer, 0], ln_all[layer, 1],
                                mod_next, alpha)
        else:
            x, h = _out_proj_ln(o_h, o_s, x, w_out[layer].astype(BF16), mods[layer, 0],
                                ln_all[layer, 0], mods[layer, 1], alpha)
            res = _moe_ffn_ln(h, x, w_router[j], w_moe_gate, w_moe_up, w_moe_down, j,
                              mods[layer, 1], ln_all[layer, 1], mod_next, alpha)
        if mod_next is not None:
            x, h = res
        else:
            (x,) = res
    return x
```

```python
import functools
import math

import jax
import jax.numpy as jnp
from jax import lax
from jax.experimental import pallas as pl
from jax.experimental.pallas import tpu as pltpu
from jax.experimental.pallas import tpu_sc as plsc

D_MODEL = 1024
D_HGRN = 512
HGRN_HEADS = 4
HGRN_DH = 128
D_SB = 512
SB_HEADS = 8
SB_DH = 64
D_IN_PROJ = 4 * D_HGRN + 3 * D_SB
N_EXPERTS = 8
LN_EPS = 1e-5
RMS_EPS = 1e-6

LOG2E = math.log2(math.e)
LANES = 128
VMEM_LIMIT = 52 * 1024 * 1024

HG_UNIT = 128
HG_BAND = 4
HG_PAD = 8
HG_LEVELS = (64, 32, 16, 8, 4)
SB_BLOCK = 128
SB_SKIP = 110.0 * LOG2E
SB_GROUP = 4
SB_WIN = 3
MOE_TILE = 1024
MOE_FF_TILE = 512
DENSE_SPLIT = 2
SC_WINDOW = 128

F32 = jnp.float32
BF16 = jnp.bfloat16
U32 = jnp.uint32


def _cparams(sem):
    return pltpu.CompilerParams(dimension_semantics=sem, vmem_limit_bytes=VMEM_LIMIT)


def _dot(a, b):
    return jnp.dot(a, b, preferred_element_type=F32)


def _dot_nt(a, b):
    return lax.dot_general(a, b, (((1,), (1,)), ((), ())), preferred_element_type=F32)


def _dot_tn(a, b):
    return lax.dot_general(a, b, (((0,), (0,)), ((), ())), preferred_element_type=F32)


def _pack_rows(v):
    w = v.shape[-1] // 2
    lo = lax.bitcast_convert_type(v[:, :w].astype(BF16).astype(F32), U32)
    hi = lax.bitcast_convert_type(v[:, w:].astype(BF16).astype(F32), U32)
    return (lo >> 16) | hi


def _unpack_rows(p):
    lo = lax.bitcast_convert_type(p << 16, F32)
    hi = lax.bitcast_convert_type(p & jnp.uint32(0xFFFF0000), F32)
    return jnp.concatenate([lo, hi], axis=-1)


def _store_planes(ref, p):
    w = p.shape[-1] // 2
    ref[0] = p[:, :w].reshape(ref.shape[1:])
    ref[1] = p[:, w:].reshape(ref.shape[1:])


def _load_planes(p0, p1):
    return _unpack_rows(jnp.concatenate([p0, p1], axis=-1))


def _sigmoid(x):
    return 1.0 / (1.0 + jnp.exp(-x))


def _silu(x):
    return x * _sigmoid(x)


def _ada_kernel(c_ref, w_ref, b_ref, o_ref):
    sc = _silu(c_ref[...])
    o_ref[0] = jnp.dot(sc, w_ref[0], preferred_element_type=F32,
                       precision=lax.Precision.HIGHEST) + b_ref[0]


def _ada_all(c, w_ada, b_ada):
    n_sub = w_ada.shape[0] * w_ada.shape[1]
    bsz, d = c.shape
    w = w_ada.reshape(n_sub, d, 3 * d)
    b = b_ada.reshape(n_sub, 1, 3 * d)
    tn = 1024
    return pl.pallas_call(
        _ada_kernel,
        grid=(n_sub, 3 * d // tn),
        in_specs=[pl.BlockSpec((bsz, d), lambda s, j: (0, 0)),
                  pl.BlockSpec((1, d, tn), lambda s, j: (s, 0, j)),
                  pl.BlockSpec((1, 1, tn), lambda s, j: (s, 0, j))],
        out_specs=pl.BlockSpec((1, bsz, tn), lambda s, j: (s, 0, j)),
        out_shape=jax.ShapeDtypeStruct((n_sub, bsz, 3 * d), F32),
        compiler_params=_cparams(("parallel", "parallel")),
        name="ada_modulation",
    )(c, w, b)


def _lb_kernel(l_ref, o_ref):
    x = l_ref[...]
    m = jnp.max(x, axis=0, keepdims=True)
    e = jnp.exp(x - m)
    p = e / jnp.sum(e, axis=0, keepdims=True)
    depth = x.shape[0]
    run = jnp.zeros_like(p[0:1])
    rows = []
    for i in range(depth):
        run = run + p[i:i + 1]
        rows.append(run - p[0:1])
    o_ref[...] = jnp.concatenate(rows, axis=0)


def _lower_bounds(lb_logits):
    return pl.pallas_call(
        _lb_kernel,
        out_shape=jax.ShapeDtypeStruct(lb_logits.shape, F32),
        name="hgrn_lower_bounds",
    )(lb_logits)


def _modulate_kernel(x_ref, m_ref, h_ref):
    m = m_ref[0]
    h_ref[0] = (x_ref[0] * (1.0 + m[1:2]) + m[0:1]).astype(BF16)


def _modulate(x, mod):
    bsz, s, d = x.shape
    ts = min(s, 1024)
    return pl.pallas_call(
        _modulate_kernel,
        grid=(bsz, s // ts),
        in_specs=[pl.BlockSpec((1, ts, d), lambda b, i: (b, i, 0)),
                  pl.BlockSpec((1, 3, d), lambda b, i: (b, 0, 0))],
        out_specs=pl.BlockSpec((1, ts, d), lambda b, i: (b, i, 0)),
        out_shape=jax.ShapeDtypeStruct((bsz, s, d), BF16),
        compiler_params=_cparams(("parallel", "parallel")),
        name="modulate_first",
    )(x, mod)


def _inproj_kernel(h_ref, w_ref, oh_ref, os_ref):
    h = h_ref[...]
    tn = 512
    nh = oh_ref.shape[1] // tn
    for j in range(nh):
        oh_ref[:, j * tn:(j + 1) * tn] = _dot(h, w_ref[:, j * tn:(j + 1) * tn]).astype(BF16)
    per = tn // LANES
    for j in range(os_ref.shape[0] // per):
        blk = _dot(h, w_ref[:, (nh + j) * tn:(nh + j + 1) * tn]).astype(BF16)
        for c in range(per):
            os_ref[j * per + c] = blk[:, c * LANES:(c + 1) * LANES]


def _in_proj(h2d, w_in):
    n, d = h2d.shape
    dout = w_in.shape[1]
    dh = 4 * D_HGRN
    n_slabs = (dout - dh) // LANES
    tm = min(n, 512)
    return pl.pallas_call(
        _inproj_kernel,
        grid=(n // tm,),
        in_specs=[pl.BlockSpec((tm, d), lambda i: (i, 0)),
                  pl.BlockSpec((d, dout), lambda i: (0, 0), pipeline_mode=pl.Buffered(1))],
        out_specs=[pl.BlockSpec((tm, dh), lambda i: (i, 0)),
                   pl.BlockSpec((n_slabs, tm, LANES), lambda i: (0, i, 0))],
        out_shape=[jax.ShapeDtypeStruct((n, dh), BF16),
                   jax.ShapeDtypeStruct((n_slabs, n, LANES), BF16)],
        compiler_params=_cparams(("parallel",)),
        name="in_proj",
    )(h2d, w_in)


def _hgrn_consts():
    t = jnp.arange(HG_UNIT)[:, None]
    s = jnp.arange(HG_UNIT)[None, :]
    tril = (s <= t).astype(BF16)
    lv = jnp.full((HG_UNIT, HG_UNIT), -1, jnp.int32)
    for idx, m in enumerate(HG_LEVELS):
        blk = 2 * m
        hit = (t // blk == s // blk) & (t % blk >= m) & (s % blk < m)
        lv = jnp.where(hit, idx, lv)
    return tril, lv


def _hgrn_kernel(q_ref, f_ref, i_ref, g_ref, lb_ref, gain_ref, tril_ref, lv_ref, o_ref,
                 states_ref, kpads_ref, bpads_ref, vpads_ref):
    @pl.when(pl.program_id(1) == 0)
    def _():
        states_ref[...] = jnp.zeros_like(states_ref)

    heads = range(HGRN_HEADS)
    zpad = jnp.zeros((HGRN_HEADS, HG_PAD, HGRN_DH), F32)
    kpads_ref[:, 0:HG_PAD, :] = zpad
    bpads_ref[:, 0:HG_PAD, :] = zpad
    vpads_ref[:, 0:HG_PAD, :] = zpad

    n_units = q_ref.shape[1] // HG_UNIT
    rowi = lax.broadcasted_iota(jnp.int32, (HG_UNIT, HGRN_DH), 0)
    tmod = jnp.bitwise_and(rowi, HG_BAND - 1)

    def unit(u, carry):
        rows = pl.ds(pl.multiple_of(u * HG_UNIT, HG_UNIT), HG_UNIT)
        lanes = [slice(h * HGRN_DH, (h + 1) * HGRN_DH) for h in heads]

        kk, qq, v, vb, b3 = [], [], [], [], []
        for h in heads:
            lb = lb_ref[:, lanes[h]]
            f = lb + (1.0 - lb) * _sigmoid(f_ref[0, rows, lanes[h]].astype(F32))
            logf = jnp.log2(f)
            kk.append(1.0 - f)
            qq.append(_silu(q_ref[0, rows, lanes[h]].astype(F32)))
            v.append(i_ref[0, rows, lanes[h]].astype(F32))
            vb.append(i_ref[0, rows, lanes[h]])
            p0 = logf.astype(BF16)
            r1 = logf - p0.astype(F32)
            p1 = r1.astype(BF16)
            p2 = (r1 - p1.astype(F32)).astype(BF16)
            b3.append(_dot(tril_ref[...], jnp.concatenate([p0, p1, p2], axis=1)))
        b = [t[:, 0:HGRN_DH] + t[:, HGRN_DH:2 * HGRN_DH] + t[:, 2 * HGRN_DH:] for t in b3]

        state = [states_ref[h] for h in heads]
        o = [_dot_nt((qq[h] * jnp.exp2(b[h])).astype(BF16), state[h].astype(BF16)) for h in heads]

        lv = lv_ref[...]
        scores = [jnp.zeros((HG_UNIT, HG_UNIT), F32) for _ in heads]
        for idx, m in enumerate(HG_LEVELS):
            blk = 2 * m
            nb = HG_UNIT // blk
            upper = jnp.bitwise_and(rowi, m) != 0
            xl = []
            for h in heads:
                ref = b[h].reshape(nb, blk, HGRN_DH)[:, m - 1:m, :]
                ref = jnp.broadcast_to(ref, (nb, blk, HGRN_DH)).reshape(HG_UNIT, HGRN_DH)
                xl.append((jnp.where(upper, qq[h], kk[h]) * jnp.exp2(-jnp.abs(b[h] - ref))).astype(BF16))
            s_l = [_dot_nt(x, x) for x in xl]
            scores = [jnp.where(lv == idx, s_l[h], scores[h]) for h in heads]
        intra = [_dot(scores[h].astype(BF16), vb[h]) for h in heads]

        for h in heads:
            kpads_ref[h, HG_PAD:, :] = kk[h]
            bpads_ref[h, HG_PAD:, :] = b[h]
            vpads_ref[h, HG_PAD:, :] = v[h]
        band = [jnp.zeros((HG_UNIT, HGRN_DH), F32) for _ in heads]
        for d in range(HG_BAND):
            win = pl.ds(HG_PAD - d, HG_UNIT)
            for h in heads:
                p = qq[h] * kpads_ref[h, win, :] * jnp.exp2(b[h] - bpads_ref[h, win, :])
                s_d = jnp.sum(p, axis=-1, keepdims=True)
                band[h] = band[h] + jnp.where(tmod >= d, s_d * vpads_ref[h, win, :], 0.0)

        for h in heads:
            b_last = b[h][HG_UNIT - 1:HG_UNIT, :]
            kdec = kk[h] * jnp.exp2(b_last - b[h])
            states_ref[h] = state[h] * jnp.exp2(b_last) + _dot_tn(vb[h], kdec.astype(BF16))
        for h in heads:
            oh = o[h] + intra[h] + band[h]
            oh = oh * lax.rsqrt(jnp.mean(oh * oh, axis=-1, keepdims=True) + RMS_EPS)
            gg = g_ref[0, rows, lanes[h]].astype(F32)
            o_ref[0, rows, lanes[h]] = (oh * gain_ref[:, lanes[h]] * _silu(gg)).astype(BF16)
        return carry

    lax.fori_loop(0, n_units, unit, 0)


def _hgrn(proj, lb, gain):
    bsz, s, _ = proj.shape
    ts = min(s, 512)
    tril, lv = _hgrn_consts()

    def col(off):
        return pl.BlockSpec((1, ts, D_HGRN), lambda b, i, off=off: (b, i, off))

    return pl.pallas_call(
        _hgrn_kernel,
        grid=(bsz, s // ts),
        in_specs=[col(0), col(1), col(2), col(3),
                  pl.BlockSpec((1, D_HGRN), lambda b, i: (0, 0)),
                  pl.BlockSpec((1, D_HGRN), lambda b, i: (0, 0)),
                  pl.BlockSpec((HG_UNIT, HG_UNIT), lambda b, i: (0, 0)),
                  pl.BlockSpec((HG_UNIT, HG_UNIT), lambda b, i: (0, 0))],
        out_specs=pl.BlockSpec((1, ts, D_HGRN), lambda b, i: (b, i, 0)),
        out_shape=jax.ShapeDtypeStruct((bsz, s, D_HGRN), BF16),
        scratch_shapes=[pltpu.VMEM((HGRN_HEADS, HGRN_DH, HGRN_DH), F32),
                        pltpu.VMEM((HGRN_HEADS, HG_UNIT + HG_PAD, HGRN_DH), F32),
                        pltpu.VMEM((HGRN_HEADS, HG_UNIT + HG_PAD, HGRN_DH), F32),
                        pltpu.VMEM((HGRN_HEADS, HG_UNIT + HG_PAD, HGRN_DH), F32)],
        compiler_params=_cparams(("parallel", "arbitrary")),
        name="hgrn2",
    )(proj, proj, proj, proj, lb, gain, tril, lv)


def _sb_kernel(q_ref, k_ref, v_ref, uo_ref, o_ref, kp_ref, vp_ref, c_ref, acc_ref, more_ref):
    tq = SB_BLOCK
    nq = q_ref.shape[2] // tq
    pad = (SB_WIN - 1) * tq
    kp_ref[0:pad, :] = jnp.zeros((pad, LANES), BF16)
    vp_ref[0:pad, :] = jnp.zeros((pad, LANES), BF16)
    kp_ref[pad:, :] = k_ref[0, 0]
    vp_ref[pad:, :] = v_ref[0, 0]

    first = lax.broadcasted_iota(jnp.int32, (tq, LANES), 1) < SB_DH
    rowi = lax.broadcasted_iota(jnp.int32, (2 * tq, tq), 0)
    coli = lax.broadcasted_iota(jnp.int32, (2 * tq, tq), 1)
    strict = coli < jnp.bitwise_and(rowi, tq - 1)

    def stacked_q(qi):
        rows = pl.ds(pl.multiple_of(qi * tq, tq), tq)
        q = q_ref[0, 0, rows, :] * (SB_DH ** -0.5)
        zero = jnp.zeros_like(q)
        return rows, jnp.concatenate([jnp.where(first, q, zero), jnp.where(first, zero, q)], axis=0)

    def windows(items, diag, win):
        order = list(reversed(range(win)))
        zs, cts = [], []
        for q2, p_top, _, _ in items:
            start = pl.multiple_of((p_top - (win - 1)) * tq, tq)
            zs.append(_dot_nt(q2, kp_ref[pl.ds(start, win * tq), :]) * LOG2E)
        for z in zs:
            logn = -(jnp.maximum(z, 0.0) + jnp.log2(1.0 + jnp.exp2(-jnp.abs(z))))
            lhs = []
            for j in order:
                lj = logn[:, j * tq:(j + 1) * tq]
                if diag and j == win - 1:
                    lj = jnp.where(strict, lj, 0.0)
                hi = lj.astype(BF16)
                lo = (lj - hi.astype(F32)).astype(BF16)
                lhs.append(jnp.concatenate([hi, lo], axis=1))
            cts.append(_dot(jnp.concatenate(lhs, axis=0), uo_ref[...]))
        outs = []
        for (q2, p_top, crow, acc), z, ct in zip(items, zs, cts):
            start = pl.multiple_of((p_top - (win - 1)) * tq, tq)
            parts = [None] * win
            for idx, j in enumerate(order):
                blk = ct[idx * 2 * tq:(idx + 1) * 2 * tq]
                a = jnp.exp2(z[:, j * tq:(j + 1) * tq] + (crow + blk))
                if diag and j == win - 1:
                    a = jnp.where(strict, a, 0.0)
                parts[j] = a.astype(BF16)
                crow = crow + jnp.broadcast_to(blk[:, 0:1], blk.shape)
            acc = acc + _dot(jnp.concatenate(parts, axis=1), vp_ref[pl.ds(start, win * tq), :])
            outs.append((crow, acc))
        return outs

    def emit(rows, acc):
        o_ref[0, 0, rows, :] = jnp.where(first, acc[0:tq], acc[tq:]).astype(BF16)

    def fast(g, carry):
        zero = jnp.zeros((2 * tq, LANES), F32)
        qis = [g * SB_GROUP + t for t in range(SB_GROUP)]
        prep = [stacked_q(qi) for qi in qis]
        res = windows([(q2, qi + SB_WIN - 1, zero, zero) for qi, (_, q2) in zip(qis, prep)], True, SB_WIN)
        for qi, (rows, _), (crow, acc) in zip(qis, prep, res):
            emit(rows, acc)
            c_ref[qi] = crow
            acc_ref[qi] = acc
            unfinished = jnp.logical_and(qi >= SB_WIN, jnp.max(crow) > -SB_SKIP)
            more_ref[qi] = unfinished.astype(jnp.int32)
        return carry

    lax.fori_loop(0, nq // SB_GROUP, fast, 0)

    def finish(qi, carry):
        @pl.when(more_ref[qi] != 0)
        def _():
            rows, q2 = stacked_q(qi)

            def cond(c):
                p_top, mx, _, _ = c
                return jnp.logical_and(p_top >= SB_WIN - 1, mx > -SB_SKIP)

            def body(c):
                p_top, _, crow, acc = c
                (crow, acc), = windows([(q2, p_top, crow, acc)], False, 1)
                return p_top - 1, jnp.max(crow), crow, acc

            crow0 = c_ref[qi]
            _, _, _, acc = lax.while_loop(cond, body, (qi - 1, jnp.max(crow0), crow0, acc_ref[qi]))
            emit(rows, acc)
        return carry

    lax.fori_loop(SB_WIN, nq, finish, 0)


def _stick_breaking(qkv):
    _, bsz, s, _ = qkv.shape
    tq = SB_BLOCK
    nq = s // tq
    assert nq % SB_GROUP == 0
    n_pairs = D_SB // LANES
    jj = jnp.arange(tq)
    suffix = (jj[:, None] >= jj[None, :]).astype(BF16)
    uo = jnp.concatenate([suffix, suffix], axis=0)

    def seq(off):
        return pl.BlockSpec((1, 1, s, LANES), lambda b, p, off=off: (off * n_pairs + p, b, 0, 0))

    return pl.pallas_call(
        _sb_kernel,
        grid=(bsz, n_pairs),
        in_specs=[seq(0), seq(1), seq(2),
                  pl.BlockSpec((2 * tq, tq), lambda b, p: (0, 0))],
        out_specs=pl.BlockSpec((1, 1, s, LANES), lambda b, p: (p, b, 0, 0)),
        out_shape=jax.ShapeDtypeStruct((n_pairs, bsz, s, LANES), BF16),
        scratch_shapes=[pltpu.VMEM((s + (SB_WIN - 1) * tq, LANES), BF16),
                        pltpu.VMEM((s + (SB_WIN - 1) * tq, LANES), BF16),
                        pltpu.VMEM((nq, 2 * tq, LANES), F32),
                        pltpu.VMEM((nq, 2 * tq, LANES), F32),
                        pltpu.SMEM((nq,), jnp.int32)],
        compiler_params=_cparams(("parallel", "parallel")),
        name="stick_breaking",
    )(qkv, qkv, qkv, uo)


def _residual_ln(x, y, gate, gain, bias, alpha):
    r = alpha * x + gate * y
    mu = jnp.mean(r, axis=-1, keepdims=True)
    rc = r - mu
    var = jnp.mean(rc * rc, axis=-1, keepdims=True)
    return rc * lax.rsqrt(var + LN_EPS) * gain + bias


def _sb_lanes(os_ref):
    return jnp.concatenate([os_ref[p, 0] for p in range(os_ref.shape[0])], axis=-1)


def _emit(xn, outs, mod_next_ref):
    outs[0][0] = xn
    if mod_next_ref is not None:
        mn = mod_next_ref[0]
        outs[1][0] = (xn * (1.0 + mn[1:2]) + mn[0:1]).astype(BF16)


def _row_specs(bsz, s, d, tm, mod_next):
    x_spec = pl.BlockSpec((1, tm, d), lambda b, i: (b, i, 0))
    mod_spec = pl.BlockSpec((1, 3, d), lambda b, i: (b, 0, 0))
    ln_spec = pl.BlockSpec((2, d), lambda b, i: (0, 0))
    out_specs = [pl.BlockSpec((1, tm, d), lambda b, i: (b, i, 0))]
    out_shape = [jax.ShapeDtypeStruct((bsz, s, d), F32)]
    if mod_next is not None:
        out_specs.append(pl.BlockSpec((1, tm, d), lambda b, i: (b, i, 0)))
        out_shape.append(jax.ShapeDtypeStruct((bsz, s, d), BF16))
    return x_spec, mod_spec, ln_spec, out_specs, out_shape


def _mix_dense_kernel(alpha, has_next, n_chunks, oh_ref, os_ref, x_ref, wo_ref, wg_ref, wu_ref, wd_ref,
                      mod0_ref, mod1_ref, ln0_ref, ln1_ref, *rest):
    mod_next_ref = rest[0] if has_next else None
    outs = rest[1:] if has_next else rest
    tm = x_ref.shape[1]
    parts = [slice(r * tm // DENSE_SPLIT, (r + 1) * tm // DENSE_SPLIT) for r in range(DENSE_SPLIT)]
    ln0 = ln0_ref[...]
    ln1 = ln1_ref[...]
    m1 = mod1_ref[0]
    ff = wg_ref.shape[1]
    tf = ff // n_chunks
    osb = [jnp.concatenate([os_ref[p, 0, r, :] for p in range(os_ref.shape[0])], axis=-1) for r in parts]
    ys = [_dot(oh_ref[0, r, :], wo_ref[0:D_HGRN, :]) + _dot(o, wo_ref[D_HGRN:, :]) for r, o in zip(parts, osb)]
    x1s = [_residual_ln(x_ref[0, r, :], y, mod0_ref[0][2:3], ln0[0:1], ln0[1:2], alpha)
           for r, y in zip(parts, ys)]
    hs = [(x1 * (1.0 + m1[1:2]) + m1[0:1]).astype(BF16) for x1 in x1s]
    y2 = [None] * len(parts)
    for j in range(n_chunks):
        cols = slice(j * tf, (j + 1) * tf)
        gs = [_dot(h, wg_ref[:, cols]) for h in hs]
        us = [_dot(h, wu_ref[:, cols]) for h in hs]
        ps = [_dot((_silu(g) * u).astype(BF16), wd_ref[cols, :]) for g, u in zip(gs, us)]
        y2 = [p if y is None else y + p for y, p in zip(y2, ps)]
    for r, x1, y in zip(parts, x1s, y2):
        x2 = _residual_ln(x1, y, m1[2:3], ln1[0:1], ln1[1:2], alpha)
        outs[0][0, r, :] = x2
        if mod_next_ref is not None:
            mn = mod_next_ref[0]
            outs[1][0, r, :] = (x2 * (1.0 + mn[1:2]) + mn[0:1]).astype(BF16)


def _mix_dense_ln(o_h, o_s, x, w_out, wg, wu, wd, mod0, mod1, ln0, ln1, mod_next, alpha):
    bsz, s, d = x.shape
    ff = wg.shape[1]
    tm = min(s, 512)
    x_spec, mod_spec, ln_spec, out_specs, out_shape = _row_specs(bsz, s, d, tm, mod_next)
    once = pl.Buffered(1)
    in_specs = [pl.BlockSpec((1, tm, D_HGRN), lambda b, i: (b, i, 0)),
                pl.BlockSpec((D_SB // LANES, 1, tm, LANES), lambda b, i: (0, b, i, 0)),
                x_spec,
                pl.BlockSpec((d, d), lambda b, i: (0, 0), pipeline_mode=once),
                pl.BlockSpec((d, ff), lambda b, i: (0, 0), pipeline_mode=once),
                pl.BlockSpec((d, ff), lambda b, i: (0, 0), pipeline_mode=once),
                pl.BlockSpec((ff, d), lambda b, i: (0, 0), pipeline_mode=once),
                mod_spec, mod_spec, ln_spec, ln_spec]
    args = [o_h, o_s, x, w_out, wg, wu, wd, mod0, mod1, ln0, ln1]
    if mod_next is not None:
        in_specs.append(mod_spec)
        args.append(mod_next)
    n_chunks = 2 if (ff // 2) % LANES == 0 else 1
    return pl.pallas_call(
        functools.partial(_mix_dense_kernel, alpha, mod_next is not None, n_chunks),
        grid=(bsz, s // tm),
        in_specs=in_specs, out_specs=out_specs, out_shape=out_shape,
        compiler_params=_cparams(("parallel", "parallel")),
        name="out_proj_dense_ffn_ln",
    )(*args)


def _route(h, w_ref, tri_ref, run_ref):
    tm = h.shape[0]
    lane = lax.broadcasted_iota(jnp.int32, (tm, LANES), 1)
    logits = _dot(h, w_ref[...])
    neg = jnp.float32(-jnp.inf)
    logits = jnp.where(lane < N_EXPERTS, logits, neg)
    m1 = jnp.max(logits, axis=-1, keepdims=True)
    i1 = jnp.min(jnp.where(logits == m1, lane, LANES), axis=-1, keepdims=True)
    rest = jnp.where(lane == i1, neg, logits)
    m2 = jnp.max(rest, axis=-1, keepdims=True)
    i2 = jnp.min(jnp.where(rest == m2, lane, LANES), axis=-1, keepdims=True)
    e21 = jnp.exp(m2 - m1)
    w1 = 1.0 / (1.0 + e21)
    w2 = e21 / (1.0 + e21)

    hot1 = lane == i1
    hot2 = lane == i2
    both = jnp.where(jnp.logical_or(hot1, hot2), 1.0, 0.0)
    before = _dot(tri_ref[...], both.astype(BF16)) + run_ref[...]
    r1 = jnp.sum(jnp.where(hot1, before, 0.0), axis=-1, keepdims=True)
    r2 = jnp.sum(jnp.where(hot2, before, 0.0), axis=-1, keepdims=True)
    run_ref[...] = run_ref[...] + jnp.sum(both, axis=0, keepdims=True)

    cols = [i1.astype(F32), i2.astype(F32), r1, r2, w1, w2]
    route = jnp.zeros((tm, LANES), F32)
    for c, val in enumerate(cols):
        route = jnp.where(lane == c, val, route)
    return route[:, 0:8]


def _outproj_route_kernel(alpha, oh_ref, os_ref, x_ref, w_ref, mod_ref, ln_ref, modn_ref, wr_ref, tri_ref,
                          x_out, hp_out, route_out, count_out, run_ref):
    @pl.when(jnp.logical_and(pl.program_id(0) == 0, pl.program_id(1) == 0))
    def _():
        run_ref[...] = jnp.zeros_like(run_ref)

    y = _dot(oh_ref[0], w_ref[0:D_HGRN, :]) + _dot(_sb_lanes(os_ref), w_ref[D_HGRN:, :])
    ln = ln_ref[...]
    xn = _residual_ln(x_ref[0], y, mod_ref[0][2:3], ln[0:1], ln[1:2], alpha)
    x_out[0] = xn
    mn = modn_ref[0]
    hn = xn * (1.0 + mn[1:2]) + mn[0:1]
    _store_planes(hp_out, _pack_rows(hn))
    route_out[0] = _route(hn.astype(BF16), wr_ref, tri_ref, run_ref)
    count_out[...] = run_ref[...]


def _out_proj_route(o_h, o_s, x, w_out, mod, ln, mod_next, w_router, alpha):
    bsz, s, d = x.shape
    tm = min(s, 512)
    w_pad = jnp.zeros((d, LANES), BF16).at[:, :N_EXPERTS].set(w_router.astype(BF16))
    jj = jnp.arange(tm)
    tri = (jj[None, :] < jj[:, None]).astype(BF16)
    row = lambda w: pl.BlockSpec((1, tm, w), lambda b, i: (b, i, 0))
    mod_spec = pl.BlockSpec((1, 3, d), lambda b, i: (b, 0, 0))
    return pl.pallas_call(
        functools.partial(_outproj_route_kernel, alpha),
        grid=(bsz, s // tm),
        in_specs=[row(D_HGRN),
                  pl.BlockSpec((D_SB // LANES, 1, tm, LANES), lambda b, i: (0, b, i, 0)),
                  row(d),
                  pl.BlockSpec((d, d), lambda b, i: (0, 0), pipeline_mode=pl.Buffered(1)),
                  mod_spec,
                  pl.BlockSpec((2, d), lambda b, i: (0, 0)),
                  mod_spec,
                  pl.BlockSpec((d, LANES), lambda b, i: (0, 0)),
                  pl.BlockSpec((tm, tm), lambda b, i: (0, 0))],
        out_specs=[row(d),
                   pl.BlockSpec((2, 1, tm, d // 4), lambda b, i: (0, b, i, 0)),
                   row(8),
                   pl.BlockSpec((1, LANES), lambda b, i: (0, 0))],
        out_shape=[jax.ShapeDtypeStruct((bsz, s, d), F32),
                   jax.ShapeDtypeStruct((2, bsz, s, d // 4), U32),
                   jax.ShapeDtypeStruct((bsz, s, 8), F32),
                   jax.ShapeDtypeStruct((1, LANES), F32)],
        scratch_shapes=[pltpu.VMEM((1, LANES), F32)],
        compiler_params=_cparams(("arbitrary", "arbitrary")),
        name="out_proj_ln_route",
    )(o_h, o_s, x, w_out, mod, ln, mod_next, w_pad, tri)


def _gmm_kernel(te_ref, nt_ref, x_ref, wg_ref, wu_ref, wd_ref, o_ref, acc_ref, xb_ref):
    i = pl.program_id(0)
    j = pl.program_id(1)
    nj = pl.num_programs(1)
    live = i < nt_ref[0]

    @pl.when(jnp.logical_and(live, j == 0))
    def _():
        acc_ref[...] = jnp.zeros_like(acc_ref)
        xb_ref[...] = _load_planes(x_ref[0], x_ref[1]).astype(BF16)

    @pl.when(live)
    def _():
        x = xb_ref[...]
        g = _dot(x, wg_ref[0, 0].astype(BF16))
        u = _dot(x, wu_ref[0, 0].astype(BF16))
        acc_ref[...] += _dot((_silu(g) * u).astype(BF16), wd_ref[0, 0].astype(BF16))

    @pl.when(jnp.logical_and(live, j == nj - 1))
    def _():
        _store_planes(o_ref, _pack_rows(acc_ref[...]))

    @pl.when(jnp.logical_and(jnp.logical_not(live), j == nj - 1))
    def _():
        o_ref[...] = jnp.zeros_like(o_ref)


def _grouped_swiglu(xs, tile_expert, n_tiles_used, wg, wu, wd, layer):
    _, n_slots, pw = xs.shape
    d = 4 * pw
    ff = wg.shape[3]
    tm, tf = MOE_TILE, MOE_FF_TILE
    n_tiles = n_slots // tm

    def row_map(i, j, te, nt):
        return (0, jnp.minimum(i, nt[0] - 1), 0)

    def ff_block(i, j, te, nt):
        return jnp.where(i < nt[0], j, ff // tf - 1)

    grid_spec = pltpu.PrefetchScalarGridSpec(
        num_scalar_prefetch=2,
        grid=(n_tiles, ff // tf),
        in_specs=[pl.BlockSpec((2, tm, pw), row_map),
                  pl.BlockSpec((1, 1, d, tf), lambda i, j, te, nt: (layer, te[i], 0, ff_block(i, j, te, nt))),
                  pl.BlockSpec((1, 1, d, tf), lambda i, j, te, nt: (layer, te[i], 0, ff_block(i, j, te, nt))),
                  pl.BlockSpec((1, 1, tf, d), lambda i, j, te, nt: (layer, te[i], ff_block(i, j, te, nt), 0))],
        out_specs=pl.BlockSpec((2, tm, pw), lambda i, j, te, nt: (0, i, 0)),
        scratch_shapes=[pltpu.VMEM((tm, d), F32), pltpu.VMEM((tm, d), BF16)],
    )
    return pl.pallas_call(
        _gmm_kernel,
        grid_spec=grid_spec,
        out_shape=jax.ShapeDtypeStruct((2, n_slots, pw), U32),
        compiler_params=_cparams(("arbitrary", "arbitrary")),
        name="moe_grouped_swiglu",
    )(tile_expert, n_tiles_used, xs, wg, wu, wd)


def _combine_kernel(alpha, has_next, ya_ref, yb_ref, r_ref, x_ref, mod_ref, ln_ref, *rest):
    mod_next_ref = rest[0] if has_next else None
    outs = rest[1:] if has_next else rest
    r = r_ref[0]
    y = (r[:, 4:5] * _load_planes(ya_ref[0, 0, 0], ya_ref[0, 1, 0])
         + r[:, 5:6] * _load_planes(yb_ref[0, 0, 0], yb_ref[0, 1, 0]))
    ln = ln_ref[...]
    xn = _residual_ln(x_ref[0], y, mod_ref[0][2:3], ln[0:1], ln[1:2], alpha)
    _emit(xn, outs, mod_next_ref)


def _combine_ln(yab, route, x, mod, ln, mod_next, alpha):
    bsz, s, d = x.shape
    tm = min(s, 512)
    x_spec, mod_spec, ln_spec, out_specs, out_shape = _row_specs(bsz, s, d, tm, mod_next)
    yab = yab.reshape(2, 2, bsz, s, d // 4)
    in_specs = [pl.BlockSpec((1, 2, 1, tm, d // 4), lambda b, i: (0, 0, b, i, 0)),
                pl.BlockSpec((1, 2, 1, tm, d // 4), lambda b, i: (1, 0, b, i, 0)),
                pl.BlockSpec((1, tm, 8), lambda b, i: (b, i, 0)),
                x_spec, mod_spec, ln_spec]
    args = [yab, yab, route.reshape(bsz, s, 8), x, mod, ln]
    if mod_next is not None:
        in_specs.append(mod_spec)
        args.append(mod_next)
    return pl.pallas_call(
        functools.partial(_combine_kernel, alpha, mod_next is not None),
        grid=(bsz, s // tm),
        in_specs=in_specs, out_specs=out_specs, out_shape=out_shape,
        compiler_params=_cparams(("parallel", "parallel")),
        name="moe_combine_ln",
    )(*args)


def _sc_mesh():
    return plsc.VectorSubcoreMesh(core_axis_name="c", subcore_axis_name="s")


def _dispatch_rows(rows, idx, n_out):
    m, w = rows.shape

    @pl.kernel(out_type=jax.ShapeDtypeStruct((n_out, w), rows.dtype), mesh=_sc_mesh(), scratch_types=[])
    def scatter(rows_hbm, idx_hbm, out_hbm):
        def body(rows_vmem, idx0_vmem, idx1_vmem):
            pltpu.sync_copy(rows_vmem, out_hbm.at[idx0_vmem.at[0]])
            pltpu.sync_copy(rows_vmem, out_hbm.at[idx1_vmem.at[0]])

        pltpu.emit_pipeline(
            body, grid=(m // SC_WINDOW,),
            in_specs=[pl.BlockSpec((SC_WINDOW, w), lambda i: (i, 0)),
                      pl.BlockSpec((1, SC_WINDOW), lambda i: (0, i)),
                      pl.BlockSpec((1, SC_WINDOW), lambda i: (1, i))],
            out_specs=[],
            core_axis_name=("c", "s"),
            dimension_semantics=(pltpu.PARALLEL,),
        )(rows_hbm, idx_hbm, idx_hbm)

    return scatter(rows, idx)


def _collect_rows(rows, idx):
    m = idx.shape[1]
    w = rows.shape[1]

    @pl.kernel(out_type=jax.ShapeDtypeStruct((m, w), rows.dtype), mesh=_sc_mesh(), scratch_types=[])
    def gather(rows_hbm, idx_hbm, out_hbm):
        def body(idx_vmem, out_vmem):
            pltpu.sync_copy(rows_hbm.at[idx_vmem.at[0]], out_vmem)

        pltpu.emit_pipeline(
            body, grid=(m // SC_WINDOW,),
            in_specs=[pl.BlockSpec((1, SC_WINDOW), lambda i: (0, i))],
            out_specs=[pl.BlockSpec((SC_WINDOW, w), lambda i: (i, 0))],
            core_axis_name=("c", "s"),
            dimension_semantics=(pltpu.PARALLEL,),
        )(idx_hbm, out_hbm)

    return gather(rows, idx)


def _moe_ffn_ln(h, route, counts, x, wg, wu, wd, layer, mod, ln, mod_next, alpha):
    bsz, s, d = x.shape
    n = bsz * s
    hp = h.reshape(2, n, d // 4)
    route = route.reshape(n, 8)

    tm = MOE_TILE
    cnt = counts[0, :N_EXPERTS].astype(jnp.int32)
    padded = ((cnt + tm - 1) // tm) * tm
    ends = jnp.cumsum(padded)
    offs = ends - padded
    e = route[:, 0:2].astype(jnp.int32)
    rank = route[:, 2:4].astype(jnp.int32)
    pos = (jnp.take(offs, e) + rank).T
    n_tiles = (2 * n) // tm + N_EXPERTS
    n_slots = n_tiles * tm
    tile_start = jnp.arange(n_tiles, dtype=jnp.int32) * tm
    tile_expert = jnp.minimum(jnp.sum(tile_start[:, None] >= ends[None, :], axis=1),
                              N_EXPERTS - 1).astype(jnp.int32)
    n_used = (ends[-1] // tm).astype(jnp.int32).reshape(1)

    idx = jnp.concatenate([pos, pos + n_slots], axis=1)
    xs = _dispatch_rows(hp.reshape(2 * n, d // 4), idx, 2 * n_slots).reshape(2, n_slots, d // 4)
    ys = _grouped_swiglu(xs, tile_expert, n_used, wg, wu, wd, layer)
    yab = _collect_rows(ys.reshape(2 * n_slots, d // 4), idx.reshape(1, 4 * n)).reshape(2, 2, n, d // 4)
    return _combine_ln(yab, route, x, mod, ln, mod_next, alpha)


def kernel(x, c, w_ada, b_ada, w_in, w_out, hgrn_lb_logits, hgrn_norm_gain,
           w_dense_gate, w_dense_up, w_dense_down, w_router, w_moe_gate, w_moe_up, w_moe_down,
           ln_gain, ln_bias):
    depth = w_in.shape[0]
    bsz, s, d = x.shape
    alpha = (2 * depth) ** 0.25

    mods = _ada_all(c, w_ada, b_ada).reshape(depth, 2, bsz, 3, d)
    lb_all = _lower_bounds(hgrn_lb_logits)
    ln_all = jnp.stack([ln_gain, ln_bias], axis=2)

    h = _modulate(x, mods[0, 0])
    for layer in range(depth):
        proj_h, proj_s = _in_proj(h.reshape(bsz * s, d), w_in[layer].astype(BF16))
        o_h = _hgrn(proj_h.reshape(bsz, s, 4 * D_HGRN), lb_all[layer:layer + 1],
                    hgrn_norm_gain[layer:layer + 1])
        o_s = _stick_breaking(proj_s.reshape(-1, bsz, s, LANES))
        mod_next = mods[layer + 1, 0] if layer + 1 < depth else None
        j = layer // 2
        if layer % 2 == 0:
            res = _mix_dense_ln(o_h, o_s, x, w_out[layer].astype(BF16), w_dense_gate[j].astype(BF16),
                                w_dense_up[j].astype(BF16), w_dense_down[j].astype(BF16),
                                mods[layer, 0], mods[layer, 1], ln_all[layer, 0], ln_all[layer, 1],
                                mod_next, alpha)
        else:
            x, h, route, counts = _out_proj_route(o_h, o_s, x, w_out[layer].astype(BF16), mods[layer, 0],
                                                  ln_all[layer, 0], mods[layer, 1], w_router[j], alpha)
            res = _moe_ffn_ln(h, route, counts, x, w_moe_gate, w_moe_up, w_moe_down, j,
                              mods[layer, 1], ln_all[layer, 1], mod_next, alpha)
        if mod_next is not None:
            x, h = res
        else:
            (x,) = res
    return x
```

```python
import functools
import math

import jax
import jax.numpy as jnp
from jax import lax
from jax.experimental import pallas as pl
from jax.experimental.pallas import tpu as pltpu
from jax.experimental.pallas import tpu_sc as plsc

D_MODEL = 1024
D_HGRN = 512
HGRN_HEADS = 4
HGRN_DH = 128
D_SB = 512
SB_HEADS = 8
SB_DH = 64
D_IN_PROJ = 4 * D_HGRN + 3 * D_SB
N_EXPERTS = 8
LN_EPS = 1e-5
RMS_EPS = 1e-6

LOG2E = math.log2(math.e)
LANES = 128
VMEM_LIMIT = 52 * 1024 * 1024

HG_UNIT = 128
HG_BAND = 4
HG_PAD = 8
HG_LEVELS = (64, 32, 16, 8, 4)
SB_BLOCK = 128
SB_SKIP = 110.0 * LOG2E
SB_GROUP = 4
SB_WIN = 3
MOE_TILE = 1024
MOE_FF_TILE = 512
DENSE_SPLIT = 2
SC_WINDOW = 128

F32 = jnp.float32
BF16 = jnp.bfloat16
U32 = jnp.uint32


def _cparams(sem):
    return pltpu.CompilerParams(dimension_semantics=sem, vmem_limit_bytes=VMEM_LIMIT)


def _dot(a, b):
    return jnp.dot(a, b, preferred_element_type=F32)


def _dot_nt(a, b):
    return lax.dot_general(a, b, (((1,), (1,)), ((), ())), preferred_element_type=F32)


def _dot_tn(a, b):
    return lax.dot_general(a, b, (((0,), (0,)), ((), ())), preferred_element_type=F32)


def _pack_rows(v):
    w = v.shape[-1] // 2
    lo = lax.bitcast_convert_type(v[:, :w].astype(BF16).astype(F32), U32)
    hi = lax.bitcast_convert_type(v[:, w:].astype(BF16).astype(F32), U32)
    return (lo >> 16) | hi


def _unpack_rows(p):
    lo = lax.bitcast_convert_type(p << 16, F32)
    hi = lax.bitcast_convert_type(p & jnp.uint32(0xFFFF0000), F32)
    return jnp.concatenate([lo, hi], axis=-1)


def _store_planes(ref, p):
    w = p.shape[-1] // 2
    ref[0] = p[:, :w].reshape(ref.shape[1:])
    ref[1] = p[:, w:].reshape(ref.shape[1:])


def _load_planes(p0, p1):
    return _unpack_rows(jnp.concatenate([p0, p1], axis=-1))


def _sigmoid(x):
    return 1.0 / (1.0 + jnp.exp(-x))


def _silu(x):
    return x * _sigmoid(x)


def _ada_kernel(c_ref, w_ref, b_ref, o_ref):
    sc = _silu(c_ref[...])
    o_ref[0] = jnp.dot(sc, w_ref[0], preferred_element_type=F32,
                       precision=lax.Precision.HIGHEST) + b_ref[0]


def _ada_all(c, w_ada, b_ada):
    n_sub = w_ada.shape[0] * w_ada.shape[1]
    bsz, d = c.shape
    w = w_ada.reshape(n_sub, d, 3 * d)
    b = b_ada.reshape(n_sub, 1, 3 * d)
    tn = 1024
    return pl.pallas_call(
        _ada_kernel,
        grid=(n_sub, 3 * d // tn),
        in_specs=[pl.BlockSpec((bsz, d), lambda s, j: (0, 0)),
                  pl.BlockSpec((1, d, tn), lambda s, j: (s, 0, j)),
                  pl.BlockSpec((1, 1, tn), lambda s, j: (s, 0, j))],
        out_specs=pl.BlockSpec((1, bsz, tn), lambda s, j: (s, 0, j)),
        out_shape=jax.ShapeDtypeStruct((n_sub, bsz, 3 * d), F32),
        compiler_params=_cparams(("parallel", "parallel")),
        name="ada_modulation",
    )(c, w, b)


def _lb_kernel(l_ref, o_ref):
    x = l_ref[...]
    m = jnp.max(x, axis=0, keepdims=True)
    e = jnp.exp(x - m)
    p = e / jnp.sum(e, axis=0, keepdims=True)
    depth = x.shape[0]
    run = jnp.zeros_like(p[0:1])
    rows = []
    for i in range(depth):
        run = run + p[i:i + 1]
        rows.append(run - p[0:1])
    o_ref[...] = jnp.concatenate(rows, axis=0)


def _lower_bounds(lb_logits):
    return pl.pallas_call(
        _lb_kernel,
        out_shape=jax.ShapeDtypeStruct(lb_logits.shape, F32),
        name="hgrn_lower_bounds",
    )(lb_logits)


def _inproj_kernel(modulate, h_ref, *rest):
    if modulate:
        m_ref, w_ref, oh_ref, os_ref = rest
        m = m_ref[0]
        h = (h_ref[...] * (1.0 + m[1:2]) + m[0:1]).astype(BF16)
    else:
        w_ref, oh_ref, os_ref = rest
        h = h_ref[...]
    tn = 512
    nh = oh_ref.shape[1] // tn
    for j in range(nh):
        oh_ref[:, j * tn:(j + 1) * tn] = _dot(h, w_ref[:, j * tn:(j + 1) * tn]).astype(BF16)
    per = tn // LANES
    for j in range(os_ref.shape[0] // per):
        blk = _dot(h, w_ref[:, (nh + j) * tn:(nh + j + 1) * tn]).astype(BF16)
        for c in range(per):
            os_ref[j * per + c] = blk[:, c * LANES:(c + 1) * LANES]


def _in_proj(h, w_in, mod=None):
    bsz, s, d = h.shape
    n = bsz * s
    dout = w_in.shape[1]
    dh = 4 * D_HGRN
    n_slabs = (dout - dh) // LANES
    tm = min(s, 1024)
    per = s // tm
    in_specs = [pl.BlockSpec((tm, d), lambda b, i: (b * per + i, 0))]
    args = [h.reshape(n, d)]
    if mod is not None:
        in_specs.append(pl.BlockSpec((1, 3, d), lambda b, i: (b, 0, 0)))
        args.append(mod)
    in_specs.append(pl.BlockSpec((d, dout), lambda b, i: (0, 0), pipeline_mode=pl.Buffered(1)))
    args.append(w_in)
    return pl.pallas_call(
        functools.partial(_inproj_kernel, mod is not None),
        grid=(bsz, per),
        in_specs=in_specs,
        out_specs=[pl.BlockSpec((tm, dh), lambda b, i: (b * per + i, 0)),
                   pl.BlockSpec((n_slabs, tm, LANES), lambda b, i: (0, b * per + i, 0))],
        out_shape=[jax.ShapeDtypeStruct((n, dh), BF16),
                   jax.ShapeDtypeStruct((n_slabs, n, LANES), BF16)],
        compiler_params=_cparams(("parallel", "parallel")),
        name="in_proj",
    )(*args)


def _hgrn_consts():
    t = jnp.arange(HG_UNIT)[:, None]
    s = jnp.arange(HG_UNIT)[None, :]
    tril = (s <= t).astype(BF16)
    lv = jnp.full((HG_UNIT, HG_UNIT), -1, jnp.int32)
    for idx, m in enumerate(HG_LEVELS):
        blk = 2 * m
        hit = (t // blk == s // blk) & (t % blk >= m) & (s % blk < m)
        lv = jnp.where(hit, idx, lv)
    return tril, lv


def _hgrn_kernel(q_ref, f_ref, i_ref, g_ref, lb_ref, gain_ref, tril_ref, lv_ref, o_ref,
                 states_ref, kpads_ref, bpads_ref, vpads_ref):
    @pl.when(pl.program_id(1) == 0)
    def _():
        states_ref[...] = jnp.zeros_like(states_ref)

    heads = range(HGRN_HEADS)
    zpad = jnp.zeros((HGRN_HEADS, HG_PAD, HGRN_DH), F32)
    kpads_ref[:, 0:HG_PAD, :] = zpad
    bpads_ref[:, 0:HG_PAD, :] = zpad
    vpads_ref[:, 0:HG_PAD, :] = zpad

    n_units = q_ref.shape[1] // HG_UNIT
    rowi = lax.broadcasted_iota(jnp.int32, (HG_UNIT, HGRN_DH), 0)
    tmod = jnp.bitwise_and(rowi, HG_BAND - 1)

    def unit(u, carry):
        rows = pl.ds(pl.multiple_of(u * HG_UNIT, HG_UNIT), HG_UNIT)
        lanes = [slice(h * HGRN_DH, (h + 1) * HGRN_DH) for h in heads]

        kk, qq, v, vb, b3 = [], [], [], [], []
        for h in heads:
            lb = lb_ref[:, lanes[h]]
            f = lb + (1.0 - lb) * _sigmoid(f_ref[0, rows, lanes[h]].astype(F32))
            logf = jnp.log2(f)
            kk.append(1.0 - f)
            qq.append(_silu(q_ref[0, rows, lanes[h]].astype(F32)))
            v.append(i_ref[0, rows, lanes[h]].astype(F32))
            vb.append(i_ref[0, rows, lanes[h]])
            p0 = logf.astype(BF16)
            r1 = logf - p0.astype(F32)
            p1 = r1.astype(BF16)
            p2 = (r1 - p1.astype(F32)).astype(BF16)
            b3.append(_dot(tril_ref[...], jnp.concatenate([p0, p1, p2], axis=1)))
        b = [t[:, 0:HGRN_DH] + t[:, HGRN_DH:2 * HGRN_DH] + t[:, 2 * HGRN_DH:] for t in b3]

        state = [states_ref[h] for h in heads]
        o = [_dot_nt((qq[h] * jnp.exp2(b[h])).astype(BF16), state[h].astype(BF16)) for h in heads]

        lv = lv_ref[...]
        scores = [jnp.zeros((HG_UNIT, HG_UNIT), F32) for _ in heads]
        for idx, m in enumerate(HG_LEVELS):
            blk = 2 * m
            nb = HG_UNIT // blk
            upper = jnp.bitwise_and(rowi, m) != 0
            xl = []
            for h in heads:
                ref = b[h].reshape(nb, blk, HGRN_DH)[:, m - 1:m, :]
                ref = jnp.broadcast_to(ref, (nb, blk, HGRN_DH)).reshape(HG_UNIT, HGRN_DH)
                xl.append((jnp.where(upper, qq[h], kk[h]) * jnp.exp2(-jnp.abs(b[h] - ref))).astype(BF16))
            s_l = [_dot_nt(x, x) for x in xl]
            scores = [jnp.where(lv == idx, s_l[h], scores[h]) for h in heads]
        intra = [_dot(scores[h].astype(BF16), vb[h]) for h in heads]

        for h in heads:
            kpads_ref[h, HG_PAD:, :] = kk[h]
            bpads_ref[h, HG_PAD:, :] = b[h]
            vpads_ref[h, HG_PAD:, :] = v[h]
        band = [jnp.zeros((HG_UNIT, HGRN_DH), F32) for _ in heads]
        for d in range(HG_BAND):
            win = pl.ds(HG_PAD - d, HG_UNIT)
            for h in heads:
                p = qq[h] * kpads_ref[h, win, :] * jnp.exp2(b[h] - bpads_ref[h, win, :])
                s_d = jnp.sum(p, axis=-1, keepdims=True)
                band[h] = band[h] + jnp.where(tmod >= d, s_d * vpads_ref[h, win, :], 0.0)

        for h in heads:
            b_last = b[h][HG_UNIT - 1:HG_UNIT, :]
            kdec = kk[h] * jnp.exp2(b_last - b[h])
            states_ref[h] = state[h] * jnp.exp2(b_last) + _dot_tn(vb[h], kdec.astype(BF16))
        for h in heads:
            oh = o[h] + intra[h] + band[h]
            oh = oh * lax.rsqrt(jnp.mean(oh * oh, axis=-1, keepdims=True) + RMS_EPS)
            gg = g_ref[0, rows, lanes[h]].astype(F32)
            o_ref[0, rows, lanes[h]] = (oh * gain_ref[:, lanes[h]] * _silu(gg)).astype(BF16)
        return carry

    lax.fori_loop(0, n_units, unit, 0)


def _hgrn(proj, lb, gain):
    bsz, s, _ = proj.shape
    ts = min(s, 1024)
    tril, lv = _hgrn_consts()

    def col(off):
        return pl.BlockSpec((1, ts, D_HGRN), lambda b, i, off=off: (b, i, off))

    return pl.pallas_call(
        _hgrn_kernel,
        grid=(bsz, s // ts),
        in_specs=[col(0), col(1), col(2), col(3),
                  pl.BlockSpec((1, D_HGRN), lambda b, i: (0, 0)),
                  pl.BlockSpec((1, D_HGRN), lambda b, i: (0, 0)),
                  pl.BlockSpec((HG_UNIT, HG_UNIT), lambda b, i: (0, 0)),
                  pl.BlockSpec((HG_UNIT, HG_UNIT), lambda b, i: (0, 0))],
        out_specs=pl.BlockSpec((1, ts, D_HGRN), lambda b, i: (b, i, 0)),
        out_shape=jax.ShapeDtypeStruct((bsz, s, D_HGRN), BF16),
        scratch_shapes=[pltpu.VMEM((HGRN_HEADS, HGRN_DH, HGRN_DH), F32),
                        pltpu.VMEM((HGRN_HEADS, HG_UNIT + HG_PAD, HGRN_DH), F32),
                        pltpu.VMEM((HGRN_HEADS, HG_UNIT + HG_PAD, HGRN_DH), F32),
                        pltpu.VMEM((HGRN_HEADS, HG_UNIT + HG_PAD, HGRN_DH), F32)],
        compiler_params=_cparams(("parallel", "arbitrary")),
        name="hgrn2",
    )(proj, proj, proj, proj, lb, gain, tril, lv)


def _sb_kernel(q_ref, k_ref, v_ref, uo_ref, o_ref, kp_ref, vp_ref, c_ref, acc_ref, more_ref):
    tq = SB_BLOCK
    nq = q_ref.shape[2] // tq
    pad = (SB_WIN - 1) * tq
    kp_ref[0:pad, :] = jnp.zeros((pad, LANES), BF16)
    vp_ref[0:pad, :] = jnp.zeros((pad, LANES), BF16)
    kp_ref[pad:, :] = (k_ref[0, 0].astype(F32) * LOG2E).astype(BF16)
    vp_ref[pad:, :] = v_ref[0, 0]

    first = lax.broadcasted_iota(jnp.int32, (tq, LANES), 1) < SB_DH
    rowi = lax.broadcasted_iota(jnp.int32, (2 * tq, tq), 0)
    coli = lax.broadcasted_iota(jnp.int32, (2 * tq, tq), 1)
    strict = coli < jnp.bitwise_and(rowi, tq - 1)

    def stacked_q(qi):
        rows = pl.ds(pl.multiple_of(qi * tq, tq), tq)
        q = q_ref[0, 0, rows, :] * (SB_DH ** -0.5)
        zero = jnp.zeros_like(q)
        return rows, jnp.concatenate([jnp.where(first, q, zero), jnp.where(first, zero, q)], axis=0)

    def windows(items, diag, win):
        order = list(reversed(range(win)))
        zs, cts = [], []
        for q2, p_top, _, _ in items:
            start = pl.multiple_of((p_top - (win - 1)) * tq, tq)
            zs.append(_dot_nt(q2, kp_ref[pl.ds(start, win * tq), :]))
        for z in zs:
            logn = -(jnp.maximum(z, 0.0) + jnp.log2(1.0 + jnp.exp2(-jnp.abs(z))))
            lhs = []
            for j in order:
                lj = logn[:, j * tq:(j + 1) * tq]
                if diag and j == win - 1:
                    lj = jnp.where(strict, lj, 0.0)
                hi = lj.astype(BF16)
                lo = (lj - hi.astype(F32)).astype(BF16)
                lhs.append(jnp.concatenate([hi, lo], axis=1))
            cts.append(_dot(jnp.concatenate(lhs, axis=0), uo_ref[...]))
        outs = []
        for (q2, p_top, crow, acc), z, ct in zip(items, zs, cts):
            start = pl.multiple_of((p_top - (win - 1)) * tq, tq)
            parts = [None] * win
            for idx, j in enumerate(order):
                blk = ct[idx * 2 * tq:(idx + 1) * 2 * tq]
                a = jnp.exp2(z[:, j * tq:(j + 1) * tq] + (crow + blk))
                if diag and j == win - 1:
                    a = jnp.where(strict, a, 0.0)
                parts[j] = a.astype(BF16)
                crow = crow + jnp.broadcast_to(blk[:, 0:1], blk.shape)
            acc = acc + _dot(jnp.concatenate(parts, axis=1), vp_ref[pl.ds(start, win * tq), :])
            outs.append((crow, acc))
        return outs

    def emit(rows, acc):
        o_ref[0, 0, rows, :] = jnp.where(first, acc[0:tq], acc[tq:]).astype(BF16)

    def fast(g, carry):
        zero = jnp.zeros((2 * tq, LANES), F32)
        qis = [g * SB_GROUP + t for t in range(SB_GROUP)]
        prep = [stacked_q(qi) for qi in qis]
        res = windows([(q2, qi + SB_WIN - 1, zero, zero) for qi, (_, q2) in zip(qis, prep)], True, SB_WIN)
        for qi, (rows, _), (crow, acc) in zip(qis, prep, res):
            emit(rows, acc)
            c_ref[qi] = crow
            acc_ref[qi] = acc
            unfinished = jnp.logical_and(qi >= SB_WIN, jnp.max(crow) > -SB_SKIP)
            more_ref[qi] = unfinished.astype(jnp.int32)
        return carry

    lax.fori_loop(0, nq // SB_GROUP, fast, 0)

    def finish(qi, carry):
        @pl.when(more_ref[qi] != 0)
        def _():
            rows, q2 = stacked_q(qi)

            def cond(c):
                p_top, mx, _, _ = c
                return jnp.logical_and(p_top >= SB_WIN - 1, mx > -SB_SKIP)

            def body(c):
                p_top, _, crow, acc = c
                (crow, acc), = windows([(q2, p_top, crow, acc)], False, 1)
                return p_top - 1, jnp.max(crow), crow, acc

            crow0 = c_ref[qi]
            _, _, _, acc = lax.while_loop(cond, body, (qi - 1, jnp.max(crow0), crow0, acc_ref[qi]))
            emit(rows, acc)
        return carry

    lax.fori_loop(SB_WIN, nq, finish, 0)


def _stick_breaking(qkv):
    _, bsz, s, _ = qkv.shape
    tq = SB_BLOCK
    nq = s // tq
    assert nq % SB_GROUP == 0
    n_pairs = D_SB // LANES
    jj = jnp.arange(tq)
    suffix = (jj[:, None] >= jj[None, :]).astype(BF16)
    uo = jnp.concatenate([suffix, suffix], axis=0)

    def seq(off):
        return pl.BlockSpec((1, 1, s, LANES), lambda b, p, off=off: (off * n_pairs + p, b, 0, 0))

    return pl.pallas_call(
        _sb_kernel,
        grid=(bsz, n_pairs),
        in_specs=[seq(0), seq(1), seq(2),
                  pl.BlockSpec((2 * tq, tq), lambda b, p: (0, 0))],
        out_specs=pl.BlockSpec((1, 1, s, LANES), lambda b, p: (p, b, 0, 0)),
        out_shape=jax.ShapeDtypeStruct((n_pairs, bsz, s, LANES), BF16),
        scratch_shapes=[pltpu.VMEM((s + (SB_WIN - 1) * tq, LANES), BF16),
                        pltpu.VMEM((s + (SB_WIN - 1) * tq, LANES), BF16),
                        pltpu.VMEM((nq, 2 * tq, LANES), F32),
                        pltpu.VMEM((nq, 2 * tq, LANES), F32),
                        pltpu.SMEM((nq,), jnp.int32)],
        compiler_params=_cparams(("parallel", "parallel")),
        name="stick_breaking",
    )(qkv, qkv, qkv, uo)


def _residual_ln(x, y, gate, gain, bias, alpha):
    r = alpha * x + gate * y
    mu = jnp.mean(r, axis=-1, keepdims=True)
    rc = r - mu
    var = jnp.mean(rc * rc, axis=-1, keepdims=True)
    return rc * lax.rsqrt(var + LN_EPS) * gain + bias


def _sb_lanes(os_ref):
    return jnp.concatenate([os_ref[p, 0] for p in range(os_ref.shape[0])], axis=-1)


def _emit(xn, outs, mod_next_ref):
    outs[0][0] = xn
    if mod_next_ref is not None:
        mn = mod_next_ref[0]
        outs[1][0] = (xn * (1.0 + mn[1:2]) + mn[0:1]).astype(BF16)


def _row_specs(bsz, s, d, tm, mod_next):
    x_spec = pl.BlockSpec((1, tm, d), lambda b, i: (b, i, 0))
    mod_spec = pl.BlockSpec((1, 3, d), lambda b, i: (b, 0, 0))
    ln_spec = pl.BlockSpec((2, d), lambda b, i: (0, 0))
    out_specs = [pl.BlockSpec((1, tm, d), lambda b, i: (b, i, 0))]
    out_shape = [jax.ShapeDtypeStruct((bsz, s, d), F32)]
    if mod_next is not None:
        out_specs.append(pl.BlockSpec((1, tm, d), lambda b, i: (b, i, 0)))
        out_shape.append(jax.ShapeDtypeStruct((bsz, s, d), BF16))
    return x_spec, mod_spec, ln_spec, out_specs, out_shape


def _mix_dense_kernel(alpha, has_next, n_chunks, oh_ref, os_ref, x_ref, wo_ref, wg_ref, wu_ref, wd_ref,
                      mod0_ref, mod1_ref, ln0_ref, ln1_ref, *rest):
    mod_next_ref = rest[0] if has_next else None
    outs = rest[1:] if has_next else rest
    tm = x_ref.shape[1]
    parts = [slice(r * tm // DENSE_SPLIT, (r + 1) * tm // DENSE_SPLIT) for r in range(DENSE_SPLIT)]
    ln0 = ln0_ref[...]
    ln1 = ln1_ref[...]
    m1 = mod1_ref[0]
    ff = wg_ref.shape[1]
    tf = ff // n_chunks
    osb = [jnp.concatenate([os_ref[p, 0, r, :] for p in range(os_ref.shape[0])], axis=-1) for r in parts]
    ys = [_dot(oh_ref[0, r, :], wo_ref[0:D_HGRN, :]) + _dot(o, wo_ref[D_HGRN:, :]) for r, o in zip(parts, osb)]
    x1s = [_residual_ln(x_ref[0, r, :], y, mod0_ref[0][2:3], ln0[0:1], ln0[1:2], alpha)
           for r, y in zip(parts, ys)]
    hs = [(x1 * (1.0 + m1[1:2]) + m1[0:1]).astype(BF16) for x1 in x1s]
    y2 = [None] * len(parts)
    for j in range(n_chunks):
        cols = slice(j * tf, (j + 1) * tf)
        gs = [_dot(h, wg_ref[:, cols]) for h in hs]
        us = [_dot(h, wu_ref[:, cols]) for h in hs]
        ps = [_dot((_silu(g) * u).astype(BF16), wd_ref[cols, :]) for g, u in zip(gs, us)]
        y2 = [p if y is None else y + p for y, p in zip(y2, ps)]
    for r, x1, y in zip(parts, x1s, y2):
        x2 = _residual_ln(x1, y, m1[2:3], ln1[0:1], ln1[1:2], alpha)
        outs[0][0, r, :] = x2
        if mod_next_ref is not None:
            mn = mod_next_ref[0]
            outs[1][0, r, :] = (x2 * (1.0 + mn[1:2]) + mn[0:1]).astype(BF16)


def _mix_dense_ln(o_h, o_s, x, w_out, wg, wu, wd, mod0, mod1, ln0, ln1, mod_next, alpha):
    bsz, s, d = x.shape
    ff = wg.shape[1]
    tm = min(s, 512)
    x_spec, mod_spec, ln_spec, out_specs, out_shape = _row_specs(bsz, s, d, tm, mod_next)
    once = pl.Buffered(1)
    in_specs = [pl.BlockSpec((1, tm, D_HGRN), lambda b, i: (b, i, 0)),
                pl.BlockSpec((D_SB // LANES, 1, tm, LANES), lambda b, i: (0, b, i, 0)),
                x_spec,
                pl.BlockSpec((d, d), lambda b, i: (0, 0), pipeline_mode=once),
                pl.BlockSpec((d, ff), lambda b, i: (0, 0), pipeline_mode=once),
                pl.BlockSpec((d, ff), lambda b, i: (0, 0), pipeline_mode=once),
                pl.BlockSpec((ff, d), lambda b, i: (0, 0), pipeline_mode=once),
                mod_spec, mod_spec, ln_spec, ln_spec]
    args = [o_h, o_s, x, w_out, wg, wu, wd, mod0, mod1, ln0, ln1]
    if mod_next is not None:
        in_specs.append(mod_spec)
        args.append(mod_next)
    n_chunks = 2 if (ff // 2) % LANES == 0 else 1
    return pl.pallas_call(
        functools.partial(_mix_dense_kernel, alpha, mod_next is not None, n_chunks),
        grid=(bsz, s // tm),
        in_specs=in_specs, out_specs=out_specs, out_shape=out_shape,
        compiler_params=_cparams(("parallel", "parallel")),
        name="out_proj_dense_ffn_ln",
    )(*args)


def _route(h, w_ref, tri_ref, run_ref):
    tm = h.shape[0]
    lane = lax.broadcasted_iota(jnp.int32, (tm, LANES), 1)
    logits = _dot(h, w_ref[...])
    neg = jnp.float32(-jnp.inf)
    logits = jnp.where(lane < N_EXPERTS, logits, neg)
    m1 = jnp.max(logits, axis=-1, keepdims=True)
    i1 = jnp.min(jnp.where(logits == m1, lane, LANES), axis=-1, keepdims=True)
    rest = jnp.where(lane == i1, neg, logits)
    m2 = jnp.max(rest, axis=-1, keepdims=True)
    i2 = jnp.min(jnp.where(rest == m2, lane, LANES), axis=-1, keepdims=True)
    e21 = jnp.exp(m2 - m1)
    w1 = 1.0 / (1.0 + e21)
    w2 = e21 / (1.0 + e21)

    hot1 = lane == i1
    hot2 = lane == i2
    both = jnp.where(jnp.logical_or(hot1, hot2), 1.0, 0.0)
    before = _dot(tri_ref[...], both.astype(BF16)) + run_ref[...]
    r1 = jnp.sum(jnp.where(hot1, before, 0.0), axis=-1, keepdims=True)
    r2 = jnp.sum(jnp.where(hot2, before, 0.0), axis=-1, keepdims=True)
    run_ref[...] = run_ref[...] + jnp.sum(both, axis=0, keepdims=True)

    cols = [i1.astype(F32), i2.astype(F32), r1, r2, w1, w2]
    route = jnp.zeros((tm, LANES), F32)
    for c, val in enumerate(cols):
        route = jnp.where(lane == c, val, route)
    return route[:, 0:8]


def _outproj_route_kernel(alpha, oh_ref, os_ref, x_ref, w_ref, mod_ref, ln_ref, modn_ref, wr_ref, tri_ref,
                          x_out, hp_out, route_out, count_out, run_ref):
    @pl.when(jnp.logical_and(pl.program_id(0) == 0, pl.program_id(1) == 0))
    def _():
        run_ref[...] = jnp.zeros_like(run_ref)

    y = _dot(oh_ref[0], w_ref[0:D_HGRN, :]) + _dot(_sb_lanes(os_ref), w_ref[D_HGRN:, :])
    ln = ln_ref[...]
    xn = _residual_ln(x_ref[0], y, mod_ref[0][2:3], ln[0:1], ln[1:2], alpha)
    x_out[0] = xn
    mn = modn_ref[0]
    hn = xn * (1.0 + mn[1:2]) + mn[0:1]
    _store_planes(hp_out, _pack_rows(hn))
    route_out[0] = _route(hn.astype(BF16), wr_ref, tri_ref, run_ref)
    count_out[...] = run_ref[...]


def _out_proj_route(o_h, o_s, x, w_out, mod, ln, mod_next, w_router, alpha):
    bsz, s, d = x.shape
    tm = min(s, 512)
    w_pad = jnp.zeros((d, LANES), BF16).at[:, :N_EXPERTS].set(w_router.astype(BF16))
    jj = jnp.arange(tm)
    tri = (jj[None, :] < jj[:, None]).astype(BF16)
    row = lambda w: pl.BlockSpec((1, tm, w), lambda b, i: (b, i, 0))
    mod_spec = pl.BlockSpec((1, 3, d), lambda b, i: (b, 0, 0))
    return pl.pallas_call(
        functools.partial(_outproj_route_kernel, alpha),
        grid=(bsz, s // tm),
        in_specs=[row(D_HGRN),
                  pl.BlockSpec((D_SB // LANES, 1, tm, LANES), lambda b, i: (0, b, i, 0)),
                  row(d),
                  pl.BlockSpec((d, d), lambda b, i: (0, 0), pipeline_mode=pl.Buffered(1)),
                  mod_spec,
                  pl.BlockSpec((2, d), lambda b, i: (0, 0)),
                  mod_spec,
                  pl.BlockSpec((d, LANES), lambda b, i: (0, 0)),
                  pl.BlockSpec((tm, tm), lambda b, i: (0, 0))],
        out_specs=[row(d),
                   pl.BlockSpec((2, 1, tm, d // 4), lambda b, i: (0, b, i, 0)),
                   row(8),
                   pl.BlockSpec((1, LANES), lambda b, i: (0, 0))],
        out_shape=[jax.ShapeDtypeStruct((bsz, s, d), F32),
                   jax.ShapeDtypeStruct((2, bsz, s, d // 4), U32),
                   jax.ShapeDtypeStruct((bsz, s, 8), F32),
                   jax.ShapeDtypeStruct((1, LANES), F32)],
        scratch_shapes=[pltpu.VMEM((1, LANES), F32)],
        compiler_params=_cparams(("arbitrary", "arbitrary")),
        name="out_proj_ln_route",
    )(o_h, o_s, x, w_out, mod, ln, mod_next, w_pad, tri)


def _gmm_kernel(te_ref, nt_ref, x_ref, wg_ref, wu_ref, wd_ref, o_ref, acc_ref, xb_ref):
    i = pl.program_id(0)
    j = pl.program_id(1)
    nj = pl.num_programs(1)
    live = i < nt_ref[0]

    @pl.when(jnp.logical_and(live, j == 0))
    def _():
        acc_ref[...] = jnp.zeros_like(acc_ref)
        xb_ref[...] = _load_planes(x_ref[0], x_ref[1]).astype(BF16)

    @pl.when(live)
    def _():
        x = xb_ref[...]
        g = _dot(x, wg_ref[0, 0].astype(BF16))
        u = _dot(x, wu_ref[0, 0].astype(BF16))
        acc_ref[...] += _dot((_silu(g) * u).astype(BF16), wd_ref[0, 0].astype(BF16))

    @pl.when(jnp.logical_and(live, j == nj - 1))
    def _():
        _store_planes(o_ref, _pack_rows(acc_ref[...]))

    @pl.when(jnp.logical_and(jnp.logical_not(live), j == nj - 1))
    def _():
        o_ref[...] = jnp.zeros_like(o_ref)


def _grouped_swiglu(xs, tile_expert, n_tiles_used, wg, wu, wd, layer):
    _, n_slots, pw = xs.shape
    d = 4 * pw
    ff = wg.shape[3]
    tm, tf = MOE_TILE, MOE_FF_TILE
    n_tiles = n_slots // tm

    def row_map(i, j, te, nt):
        return (0, jnp.minimum(i, nt[0] - 1), 0)

    def ff_block(i, j, te, nt):
        return jnp.where(i < nt[0], j, ff // tf - 1)

    grid_spec = pltpu.PrefetchScalarGridSpec(
        num_scalar_prefetch=2,
        grid=(n_tiles, ff // tf),
        in_specs=[pl.BlockSpec((2, tm, pw), row_map),
                  pl.BlockSpec((1, 1, d, tf), lambda i, j, te, nt: (layer, te[i], 0, ff_block(i, j, te, nt))),
                  pl.BlockSpec((1, 1, d, tf), lambda i, j, te, nt: (layer, te[i], 0, ff_block(i, j, te, nt))),
                  pl.BlockSpec((1, 1, tf, d), lambda i, j, te, nt: (layer, te[i], ff_block(i, j, te, nt), 0))],
        out_specs=pl.BlockSpec((2, tm, pw), lambda i, j, te, nt: (0, i, 0)),
        scratch_shapes=[pltpu.VMEM((tm, d), F32), pltpu.VMEM((tm, d), BF16)],
    )
    return pl.pallas_call(
        _gmm_kernel,
        grid_spec=grid_spec,
        out_shape=jax.ShapeDtypeStruct((2, n_slots, pw), U32),
        compiler_params=_cparams(("arbitrary", "arbitrary")),
        name="moe_grouped_swiglu",
    )(tile_expert, n_tiles_used, xs, wg, wu, wd)


def _combine_kernel(alpha, has_next, ya_ref, yb_ref, r_ref, x_ref, mod_ref, ln_ref, *rest):
    mod_next_ref = rest[0] if has_next else None
    outs = rest[1:] if has_next else rest
    r = r_ref[0]
    y = (r[:, 4:5] * _load_planes(ya_ref[0, 0, 0], ya_ref[0, 1, 0])
         + r[:, 5:6] * _load_planes(yb_ref[0, 0, 0], yb_ref[0, 1, 0]))
    ln = ln_ref[...]
    xn = _residual_ln(x_ref[0], y, mod_ref[0][2:3], ln[0:1], ln[1:2], alpha)
    _emit(xn, outs, mod_next_ref)


def _combine_ln(yab, route, x, mod, ln, mod_next, alpha):
    bsz, s, d = x.shape
    tm = min(s, 512)
    x_spec, mod_spec, ln_spec, out_specs, out_shape = _row_specs(bsz, s, d, tm, mod_next)
    yab = yab.reshape(2, 2, bsz, s, d // 4)
    in_specs = [pl.BlockSpec((1, 2, 1, tm, d // 4), lambda b, i: (0, 0, b, i, 0)),
                pl.BlockSpec((1, 2, 1, tm, d // 4), lambda b, i: (1, 0, b, i, 0)),
                pl.BlockSpec((1, tm, 8), lambda b, i: (b, i, 0)),
                x_spec, mod_spec, ln_spec]
    args = [yab, yab, route.reshape(bsz, s, 8), x, mod, ln]
    if mod_next is not None:
        in_specs.append(mod_spec)
        args.append(mod_next)
    return pl.pallas_call(
        functools.partial(_combine_kernel, alpha, mod_next is not None),
        grid=(bsz, s // tm),
        in_specs=in_specs, out_specs=out_specs, out_shape=out_shape,
        compiler_params=_cparams(("parallel", "parallel")),
        name="moe_combine_ln",
    )(*args)


def _sc_mesh():
    return plsc.VectorSubcoreMesh(core_axis_name="c", subcore_axis_name="s")


def _dispatch_rows(rows, idx, n_out):
    m, w = rows.shape

    @pl.kernel(out_type=jax.ShapeDtypeStruct((n_out, w), rows.dtype), mesh=_sc_mesh(), scratch_types=[])
    def scatter(rows_hbm, idx_hbm, out_hbm):
        def body(rows_vmem, idx0_vmem, idx1_vmem):
            pltpu.sync_copy(rows_vmem, out_hbm.at[idx0_vmem.at[0]])
            pltpu.sync_copy(rows_vmem, out_hbm.at[idx1_vmem.at[0]])

        pltpu.emit_pipeline(
            body, grid=(m // SC_WINDOW,),
            in_specs=[pl.BlockSpec((SC_WINDOW, w), lambda i: (i, 0)),
                      pl.BlockSpec((1, SC_WINDOW), lambda i: (0, i)),
                      pl.BlockSpec((1, SC_WINDOW), lambda i: (1, i))],
            out_specs=[],
            core_axis_name=("c", "s"),
            dimension_semantics=(pltpu.PARALLEL,),
        )(rows_hbm, idx_hbm, idx_hbm)

    return scatter(rows, idx)


def _collect_rows(rows, idx):
    m = idx.shape[1]
    w = rows.shape[1]

    @pl.kernel(out_type=jax.ShapeDtypeStruct((m, w), rows.dtype), mesh=_sc_mesh(), scratch_types=[])
    def gather(rows_hbm, idx_hbm, out_hbm):
        def body(idx_vmem, out_vmem):
            pltpu.sync_copy(rows_hbm.at[idx_vmem.at[0]], out_vmem)

        pltpu.emit_pipeline(
            body, grid=(m // SC_WINDOW,),
            in_specs=[pl.BlockSpec((1, SC_WINDOW), lambda i: (0, i))],
            out_specs=[pl.BlockSpec((SC_WINDOW, w), lambda i: (i, 0))],
            core_axis_name=("c", "s"),
            dimension_semantics=(pltpu.PARALLEL,),
        )(idx_hbm, out_hbm)

    return gather(rows, idx)


def _moe_ffn_ln(h, route, counts, x, wg, wu, wd, layer, mod, ln, mod_next, alpha):
    bsz, s, d = x.shape
    n = bsz * s
    hp = h.reshape(2, n, d // 4)
    route = route.reshape(n, 8)

    tm = MOE_TILE
    cnt = counts[0, :N_EXPERTS].astype(jnp.int32)
    padded = ((cnt + tm - 1) // tm) * tm
    ends = jnp.cumsum(padded)
    offs = ends - padded
    e = route[:, 0:2].astype(jnp.int32)
    rank = route[:, 2:4].astype(jnp.int32)
    pos = (jnp.take(offs, e) + rank).T
    n_tiles = (2 * n) // tm + N_EXPERTS
    n_slots = n_tiles * tm
    tile_start = jnp.arange(n_tiles, dtype=jnp.int32) * tm
    tile_expert = jnp.minimum(jnp.sum(tile_start[:, None] >= ends[None, :], axis=1),
                              N_EXPERTS - 1).astype(jnp.int32)
    n_used = (ends[-1] // tm).astype(jnp.int32).reshape(1)

    idx = jnp.concatenate([pos, pos + n_slots], axis=1)
    xs = _dispatch_rows(hp.reshape(2 * n, d // 4), idx, 2 * n_slots).reshape(2, n_slots, d // 4)
    ys = _grouped_swiglu(xs, tile_expert, n_used, wg, wu, wd, layer)
    yab = _collect_rows(ys.reshape(2 * n_slots, d // 4), idx.reshape(1, 4 * n)).reshape(2, 2, n, d // 4)
    return _combine_ln(yab, route, x, mod, ln, mod_next, alpha)


def kernel(x, c, w_ada, b_ada, w_in, w_out, hgrn_lb_logits, hgrn_norm_gain,
           w_dense_gate, w_dense_up, w_dense_down, w_router, w_moe_gate, w_moe_up, w_moe_down,
           ln_gain, ln_bias):
    depth = w_in.shape[0]
    bsz, s, d = x.shape
    alpha = (2 * depth) ** 0.25

    mods = _ada_all(c, w_ada, b_ada).reshape(depth, 2, bsz, 3, d)
    lb_all = _lower_bounds(hgrn_lb_logits)
    ln_all = jnp.stack([ln_gain, ln_bias], axis=2)

    h = x
    for layer in range(depth):
        proj_h, proj_s = _in_proj(h, w_in[layer].astype(BF16), mods[0, 0] if layer == 0 else None)
        o_h = _hgrn(proj_h.reshape(bsz, s, 4 * D_HGRN), lb_all[layer:layer + 1],
                    hgrn_norm_gain[layer:layer + 1])
        o_s = _stick_breaking(proj_s.reshape(-1, bsz, s, LANES))
        mod_next = mods[layer + 1, 0] if layer + 1 < depth else None
        j = layer // 2
        if layer % 2 == 0:
            res = _mix_dense_ln(o_h, o_s, x, w_out[layer].astype(BF16), w_dense_gate[j].astype(BF16),
                                w_dense_up[j].astype(BF16), w_dense_down[j].astype(BF16),
                                mods[layer, 0], mods[layer, 1], ln_all[layer, 0], ln_all[layer, 1],
                                mod_next, alpha)
        else:
            x, h, route, counts = _out_proj_route(o_h, o_s, x, w_out[layer].astype(BF16), mods[layer, 0],
                                                  ln_all[layer, 0], mods[layer, 1], w_router[j], alpha)
            res = _moe_ffn_ln(h, route, counts, x, w_moe_gate, w_moe_up, w_moe_down, j,
                              mods[layer, 1], ln_all[layer, 1], mod_next, alpha)
        if mod_next is not None:
            x, h = res
        else:
            (x,) = res
    return x
```

```python
import functools
import math

import jax
import jax.numpy as jnp
from jax import lax
from jax.experimental import pallas as pl
from jax.experimental.pallas import tpu as pltpu
from jax.experimental.pallas import tpu_sc as plsc

D_MODEL = 1024
D_HGRN = 512
HGRN_HEADS = 4
HGRN_DH = 128
D_SB = 512
SB_HEADS = 8
SB_DH = 64
D_IN_PROJ = 4 * D_HGRN + 3 * D_SB
N_EXPERTS = 8
LN_EPS = 1e-5
RMS_EPS = 1e-6

LOG2E = math.log2(math.e)
LANES = 128
VMEM_LIMIT = 52 * 1024 * 1024

HG_UNIT = 128
HG_BAND = 4
HG_PAD = 8
HG_LEVELS = (64, 32, 16, 8, 4)
SB_BLOCK = 128
SB_SKIP = 110.0 * LOG2E
SB_GROUP = 4
SB_WIN = 3
MOE_TILE = 1024
MOE_FF_TILE = 512
DENSE_SPLIT = 2
SC_WINDOW = 128

F32 = jnp.float32
BF16 = jnp.bfloat16
U32 = jnp.uint32


def _cparams(sem):
    return pltpu.CompilerParams(dimension_semantics=sem, vmem_limit_bytes=VMEM_LIMIT)


def _dot(a, b):
    return jnp.dot(a, b, preferred_element_type=F32)


def _dot_nt(a, b):
    return lax.dot_general(a, b, (((1,), (1,)), ((), ())), preferred_element_type=F32)


def _dot_tn(a, b):
    return lax.dot_general(a, b, (((0,), (0,)), ((), ())), preferred_element_type=F32)


def _pack_rows(v):
    w = v.shape[-1] // 2
    lo = lax.bitcast_convert_type(v[:, :w].astype(BF16).astype(F32), U32)
    hi = lax.bitcast_convert_type(v[:, w:].astype(BF16).astype(F32), U32)
    return (lo >> 16) | hi


def _unpack_rows(p):
    lo = lax.bitcast_convert_type(p << 16, F32)
    hi = lax.bitcast_convert_type(p & jnp.uint32(0xFFFF0000), F32)
    return jnp.concatenate([lo, hi], axis=-1)


def _store_planes(ref, p):
    w = p.shape[-1] // 2
    ref[0] = p[:, :w].reshape(ref.shape[1:])
    ref[1] = p[:, w:].reshape(ref.shape[1:])


def _load_planes(p0, p1):
    return _unpack_rows(jnp.concatenate([p0, p1], axis=-1))


def _sigmoid(x):
    return 1.0 / (1.0 + jnp.exp(-x))


def _silu(x):
    hx = 0.5 * x
    return hx + hx * jnp.tanh(hx)


def _ada_kernel(c_ref, w_ref, b_ref, o_ref):
    sc = _silu(c_ref[...])
    o_ref[0] = jnp.dot(sc, w_ref[0], preferred_element_type=F32,
                       precision=lax.Precision.HIGHEST) + b_ref[0]


def _ada_all(c, w_ada, b_ada):
    n_sub = w_ada.shape[0] * w_ada.shape[1]
    bsz, d = c.shape
    w = w_ada.reshape(n_sub, d, 3 * d)
    b = b_ada.reshape(n_sub, 1, 3 * d)
    tn = 1024
    return pl.pallas_call(
        _ada_kernel,
        grid=(n_sub, 3 * d // tn),
        in_specs=[pl.BlockSpec((bsz, d), lambda s, j: (0, 0)),
                  pl.BlockSpec((1, d, tn), lambda s, j: (s, 0, j)),
                  pl.BlockSpec((1, 1, tn), lambda s, j: (s, 0, j))],
        out_specs=pl.BlockSpec((1, bsz, tn), lambda s, j: (s, 0, j)),
        out_shape=jax.ShapeDtypeStruct((n_sub, bsz, 3 * d), F32),
        compiler_params=_cparams(("parallel", "parallel")),
        name="ada_modulation",
    )(c, w, b)


def _lb_kernel(l_ref, o_ref):
    x = l_ref[...]
    m = jnp.max(x, axis=0, keepdims=True)
    e = jnp.exp(x - m)
    p = e / jnp.sum(e, axis=0, keepdims=True)
    depth = x.shape[0]
    run = jnp.zeros_like(p[0:1])
    rows = []
    for i in range(depth):
        run = run + p[i:i + 1]
        rows.append(run - p[0:1])
    o_ref[...] = jnp.concatenate(rows, axis=0)


def _lower_bounds(lb_logits):
    return pl.pallas_call(
        _lb_kernel,
        out_shape=jax.ShapeDtypeStruct(lb_logits.shape, F32),
        name="hgrn_lower_bounds",
    )(lb_logits)


def _inproj_kernel(modulate, h_ref, *rest):
    if modulate:
        m_ref, w_ref, oh_ref, os_ref = rest
        m = m_ref[0]
        h = (h_ref[...] * (1.0 + m[1:2]) + m[0:1]).astype(BF16)
    else:
        w_ref, oh_ref, os_ref = rest
        h = h_ref[...]
    tn = 512
    nh = oh_ref.shape[1] // tn
    for j in range(nh):
        oh_ref[:, j * tn:(j + 1) * tn] = _dot(h, w_ref[:, j * tn:(j + 1) * tn]).astype(BF16)
    per = tn // LANES
    for j in range(os_ref.shape[0] // per):
        blk = _dot(h, w_ref[:, (nh + j) * tn:(nh + j + 1) * tn]).astype(BF16)
        for c in range(per):
            os_ref[j * per + c] = blk[:, c * LANES:(c + 1) * LANES]


def _in_proj(h, w_in, mod=None):
    bsz, s, d = h.shape
    n = bsz * s
    dout = w_in.shape[1]
    dh = 4 * D_HGRN
    n_slabs = (dout - dh) // LANES
    tm = min(s, 1024)
    per = s // tm
    in_specs = [pl.BlockSpec((tm, d), lambda b, i: (b * per + i, 0))]
    args = [h.reshape(n, d)]
    if mod is not None:
        in_specs.append(pl.BlockSpec((1, 3, d), lambda b, i: (b, 0, 0)))
        args.append(mod)
    in_specs.append(pl.BlockSpec((d, dout), lambda b, i: (0, 0), pipeline_mode=pl.Buffered(1)))
    args.append(w_in)
    return pl.pallas_call(
        functools.partial(_inproj_kernel, mod is not None),
        grid=(bsz, per),
        in_specs=in_specs,
        out_specs=[pl.BlockSpec((tm, dh), lambda b, i: (b * per + i, 0)),
                   pl.BlockSpec((n_slabs, tm, LANES), lambda b, i: (0, b * per + i, 0))],
        out_shape=[jax.ShapeDtypeStruct((n, dh), BF16),
                   jax.ShapeDtypeStruct((n_slabs, n, LANES), BF16)],
        compiler_params=_cparams(("parallel", "parallel")),
        name="in_proj",
    )(*args)


def _hgrn_consts():
    t = jnp.arange(HG_UNIT)[:, None]
    s = jnp.arange(HG_UNIT)[None, :]
    tril = (s <= t).astype(BF16)
    lv = jnp.full((HG_UNIT, HG_UNIT), -1, jnp.int32)
    for idx, m in enumerate(HG_LEVELS):
        blk = 2 * m
        hit = (t // blk == s // blk) & (t % blk >= m) & (s % blk < m)
        lv = jnp.where(hit, idx, lv)
    return tril, lv


def _hgrn_kernel(q_ref, f_ref, i_ref, g_ref, lb_ref, gain_ref, tril_ref, lv_ref, o_ref,
                 states_ref, kpads_ref, fpads_ref, vpads_ref):
    @pl.when(pl.program_id(1) == 0)
    def _():
        states_ref[...] = jnp.zeros_like(states_ref)

    heads = range(HGRN_HEADS)
    zpad = jnp.zeros((HGRN_HEADS, HG_PAD, HGRN_DH), F32)
    kpads_ref[:, 0:HG_PAD, :] = zpad
    fpads_ref[:, 0:HG_PAD, :] = zpad
    vpads_ref[:, 0:HG_PAD, :] = zpad

    n_units = q_ref.shape[1] // HG_UNIT
    rowi = lax.broadcasted_iota(jnp.int32, (HG_UNIT, HGRN_DH), 0)
    tmod = jnp.bitwise_and(rowi, HG_BAND - 1)

    def unit(u, carry):
        rows = pl.ds(pl.multiple_of(u * HG_UNIT, HG_UNIT), HG_UNIT)
        lanes = [slice(h * HGRN_DH, (h + 1) * HGRN_DH) for h in heads]

        fs, kk, qq, v, vb, b3 = [], [], [], [], [], []
        for h in heads:
            lb = lb_ref[:, lanes[h]]
            f = lb + (1.0 - lb) * _sigmoid(f_ref[0, rows, lanes[h]].astype(F32))
            logf = jnp.log2(f)
            fs.append(f)
            kk.append(1.0 - f)
            qq.append(_silu(q_ref[0, rows, lanes[h]].astype(F32)))
            v.append(i_ref[0, rows, lanes[h]].astype(F32))
            vb.append(i_ref[0, rows, lanes[h]])
            p0 = logf.astype(BF16)
            r1 = logf - p0.astype(F32)
            p1 = r1.astype(BF16)
            p2 = (r1 - p1.astype(F32)).astype(BF16)
            b3.append(_dot(tril_ref[...], jnp.concatenate([p0, p1, p2], axis=1)))
        b = [t[:, 0:HGRN_DH] + t[:, HGRN_DH:2 * HGRN_DH] + t[:, 2 * HGRN_DH:] for t in b3]

        state = [states_ref[h] for h in heads]
        o = [_dot_nt((qq[h] * jnp.exp2(b[h])).astype(BF16), state[h].astype(BF16)) for h in heads]

        lv = lv_ref[...]
        scores = [jnp.zeros((HG_UNIT, HG_UNIT), F32) for _ in heads]
        for idx, m in enumerate(HG_LEVELS):
            blk = 2 * m
            nb = HG_UNIT // blk
            upper = jnp.bitwise_and(rowi, m) != 0
            xl = []
            for h in heads:
                ref = b[h].reshape(nb, blk, HGRN_DH)[:, m - 1:m, :]
                ref = jnp.broadcast_to(ref, (nb, blk, HGRN_DH)).reshape(HG_UNIT, HGRN_DH)
                xl.append((jnp.where(upper, qq[h], kk[h]) * jnp.exp2(-jnp.abs(b[h] - ref))).astype(BF16))
            s_l = [_dot_nt(x, x) for x in xl]
            scores = [jnp.where(lv == idx, s_l[h], scores[h]) for h in heads]
        intra = [_dot(scores[h].astype(BF16), vb[h]) for h in heads]

        for h in heads:
            kpads_ref[h, HG_PAD:, :] = kk[h]
            fpads_ref[h, HG_PAD:, :] = fs[h]
            vpads_ref[h, HG_PAD:, :] = v[h]
        band = [jnp.sum(qq[h] * kk[h], axis=-1, keepdims=True) * v[h] for h in heads]
        dec = [None for _ in heads]
        for d in range(1, HG_BAND):
            win = pl.ds(HG_PAD - d, HG_UNIT)
            for h in heads:
                dec[h] = fs[h] if d == 1 else dec[h] * fpads_ref[h, pl.ds(HG_PAD - (d - 1), HG_UNIT), :]
                p = qq[h] * kpads_ref[h, win, :] * dec[h]
                s_d = jnp.sum(p, axis=-1, keepdims=True)
                band[h] = band[h] + jnp.where(tmod >= d, s_d * vpads_ref[h, win, :], 0.0)

        for h in heads:
            b_last = b[h][HG_UNIT - 1:HG_UNIT, :]
            kdec = kk[h] * jnp.exp2(b_last - b[h])
            states_ref[h] = state[h] * jnp.exp2(b_last) + _dot_tn(vb[h], kdec.astype(BF16))
        for h in heads:
            oh = o[h] + intra[h] + band[h]
            oh = oh * lax.rsqrt(jnp.mean(oh * oh, axis=-1, keepdims=True) + RMS_EPS)
            gg = g_ref[0, rows, lanes[h]].astype(F32)
            o_ref[0, rows, lanes[h]] = (oh * gain_ref[:, lanes[h]] * _silu(gg)).astype(BF16)
        return carry

    lax.fori_loop(0, n_units, unit, 0)


def _hgrn(proj, lb, gain):
    bsz, s, _ = proj.shape
    ts = min(s, 1024)
    tril, lv = _hgrn_consts()

    def col(off):
        return pl.BlockSpec((1, ts, D_HGRN), lambda b, i, off=off: (b, i, off))

    return pl.pallas_call(
        _hgrn_kernel,
        grid=(bsz, s // ts),
        in_specs=[col(0), col(1), col(2), col(3),
                  pl.BlockSpec((1, D_HGRN), lambda b, i: (0, 0)),
                  pl.BlockSpec((1, D_HGRN), lambda b, i: (0, 0)),
                  pl.BlockSpec((HG_UNIT, HG_UNIT), lambda b, i: (0, 0)),
                  pl.BlockSpec((HG_UNIT, HG_UNIT), lambda b, i: (0, 0))],
        out_specs=pl.BlockSpec((1, ts, D_HGRN), lambda b, i: (b, i, 0)),
        out_shape=jax.ShapeDtypeStruct((bsz, s, D_HGRN), BF16),
        scratch_shapes=[pltpu.VMEM((HGRN_HEADS, HGRN_DH, HGRN_DH), F32),
                        pltpu.VMEM((HGRN_HEADS, HG_UNIT + HG_PAD, HGRN_DH), F32),
                        pltpu.VMEM((HGRN_HEADS, HG_UNIT + HG_PAD, HGRN_DH), F32),
                        pltpu.VMEM((HGRN_HEADS, HG_UNIT + HG_PAD, HGRN_DH), F32)],
        compiler_params=_cparams(("parallel", "arbitrary")),
        name="hgrn2",
    )(proj, proj, proj, proj, lb, gain, tril, lv)


def _sb_kernel(q_ref, k_ref, v_ref, uo_ref, o_ref, kp_ref, vp_ref, c_ref, acc_ref, more_ref):
    tq = SB_BLOCK
    nq = q_ref.shape[2] // tq
    pad = (SB_WIN - 1) * tq
    kp_ref[0:pad, :] = jnp.zeros((pad, LANES), BF16)
    vp_ref[0:pad, :] = jnp.zeros((pad, LANES), BF16)
    kp_ref[pad:, :] = (k_ref[0, 0].astype(F32) * LOG2E).astype(BF16)
    vp_ref[pad:, :] = v_ref[0, 0]

    first = lax.broadcasted_iota(jnp.int32, (tq, LANES), 1) < SB_DH
    rowi = lax.broadcasted_iota(jnp.int32, (2 * tq, tq), 0)
    coli = lax.broadcasted_iota(jnp.int32, (2 * tq, tq), 1)
    strict = coli < jnp.bitwise_and(rowi, tq - 1)

    def stacked_q(qi):
        rows = pl.ds(pl.multiple_of(qi * tq, tq), tq)
        q = q_ref[0, 0, rows, :] * (SB_DH ** -0.5)
        zero = jnp.zeros_like(q)
        return rows, jnp.concatenate([jnp.where(first, q, zero), jnp.where(first, zero, q)], axis=0)

    def windows(items, diag, win):
        order = list(reversed(range(win)))
        zs, cts = [], []
        for q2, p_top, _, _ in items:
            start = pl.multiple_of((p_top - (win - 1)) * tq, tq)
            zs.append(_dot_nt(q2, kp_ref[pl.ds(start, win * tq), :]))
        for z in zs:
            logn = -(jnp.maximum(z, 0.0) + jnp.log2(1.0 + jnp.exp2(-jnp.abs(z))))
            lhs = []
            for j in order:
                lj = logn[:, j * tq:(j + 1) * tq]
                if diag and j == win - 1:
                    lj = jnp.where(strict, lj, 0.0)
                hi = lj.astype(BF16)
                lo = (lj - hi.astype(F32)).astype(BF16)
                lhs.append(jnp.concatenate([hi, lo], axis=1))
            cts.append(_dot(jnp.concatenate(lhs, axis=0), uo_ref[...]))
        outs = []
        for (q2, p_top, crow, acc), z, ct in zip(items, zs, cts):
            start = pl.multiple_of((p_top - (win - 1)) * tq, tq)
            parts = [None] * win
            for idx, j in enumerate(order):
                blk = ct[idx * 2 * tq:(idx + 1) * 2 * tq]
                a = jnp.exp2(z[:, j * tq:(j + 1) * tq] + (crow + blk))
                if diag and j == win - 1:
                    a = jnp.where(strict, a, 0.0)
                parts[j] = a.astype(BF16)
                crow = crow + jnp.broadcast_to(blk[:, 0:1], blk.shape)
            acc = acc + _dot(jnp.concatenate(parts, axis=1), vp_ref[pl.ds(start, win * tq), :])
            outs.append((crow, acc))
        return outs

    def emit(rows, acc):
        o_ref[0, 0, rows, :] = jnp.where(first, acc[0:tq], acc[tq:]).astype(BF16)

    def fast(g, carry):
        zero = jnp.zeros((2 * tq, LANES), F32)
        qis = [g * SB_GROUP + t for t in range(SB_GROUP)]
        prep = [stacked_q(qi) for qi in qis]
        res = windows([(q2, qi + SB_WIN - 1, zero, zero) for qi, (_, q2) in zip(qis, prep)], True, SB_WIN)
        for qi, (rows, _), (crow, acc) in zip(qis, prep, res):
            emit(rows, acc)
            c_ref[qi] = crow
            acc_ref[qi] = acc
            unfinished = jnp.logical_and(qi >= SB_WIN, jnp.max(crow) > -SB_SKIP)
            more_ref[qi] = unfinished.astype(jnp.int32)
        return carry

    lax.fori_loop(0, nq // SB_GROUP, fast, 0)

    def finish(qi, carry):
        @pl.when(more_ref[qi] != 0)
        def _():
            rows, q2 = stacked_q(qi)

            def cond(c):
                p_top, mx, _, _ = c
                return jnp.logical_and(p_top >= SB_WIN - 1, mx > -SB_SKIP)

            def body(c):
                p_top, _, crow, acc = c
                (crow, acc), = windows([(q2, p_top, crow, acc)], False, 1)
                return p_top - 1, jnp.max(crow), crow, acc

            crow0 = c_ref[qi]
            _, _, _, acc = lax.while_loop(cond, body, (qi - 1, jnp.max(crow0), crow0, acc_ref[qi]))
            emit(rows, acc)
        return carry

    lax.fori_loop(SB_WIN, nq, finish, 0)


def _stick_breaking(qkv):
    _, bsz, s, _ = qkv.shape
    tq = SB_BLOCK
    nq = s // tq
    assert nq % SB_GROUP == 0
    n_pairs = D_SB // LANES
    jj = jnp.arange(tq)
    suffix = (jj[:, None] >= jj[None, :]).astype(BF16)
    uo = jnp.concatenate([suffix, suffix], axis=0)

    def seq(off):
        return pl.BlockSpec((1, 1, s, LANES), lambda b, p, off=off: (off * n_pairs + p, b, 0, 0))

    return pl.pallas_call(
        _sb_kernel,
        grid=(bsz, n_pairs),
        in_specs=[seq(0), seq(1), seq(2),
                  pl.BlockSpec((2 * tq, tq), lambda b, p: (0, 0))],
        out_specs=pl.BlockSpec((1, 1, s, LANES), lambda b, p: (p, b, 0, 0)),
        out_shape=jax.ShapeDtypeStruct((n_pairs, bsz, s, LANES), BF16),
        scratch_shapes=[pltpu.VMEM((s + (SB_WIN - 1) * tq, LANES), BF16),
                        pltpu.VMEM((s + (SB_WIN - 1) * tq, LANES), BF16),
                        pltpu.VMEM((nq, 2 * tq, LANES), F32),
                        pltpu.VMEM((nq, 2 * tq, LANES), F32),
                        pltpu.SMEM((nq,), jnp.int32)],
        compiler_params=_cparams(("parallel", "parallel")),
        name="stick_breaking",
    )(qkv, qkv, qkv, uo)


def _residual_ln(x, y, gate, gain, bias, alpha):
    r = alpha * x + gate * y
    mu = jnp.mean(r, axis=-1, keepdims=True)
    rc = r - mu
    var = jnp.mean(rc * rc, axis=-1, keepdims=True)
    return rc * lax.rsqrt(var + LN_EPS) * gain + bias


def _sb_lanes(os_ref):
    return jnp.concatenate([os_ref[p, 0] for p in range(os_ref.shape[0])], axis=-1)


def _emit(xn, outs, mod_next_ref):
    outs[0][0] = xn
    if mod_next_ref is not None:
        mn = mod_next_ref[0]
        outs[1][0] = (xn * (1.0 + mn[1:2]) + mn[0:1]).astype(BF16)


def _row_specs(bsz, s, d, tm, mod_next):
    x_spec = pl.BlockSpec((1, tm, d), lambda b, i: (b, i, 0))
    mod_spec = pl.BlockSpec((1, 3, d), lambda b, i: (b, 0, 0))
    ln_spec = pl.BlockSpec((2, d), lambda b, i: (0, 0))
    out_specs = [pl.BlockSpec((1, tm, d), lambda b, i: (b, i, 0))]
    out_shape = [jax.ShapeDtypeStruct((bsz, s, d), F32)]
    if mod_next is not None:
        out_specs.append(pl.BlockSpec((1, tm, d), lambda b, i: (b, i, 0)))
        out_shape.append(jax.ShapeDtypeStruct((bsz, s, d), BF16))
    return x_spec, mod_spec, ln_spec, out_specs, out_shape


def _mix_dense_kernel(alpha, has_next, n_chunks, oh_ref, os_ref, x_ref, wo_ref, wg_ref, wu_ref, wd_ref,
                      mod0_ref, mod1_ref, ln0_ref, ln1_ref, *rest):
    mod_next_ref = rest[0] if has_next else None
    outs = rest[1:] if has_next else rest
    tm = x_ref.shape[1]
    parts = [slice(r * tm // DENSE_SPLIT, (r + 1) * tm // DENSE_SPLIT) for r in range(DENSE_SPLIT)]
    ln0 = ln0_ref[...]
    ln1 = ln1_ref[...]
    m1 = mod1_ref[0]
    ff = wg_ref.shape[1]
    tf = ff // n_chunks
    osb = [jnp.concatenate([os_ref[p, 0, r, :] for p in range(os_ref.shape[0])], axis=-1) for r in parts]
    ys = [_dot(oh_ref[0, r, :], wo_ref[0:D_HGRN, :]) + _dot(o, wo_ref[D_HGRN:, :]) for r, o in zip(parts, osb)]
    x1s = [_residual_ln(x_ref[0, r, :], y, mod0_ref[0][2:3], ln0[0:1], ln0[1:2], alpha)
           for r, y in zip(parts, ys)]
    hs = [(x1 * (1.0 + m1[1:2]) + m1[0:1]).astype(BF16) for x1 in x1s]
    y2 = [None] * len(parts)
    for j in range(n_chunks):
        cols = slice(j * tf, (j + 1) * tf)
        gs = [_dot(h, wg_ref[:, cols]) for h in hs]
        us = [_dot(h, wu_ref[:, cols]) for h in hs]
        ps = [_dot((_silu(g) * u).astype(BF16), wd_ref[cols, :]) for g, u in zip(gs, us)]
        y2 = [p if y is None else y + p for y, p in zip(y2, ps)]
    for r, x1, y in zip(parts, x1s, y2):
        x2 = _residual_ln(x1, y, m1[2:3], ln1[0:1], ln1[1:2], alpha)
        outs[0][0, r, :] = x2
        if mod_next_ref is not None:
            mn = mod_next_ref[0]
            outs[1][0, r, :] = (x2 * (1.0 + mn[1:2]) + mn[0:1]).astype(BF16)


def _mix_dense_ln(o_h, o_s, x, w_out, wg, wu, wd, mod0, mod1, ln0, ln1, mod_next, alpha):
    bsz, s, d = x.shape
    ff = wg.shape[1]
    tm = min(s, 512)
    x_spec, mod_spec, ln_spec, out_specs, out_shape = _row_specs(bsz, s, d, tm, mod_next)
    once = pl.Buffered(1)
    in_specs = [pl.BlockSpec((1, tm, D_HGRN), lambda b, i: (b, i, 0)),
                pl.BlockSpec((D_SB // LANES, 1, tm, LANES), lambda b, i: (0, b, i, 0)),
                x_spec,
                pl.BlockSpec((d, d), lambda b, i: (0, 0), pipeline_mode=once),
                pl.BlockSpec((d, ff), lambda b, i: (0, 0), pipeline_mode=once),
                pl.BlockSpec((d, ff), lambda b, i: (0, 0), pipeline_mode=once),
                pl.BlockSpec((ff, d), lambda b, i: (0, 0), pipeline_mode=once),
                mod_spec, mod_spec, ln_spec, ln_spec]
    args = [o_h, o_s, x, w_out, wg, wu, wd, mod0, mod1, ln0, ln1]
    if mod_next is not None:
        in_specs.append(mod_spec)
        args.append(mod_next)
    n_chunks = 2 if (ff // 2) % LANES == 0 else 1
    return pl.pallas_call(
        functools.partial(_mix_dense_kernel, alpha, mod_next is not None, n_chunks),
        grid=(bsz, s // tm),
        in_specs=in_specs, out_specs=out_specs, out_shape=out_shape,
        compiler_params=_cparams(("parallel", "parallel")),
        name="out_proj_dense_ffn_ln",
    )(*args)


def _route(h, w_ref, tri_ref, run_ref):
    tm = h.shape[0]
    lane = lax.broadcasted_iota(jnp.int32, (tm, LANES), 1)
    logits = _dot(h, w_ref[...])
    neg = jnp.float32(-jnp.inf)
    logits = jnp.where(lane < N_EXPERTS, logits, neg)
    m1 = jnp.max(logits, axis=-1, keepdims=True)
    i1 = jnp.min(jnp.where(logits == m1, lane, LANES), axis=-1, keepdims=True)
    rest = jnp.where(lane == i1, neg, logits)
    m2 = jnp.max(rest, axis=-1, keepdims=True)
    i2 = jnp.min(jnp.where(rest == m2, lane, LANES), axis=-1, keepdims=True)
    e21 = jnp.exp(m2 - m1)
    w1 = 1.0 / (1.0 + e21)
    w2 = e21 / (1.0 + e21)

    hot1 = lane == i1
    hot2 = lane == i2
    both = jnp.where(jnp.logical_or(hot1, hot2), 1.0, 0.0)
    before = _dot(tri_ref[...], both.astype(BF16)) + run_ref[...]
    r1 = jnp.sum(jnp.where(hot1, before, 0.0), axis=-1, keepdims=True)
    r2 = jnp.sum(jnp.where(hot2, before, 0.0), axis=-1, keepdims=True)
    run_ref[...] = run_ref[...] + jnp.sum(both, axis=0, keepdims=True)

    cols = [i1.astype(F32), i2.astype(F32), r1, r2, w1, w2]
    route = jnp.zeros((tm, LANES), F32)
    for c, val in enumerate(cols):
        route = jnp.where(lane == c, val, route)
    return route[:, 0:8]


def _outproj_route_kernel(alpha, oh_ref, os_ref, x_ref, w_ref, mod_ref, ln_ref, modn_ref, wr_ref, tri_ref,
                          x_out, hp_out, route_out, count_out, run_ref):
    @pl.when(jnp.logical_and(pl.program_id(0) == 0, pl.program_id(1) == 0))
    def _():
        run_ref[...] = jnp.zeros_like(run_ref)

    y = _dot(oh_ref[0], w_ref[0:D_HGRN, :]) + _dot(_sb_lanes(os_ref), w_ref[D_HGRN:, :])
    ln = ln_ref[...]
    xn = _residual_ln(x_ref[0], y, mod_ref[0][2:3], ln[0:1], ln[1:2], alpha)
    x_out[0] = xn
    mn = modn_ref[0]
    hn = xn * (1.0 + mn[1:2]) + mn[0:1]
    _store_planes(hp_out, _pack_rows(hn))
    route_out[0] = _route(hn.astype(BF16), wr_ref, tri_ref, run_ref)
    count_out[...] = run_ref[...]


def _out_proj_route(o_h, o_s, x, w_out, mod, ln, mod_next, w_router, alpha):
    bsz, s, d = x.shape
    tm = min(s, 512)
    w_pad = jnp.zeros((d, LANES), BF16).at[:, :N_EXPERTS].set(w_router.astype(BF16))
    jj = jnp.arange(tm)
    tri = (jj[None, :] < jj[:, None]).astype(BF16)
    row = lambda w: pl.BlockSpec((1, tm, w), lambda b, i: (b, i, 0))
    mod_spec = pl.BlockSpec((1, 3, d), lambda b, i: (b, 0, 0))
    return pl.pallas_call(
        functools.partial(_outproj_route_kernel, alpha),
        grid=(bsz, s // tm),
        in_specs=[row(D_HGRN),
                  pl.BlockSpec((D_SB // LANES, 1, tm, LANES), lambda b, i: (0, b, i, 0)),
                  row(d),
                  pl.BlockSpec((d, d), lambda b, i: (0, 0), pipeline_mode=pl.Buffered(1)),
                  mod_spec,
                  pl.BlockSpec((2, d), lambda b, i: (0, 0)),
                  mod_spec,
                  pl.BlockSpec((d, LANES), lambda b, i: (0, 0)),
                  pl.BlockSpec((tm, tm), lambda b, i: (0, 0))],
        out_specs=[row(d),
                   pl.BlockSpec((2, 1, tm, d // 4), lambda b, i: (0, b, i, 0)),
                   row(8),
                   pl.BlockSpec((1, LANES), lambda b, i: (0, 0))],
        out_shape=[jax.ShapeDtypeStruct((bsz, s, d), F32),
                   jax.ShapeDtypeStruct((2, bsz, s, d // 4), U32),
                   jax.ShapeDtypeStruct((bsz, s, 8), F32),
                   jax.ShapeDtypeStruct((1, LANES), F32)],
        scratch_shapes=[pltpu.VMEM((1, LANES), F32)],
        compiler_params=_cparams(("arbitrary", "arbitrary")),
        name="out_proj_ln_route",
    )(o_h, o_s, x, w_out, mod, ln, mod_next, w_pad, tri)


def _gmm_kernel(te_ref, nt_ref, x_ref, wg_ref, wu_ref, wd_ref, o_ref, acc_ref, xb_ref):
    i = pl.program_id(0)
    j = pl.program_id(1)
    nj = pl.num_programs(1)
    live = i < nt_ref[0]

    @pl.when(jnp.logical_and(live, j == 0))
    def _():
        acc_ref[...] = jnp.zeros_like(acc_ref)
        xb_ref[...] = _load_planes(x_ref[0], x_ref[1]).astype(BF16)

    @pl.when(live)
    def _():
        x = xb_ref[...]
        g = _dot(x, wg_ref[0, 0].astype(BF16))
        u = _dot(x, wu_ref[0, 0].astype(BF16))
        acc_ref[...] += _dot((_silu(g) * u).astype(BF16), wd_ref[0, 0].astype(BF16))

    @pl.when(jnp.logical_and(live, j == nj - 1))
    def _():
        _store_planes(o_ref, _pack_rows(acc_ref[...]))

    @pl.when(jnp.logical_and(jnp.logical_not(live), j == nj - 1))
    def _():
        o_ref[...] = jnp.zeros_like(o_ref)


def _grouped_swiglu(xs, tile_expert, n_tiles_used, wg, wu, wd, layer):
    _, n_slots, pw = xs.shape
    d = 4 * pw
    ff = wg.shape[3]
    tm, tf = MOE_TILE, MOE_FF_TILE
    n_tiles = n_slots // tm

    def row_map(i, j, te, nt):
        return (0, jnp.minimum(i, nt[0] - 1), 0)

    def ff_block(i, j, te, nt):
        return jnp.where(i < nt[0], j, ff // tf - 1)

    grid_spec = pltpu.PrefetchScalarGridSpec(
        num_scalar_prefetch=2,
        grid=(n_tiles, ff // tf),
        in_specs=[pl.BlockSpec((2, tm, pw), row_map),
                  pl.BlockSpec((1, 1, d, tf), lambda i, j, te, nt: (layer, te[i], 0, ff_block(i, j, te, nt))),
                  pl.BlockSpec((1, 1, d, tf), lambda i, j, te, nt: (layer, te[i], 0, ff_block(i, j, te, nt))),
                  pl.BlockSpec((1, 1, tf, d), lambda i, j, te, nt: (layer, te[i], ff_block(i, j, te, nt), 0))],
        out_specs=pl.BlockSpec((2, tm, pw), lambda i, j, te, nt: (0, i, 0)),
        scratch_shapes=[pltpu.VMEM((tm, d), F32), pltpu.VMEM((tm, d), BF16)],
    )
    return pl.pallas_call(
        _gmm_kernel,
        grid_spec=grid_spec,
        out_shape=jax.ShapeDtypeStruct((2, n_slots, pw), U32),
        compiler_params=_cparams(("arbitrary", "arbitrary")),
        name="moe_grouped_swiglu",
    )(tile_expert, n_tiles_used, xs, wg, wu, wd)


def _combine_kernel(alpha, has_next, ya_ref, yb_ref, r_ref, x_ref, mod_ref, ln_ref, *rest):
    mod_next_ref = rest[0] if has_next else None
    outs = rest[1:] if has_next else rest
    r = r_ref[0]
    y = (r[:, 4:5] * _load_planes(ya_ref[0, 0, 0], ya_ref[0, 1, 0])
         + r[:, 5:6] * _load_planes(yb_ref[0, 0, 0], yb_ref[0, 1, 0]))
    ln = ln_ref[...]
    xn = _residual_ln(x_ref[0], y, mod_ref[0][2:3], ln[0:1], ln[1:2], alpha)
    _emit(xn, outs, mod_next_ref)


def _combine_ln(yab, route, x, mod, ln, mod_next, alpha):
    bsz, s, d = x.shape
    tm = min(s, 512)
    x_spec, mod_spec, ln_spec, out_specs, out_shape = _row_specs(bsz, s, d, tm, mod_next)
    yab = yab.reshape(2, 2, bsz, s, d // 4)
    in_specs = [pl.BlockSpec((1, 2, 1, tm, d // 4), lambda b, i: (0, 0, b, i, 0)),
                pl.BlockSpec((1, 2, 1, tm, d // 4), lambda b, i: (1, 0, b, i, 0)),
                pl.BlockSpec((1, tm, 8), lambda b, i: (b, i, 0)),
                x_spec, mod_spec, ln_spec]
    args = [yab, yab, route.reshape(bsz, s, 8), x, mod, ln]
    if mod_next is not None:
        in_specs.append(mod_spec)
        args.append(mod_next)
    return pl.pallas_call(
        functools.partial(_combine_kernel, alpha, mod_next is not None),
        grid=(bsz, s // tm),
        in_specs=in_specs, out_specs=out_specs, out_shape=out_shape,
        compiler_params=_cparams(("parallel", "parallel")),
        name="moe_combine_ln",
    )(*args)


def _sc_mesh():
    return plsc.VectorSubcoreMesh(core_axis_name="c", subcore_axis_name="s")


def _dispatch_rows(rows, idx, n_out):
    m, w = rows.shape

    @pl.kernel(out_type=jax.ShapeDtypeStruct((n_out, w), rows.dtype), mesh=_sc_mesh(), scratch_types=[])
    def scatter(rows_hbm, idx_hbm, out_hbm):
        def body(rows_vmem, idx0_vmem, idx1_vmem):
            pltpu.sync_copy(rows_vmem, out_hbm.at[idx0_vmem.at[0]])
            pltpu.sync_copy(rows_vmem, out_hbm.at[idx1_vmem.at[0]])

        pltpu.emit_pipeline(
            body, grid=(m // SC_WINDOW,),
            in_specs=[pl.BlockSpec((SC_WINDOW, w), lambda i: (i, 0)),
                      pl.BlockSpec((1, SC_WINDOW), lambda i: (0, i)),
                      pl.BlockSpec((1, SC_WINDOW), lambda i: (1, i))],
            out_specs=[],
            core_axis_name=("c", "s"),
            dimension_semantics=(pltpu.PARALLEL,),
        )(rows_hbm, idx_hbm, idx_hbm)

    return scatter(rows, idx)


def _collect_rows(rows, idx):
    m = idx.shape[1]
    w = rows.shape[1]

    @pl.kernel(out_type=jax.ShapeDtypeStruct((m, w), rows.dtype), mesh=_sc_mesh(), scratch_types=[])
    def gather(rows_hbm, idx_hbm, out_hbm):
        def body(idx_vmem, out_vmem):
            pltpu.sync_copy(rows_hbm.at[idx_vmem.at[0]], out_vmem)

        pltpu.emit_pipeline(
            body, grid=(m // SC_WINDOW,),
            in_specs=[pl.BlockSpec((1, SC_WINDOW), lambda i: (0, i))],
            out_specs=[pl.BlockSpec((SC_WINDOW, w), lambda i: (i, 0))],
            core_axis_name=("c", "s"),
            dimension_semantics=(pltpu.PARALLEL,),
        )(idx_hbm, out_hbm)

    return gather(rows, idx)


def _moe_ffn_ln(h, route, counts, x, wg, wu, wd, layer, mod, ln, mod_next, alpha):
    bsz, s, d = x.shape
    n = bsz * s
    hp = h.reshape(2, n, d // 4)
    route = route.reshape(n, 8)

    tm = MOE_TILE
    cnt = counts[0, :N_EXPERTS].astype(jnp.int32)
    padded = ((cnt + tm - 1) // tm) * tm
    ends = jnp.cumsum(padded)
    offs = ends - padded
    e = route[:, 0:2].astype(jnp.int32)
    rank = route[:, 2:4].astype(jnp.int32)
    pos = (jnp.take(offs, e) + rank).T
    n_tiles = (2 * n) // tm + N_EXPERTS
    n_slots = n_tiles * tm
    tile_start = jnp.arange(n_tiles, dtype=jnp.int32) * tm
    tile_expert = jnp.minimum(jnp.sum(tile_start[:, None] >= ends[None, :], axis=1),
                              N_EXPERTS - 1).astype(jnp.int32)
    n_used = (ends[-1] // tm).astype(jnp.int32).reshape(1)

    idx = jnp.concatenate([pos, pos + n_slots], axis=1)
    xs = _dispatch_rows(hp.reshape(2 * n, d // 4), idx, 2 * n_slots).reshape(2, n_slots, d // 4)
    ys = _grouped_swiglu(xs, tile_expert, n_used, wg, wu, wd, layer)
    yab = _collect_rows(ys.reshape(2 * n_slots, d // 4), idx.reshape(1, 4 * n)).reshape(2, 2, n, d // 4)
    return _combine_ln(yab, route, x, mod, ln, mod_next, alpha)


def kernel(x, c, w_ada, b_ada, w_in, w_out, hgrn_lb_logits, hgrn_norm_gain,
           w_dense_gate, w_dense_up, w_dense_down, w_router, w_moe_gate, w_moe_up, w_moe_down,
           ln_gain, ln_bias):
    depth = w_in.shape[0]
    bsz, s, d = x.shape
    alpha = (2 * depth) ** 0.25

    mods = _ada_all(c, w_ada, b_ada).reshape(depth, 2, bsz, 3, d)
    lb_all = _lower_bounds(hgrn_lb_logits)
    ln_all = jnp.stack([ln_gain, ln_bias], axis=2)

    h = x
    for layer in range(depth):
        proj_h, proj_s = _in_proj(h, w_in[layer].astype(BF16), mods[0, 0] if layer == 0 else None)
        o_h = _hgrn(proj_h.reshape(bsz, s, 4 * D_HGRN), lb_all[layer:layer + 1],
                    hgrn_norm_gain[layer:layer + 1])
        o_s = _stick_breaking(proj_s.reshape(-1, bsz, s, LANES))
        mod_next = mods[layer + 1, 0] if layer + 1 < depth else None
        j = layer // 2
        if layer % 2 == 0:
            res = _mix_dense_ln(o_h, o_s, x, w_out[layer].astype(BF16), w_dense_gate[j].astype(BF16),
                                w_dense_up[j].astype(BF16), w_dense_down[j].astype(BF16),
                                mods[layer, 0], mods[layer, 1], ln_all[layer, 0], ln_all[layer, 1],
                                mod_next, alpha)
        else:
            x, h, route, counts = _out_proj_route(o_h, o_s, x, w_out[layer].astype(BF16), mods[layer, 0],
                                                  ln_all[layer, 0], mods[layer, 1], w_router[j], alpha)
            res = _moe_ffn_ln(h, route, counts, x, w_moe_gate, w_moe_up, w_moe_down, j,
                              mods[layer, 1], ln_all[layer, 1], mod_next, alpha)
        if mod_next is not None:
            x, h = res
        else:
            (x,) = res
    return x
```

```python
import functools
import math

import jax
import jax.numpy as jnp
from jax import lax
from jax.experimental import pallas as pl
from jax.experimental.pallas import tpu as pltpu
from jax.experimental.pallas import tpu_sc as plsc

D_MODEL = 1024
D_HGRN = 512
HGRN_HEADS = 4
HGRN_DH = 128
D_SB = 512
SB_HEADS = 8
SB_DH = 64
D_IN_PROJ = 4 * D_HGRN + 3 * D_SB
N_EXPERTS = 8
LN_EPS = 1e-5
RMS_EPS = 1e-6

LOG2E = math.log2(math.e)
LANES = 128
VMEM_LIMIT = 52 * 1024 * 1024

HG_UNIT = 128
HG_BAND = 4
HG_PAD = 8
HG_LEVELS = (64, 32, 16, 8, 4)
SB_BLOCK = 128
SB_SKIP = 110.0 * LOG2E
SB_GROUP = 8
SB_WIN = 3
MOE_TILE = 1024
MOE_FF_TILE = 512
DENSE_SPLIT = 2
SC_WINDOW = 128

F32 = jnp.float32
BF16 = jnp.bfloat16
U32 = jnp.uint32


def _cparams(sem):
    return pltpu.CompilerParams(dimension_semantics=sem, vmem_limit_bytes=VMEM_LIMIT)


def _dot(a, b):
    return jnp.dot(a, b, preferred_element_type=F32)


def _dot_nt(a, b):
    return lax.dot_general(a, b, (((1,), (1,)), ((), ())), preferred_element_type=F32)


def _dot_tn(a, b):
    return lax.dot_general(a, b, (((0,), (0,)), ((), ())), preferred_element_type=F32)


def _pack_rows(v):
    w = v.shape[-1] // 2
    lo = lax.bitcast_convert_type(v[:, :w].astype(BF16).astype(F32), U32)
    hi = lax.bitcast_convert_type(v[:, w:].astype(BF16).astype(F32), U32)
    return (lo >> 16) | hi


def _unpack_rows(p):
    lo = lax.bitcast_convert_type(p << 16, F32)
    hi = lax.bitcast_convert_type(p & jnp.uint32(0xFFFF0000), F32)
    return jnp.concatenate([lo, hi], axis=-1)


def _store_planes(ref, p):
    w = p.shape[-1] // 2
    ref[0] = p[:, :w].reshape(ref.shape[1:])
    ref[1] = p[:, w:].reshape(ref.shape[1:])


def _load_planes(p0, p1):
    return _unpack_rows(jnp.concatenate([p0, p1], axis=-1))


def _sigmoid(x):
    return 1.0 / (1.0 + jnp.exp(-x))


def _silu(x):
    hx = 0.5 * x
    return hx + hx * jnp.tanh(hx)


def _ada_kernel(c_ref, w_ref, b_ref, o_ref):
    sc = _silu(c_ref[...])
    o_ref[0] = jnp.dot(sc, w_ref[0], preferred_element_type=F32,
                       precision=lax.Precision.HIGHEST) + b_ref[0]


def _ada_all(c, w_ada, b_ada):
    n_sub = w_ada.shape[0] * w_ada.shape[1]
    bsz, d = c.shape
    w = w_ada.reshape(n_sub, d, 3 * d)
    b = b_ada.reshape(n_sub, 1, 3 * d)
    tn = 1024
    return pl.pallas_call(
        _ada_kernel,
        grid=(n_sub, 3 * d // tn),
        in_specs=[pl.BlockSpec((bsz, d), lambda s, j: (0, 0)),
                  pl.BlockSpec((1, d, tn), lambda s, j: (s, 0, j)),
                  pl.BlockSpec((1, 1, tn), lambda s, j: (s, 0, j))],
        out_specs=pl.BlockSpec((1, bsz, tn), lambda s, j: (s, 0, j)),
        out_shape=jax.ShapeDtypeStruct((n_sub, bsz, 3 * d), F32),
        compiler_params=_cparams(("parallel", "parallel")),
        name="ada_modulation",
    )(c, w, b)


def _lb_kernel(l_ref, o_ref):
    x = l_ref[...]
    m = jnp.max(x, axis=0, keepdims=True)
    e = jnp.exp(x - m)
    p = e / jnp.sum(e, axis=0, keepdims=True)
    depth = x.shape[0]
    run = jnp.zeros_like(p[0:1])
    rows = []
    for i in range(depth):
        run = run + p[i:i + 1]
        rows.append(run - p[0:1])
    o_ref[...] = jnp.concatenate(rows, axis=0)


def _lower_bounds(lb_logits):
    return pl.pallas_call(
        _lb_kernel,
        out_shape=jax.ShapeDtypeStruct(lb_logits.shape, F32),
        name="hgrn_lower_bounds",
    )(lb_logits)


def _inproj_kernel(modulate, h_ref, *rest):
    if modulate:
        m_ref, w_ref, oh_ref, os_ref = rest
        m = m_ref[0]
        h = (h_ref[...] * (1.0 + m[1:2]) + m[0:1]).astype(BF16)
    else:
        w_ref, oh_ref, os_ref = rest
        h = h_ref[...]
    tn = 512
    nh = oh_ref.shape[0]
    for j in range(nh):
        oh_ref[j] = _dot(h, w_ref[:, j * tn:(j + 1) * tn]).astype(BF16)
    per = tn // LANES
    for j in range(os_ref.shape[0] // per):
        blk = _dot(h, w_ref[:, (nh + j) * tn:(nh + j + 1) * tn]).astype(BF16)
        for c in range(per):
            os_ref[j * per + c] = blk[:, c * LANES:(c + 1) * LANES]


def _in_proj(h, w_in, mod=None):
    bsz, s, d = h.shape
    n = bsz * s
    dout = w_in.shape[1]
    dh = 4 * D_HGRN
    n_slabs = (dout - dh) // LANES
    tm = min(s, 1024)
    per = s // tm
    in_specs = [pl.BlockSpec((tm, d), lambda b, i: (b * per + i, 0))]
    args = [h.reshape(n, d)]
    if mod is not None:
        in_specs.append(pl.BlockSpec((1, 3, d), lambda b, i: (b, 0, 0)))
        args.append(mod)
    in_specs.append(pl.BlockSpec((d, dout), lambda b, i: (0, 0), pipeline_mode=pl.Buffered(1)))
    args.append(w_in)
    return pl.pallas_call(
        functools.partial(_inproj_kernel, mod is not None),
        grid=(bsz, per),
        in_specs=in_specs,
        out_specs=[pl.BlockSpec((4, tm, D_HGRN), lambda b, i: (0, b * per + i, 0)),
                   pl.BlockSpec((n_slabs, tm, LANES), lambda b, i: (0, b * per + i, 0))],
        out_shape=[jax.ShapeDtypeStruct((4, n, D_HGRN), BF16),
                   jax.ShapeDtypeStruct((n_slabs, n, LANES), BF16)],
        compiler_params=_cparams(("parallel", "parallel")),
        name="in_proj",
    )(*args)


def _hgrn_consts():
    t = jnp.arange(HG_UNIT)[:, None]
    s = jnp.arange(HG_UNIT)[None, :]
    tril = (s <= t).astype(BF16)
    lv = jnp.full((HG_UNIT, HG_UNIT), -1, jnp.int32)
    for idx, m in enumerate(HG_LEVELS):
        blk = 2 * m
        hit = (t // blk == s // blk) & (t % blk >= m) & (s % blk < m)
        lv = jnp.where(hit, idx, lv)
    return tril, lv


def _hgrn_kernel(q_ref, f_ref, i_ref, g_ref, lb_ref, gain_ref, tril_ref, lv_ref, o_ref,
                 states_ref, kpads_ref, fpads_ref, vpads_ref):
    @pl.when(pl.program_id(1) == 0)
    def _():
        states_ref[...] = jnp.zeros_like(states_ref)

    heads = range(HGRN_HEADS)
    zpad = jnp.zeros((HGRN_HEADS, HG_PAD, HGRN_DH), F32)
    kpads_ref[:, 0:HG_PAD, :] = zpad
    fpads_ref[:, 0:HG_PAD, :] = zpad
    vpads_ref[:, 0:HG_PAD, :] = zpad

    n_units = q_ref.shape[2] // HG_UNIT
    rowi = lax.broadcasted_iota(jnp.int32, (HG_UNIT, HGRN_DH), 0)
    tmod = jnp.bitwise_and(rowi, HG_BAND - 1)

    def unit(u, carry):
        rows = pl.ds(pl.multiple_of(u * HG_UNIT, HG_UNIT), HG_UNIT)
        lanes = [slice(h * HGRN_DH, (h + 1) * HGRN_DH) for h in heads]

        fs, kk, qq, v, vb, b3 = [], [], [], [], [], []
        for h in heads:
            lb = lb_ref[:, lanes[h]]
            f = lb + (1.0 - lb) * _sigmoid(f_ref[0, 0, rows, lanes[h]].astype(F32))
            logf = jnp.log2(f)
            fs.append(f)
            kk.append(1.0 - f)
            qq.append(_silu(q_ref[0, 0, rows, lanes[h]].astype(F32)))
            v.append(i_ref[0, 0, rows, lanes[h]].astype(F32))
            vb.append(i_ref[0, 0, rows, lanes[h]])
            p0 = logf.astype(BF16)
            r1 = logf - p0.astype(F32)
            p1 = r1.astype(BF16)
            p2 = (r1 - p1.astype(F32)).astype(BF16)
            b3.append(_dot(tril_ref[...], jnp.concatenate([p0, p1, p2], axis=1)))
        b = [t[:, 0:HGRN_DH] + t[:, HGRN_DH:2 * HGRN_DH] + t[:, 2 * HGRN_DH:] for t in b3]

        state = [states_ref[h] for h in heads]
        o = [_dot_nt((qq[h] * jnp.exp2(b[h])).astype(BF16), state[h].astype(BF16)) for h in heads]

        lv = lv_ref[...]
        scores = [jnp.zeros((HG_UNIT, HG_UNIT), F32) for _ in heads]
        for idx, m in enumerate(HG_LEVELS):
            blk = 2 * m
            nb = HG_UNIT // blk
            upper = jnp.bitwise_and(rowi, m) != 0
            xl = []
            for h in heads:
                ref = b[h].reshape(nb, blk, HGRN_DH)[:, m - 1:m, :]
                ref = jnp.broadcast_to(ref, (nb, blk, HGRN_DH)).reshape(HG_UNIT, HGRN_DH)
                xl.append((jnp.where(upper, qq[h], kk[h]) * jnp.exp2(-jnp.abs(b[h] - ref))).astype(BF16))
            s_l = [_dot_nt(x, x) for x in xl]
            scores = [jnp.where(lv == idx, s_l[h], scores[h]) for h in heads]
        intra = [_dot(scores[h].astype(BF16), vb[h]) for h in heads]

        for h in heads:
            kpads_ref[h, HG_PAD:, :] = kk[h]
            fpads_ref[h, HG_PAD:, :] = fs[h]
            vpads_ref[h, HG_PAD:, :] = v[h]
        band = [jnp.sum(qq[h] * kk[h], axis=-1, keepdims=True) * v[h] for h in heads]
        dec = [None for _ in heads]
        for d in range(1, HG_BAND):
            win = pl.ds(HG_PAD - d, HG_UNIT)
            for h in heads:
                dec[h] = fs[h] if d == 1 else dec[h] * fpads_ref[h, pl.ds(HG_PAD - (d - 1), HG_UNIT), :]
                p = qq[h] * kpads_ref[h, win, :] * dec[h]
                s_d = jnp.sum(p, axis=-1, keepdims=True)
                band[h] = band[h] + jnp.where(tmod >= d, s_d * vpads_ref[h, win, :], 0.0)

        for h in heads:
            b_last = b[h][HG_UNIT - 1:HG_UNIT, :]
            kdec = kk[h] * jnp.exp2(b_last - b[h])
            states_ref[h] = state[h] * jnp.exp2(b_last) + _dot_tn(vb[h], kdec.astype(BF16))
        for h in heads:
            oh = o[h] + intra[h] + band[h]
            oh = oh * lax.rsqrt(jnp.mean(oh * oh, axis=-1, keepdims=True) + RMS_EPS)
            gg = g_ref[0, 0, rows, lanes[h]].astype(F32)
            o_ref[0, rows, lanes[h]] = (oh * gain_ref[:, lanes[h]] * _silu(gg)).astype(BF16)
        return carry

    lax.fori_loop(0, n_units, unit, 0)


def _hgrn(proj, lb, gain):
    _, bsz, s, _ = proj.shape
    ts = min(s, 1024)
    tril, lv = _hgrn_consts()

    def col(off):
        return pl.BlockSpec((1, 1, ts, D_HGRN), lambda b, i, off=off: (off, b, i, 0))

    return pl.pallas_call(
        _hgrn_kernel,
        grid=(bsz, s // ts),
        in_specs=[col(0), col(1), col(2), col(3),
                  pl.BlockSpec((1, D_HGRN), lambda b, i: (0, 0)),
                  pl.BlockSpec((1, D_HGRN), lambda b, i: (0, 0)),
                  pl.BlockSpec((HG_UNIT, HG_UNIT), lambda b, i: (0, 0)),
                  pl.BlockSpec((HG_UNIT, HG_UNIT), lambda b, i: (0, 0))],
        out_specs=pl.BlockSpec((1, ts, D_HGRN), lambda b, i: (b, i, 0)),
        out_shape=jax.ShapeDtypeStruct((bsz, s, D_HGRN), BF16),
        scratch_shapes=[pltpu.VMEM((HGRN_HEADS, HGRN_DH, HGRN_DH), F32),
                        pltpu.VMEM((HGRN_HEADS, HG_UNIT + HG_PAD, HGRN_DH), F32),
                        pltpu.VMEM((HGRN_HEADS, HG_UNIT + HG_PAD, HGRN_DH), F32),
                        pltpu.VMEM((HGRN_HEADS, HG_UNIT + HG_PAD, HGRN_DH), F32)],
        compiler_params=_cparams(("parallel", "arbitrary")),
        name="hgrn2",
    )(proj, proj, proj, proj, lb, gain, tril, lv)


def _sb_kernel(q_ref, k_ref, v_ref, uo_ref, o_ref, kp_ref, vp_ref, c_ref, acc_ref, more_ref):
    tq = SB_BLOCK
    nq = q_ref.shape[2] // tq
    pad = (SB_WIN - 1) * tq
    kp_ref[0:pad, :] = jnp.zeros((pad, LANES), BF16)
    vp_ref[0:pad, :] = jnp.zeros((pad, LANES), BF16)
    kp_ref[pad:, :] = (k_ref[0, 0].astype(F32) * LOG2E).astype(BF16)
    vp_ref[pad:, :] = v_ref[0, 0]

    first = lax.broadcasted_iota(jnp.int32, (tq, LANES), 1) < SB_DH
    rowi = lax.broadcasted_iota(jnp.int32, (2 * tq, tq), 0)
    coli = lax.broadcasted_iota(jnp.int32, (2 * tq, tq), 1)
    strict = coli < jnp.bitwise_and(rowi, tq - 1)

    def stacked_q(qi):
        rows = pl.ds(pl.multiple_of(qi * tq, tq), tq)
        q = q_ref[0, 0, rows, :] * (SB_DH ** -0.5)
        zero = jnp.zeros_like(q)
        return rows, jnp.concatenate([jnp.where(first, q, zero), jnp.where(first, zero, q)], axis=0)

    def windows(items, diag, win):
        order = list(reversed(range(win)))
        zs, cts = [], []
        for q2, p_top, _, _ in items:
            start = pl.multiple_of((p_top - (win - 1)) * tq, tq)
            zs.append(_dot_nt(q2, kp_ref[pl.ds(start, win * tq), :]))
        for z in zs:
            logn = -(jnp.maximum(z, 0.0) + jnp.log2(1.0 + jnp.exp2(-jnp.abs(z))))
            lhs = []
            for j in order:
                lj = logn[:, j * tq:(j + 1) * tq]
                if diag and j == win - 1:
                    lj = jnp.where(strict, lj, 0.0)
                hi = lj.astype(BF16)
                lo = (lj - hi.astype(F32)).astype(BF16)
                lhs.append(jnp.concatenate([hi, lo], axis=1))
            cts.append(_dot(jnp.concatenate(lhs, axis=0), uo_ref[...]))
        outs = []
        for (q2, p_top, crow, acc), z, ct in zip(items, zs, cts):
            start = pl.multiple_of((p_top - (win - 1)) * tq, tq)
            parts = [None] * win
            for idx, j in enumerate(order):
                blk = ct[idx * 2 * tq:(idx + 1) * 2 * tq]
                a = jnp.exp2(z[:, j * tq:(j + 1) * tq] + (crow + blk))
                if diag and j == win - 1:
                    a = jnp.where(strict, a, 0.0)
                parts[j] = a.astype(BF16)
                crow = crow + jnp.broadcast_to(blk[:, 0:1], blk.shape)
            acc = acc + _dot(jnp.concatenate(parts, axis=1), vp_ref[pl.ds(start, win * tq), :])
            outs.append((crow, acc))
        return outs

    def emit(rows, acc):
        o_ref[0, 0, rows, :] = jnp.where(first, acc[0:tq], acc[tq:]).astype(BF16)

    def fast(g, carry):
        zero = jnp.zeros((2 * tq, LANES), F32)
        qis = [g * SB_GROUP + t for t in range(SB_GROUP)]
        prep = [stacked_q(qi) for qi in qis]
        res = windows([(q2, qi + SB_WIN - 1, zero, zero) for qi, (_, q2) in zip(qis, prep)], True, SB_WIN)
        for qi, (rows, _), (crow, acc) in zip(qis, prep, res):
            emit(rows, acc)
            c_ref[qi] = crow
            acc_ref[qi] = acc
            unfinished = jnp.logical_and(qi >= SB_WIN, jnp.max(crow) > -SB_SKIP)
            more_ref[qi] = unfinished.astype(jnp.int32)
        return carry

    lax.fori_loop(0, nq // SB_GROUP, fast, 0)

    def finish(qi, carry):
        @pl.when(more_ref[qi] != 0)
        def _():
            rows, q2 = stacked_q(qi)

            def cond(c):
                p_top, mx, _, _ = c
                return jnp.logical_and(p_top >= SB_WIN - 1, mx > -SB_SKIP)

            def body(c):
                p_top, _, crow, acc = c
                (crow, acc), = windows([(q2, p_top, crow, acc)], False, 1)
                return p_top - 1, jnp.max(crow), crow, acc

            crow0 = c_ref[qi]
            _, _, _, acc = lax.while_loop(cond, body, (qi - 1, jnp.max(crow0), crow0, acc_ref[qi]))
            emit(rows, acc)
        return carry

    lax.fori_loop(SB_WIN, nq, finish, 0)


def _stick_breaking(qkv):
    _, bsz, s, _ = qkv.shape
    tq = SB_BLOCK
    nq = s // tq
    assert nq % SB_GROUP == 0
    n_pairs = D_SB // LANES
    jj = jnp.arange(tq)
    suffix = (jj[:, None] >= jj[None, :]).astype(BF16)
    uo = jnp.concatenate([suffix, suffix], axis=0)

    def seq(off):
        return pl.BlockSpec((1, 1, s, LANES), lambda b, p, off=off: (off * n_pairs + p, b, 0, 0))

    return pl.pallas_call(
        _sb_kernel,
        grid=(bsz, n_pairs),
        in_specs=[seq(0), seq(1), seq(2),
                  pl.BlockSpec((2 * tq, tq), lambda b, p: (0, 0))],
        out_specs=pl.BlockSpec((1, 1, s, LANES), lambda b, p: (p, b, 0, 0)),
        out_shape=jax.ShapeDtypeStruct((n_pairs, bsz, s, LANES), BF16),
        scratch_shapes=[pltpu.VMEM((s + (SB_WIN - 1) * tq, LANES), BF16),
                        pltpu.VMEM((s + (SB_WIN - 1) * tq, LANES), BF16),
                        pltpu.VMEM((nq, 2 * tq, LANES), F32),
                        pltpu.VMEM((nq, 2 * tq, LANES), F32),
                        pltpu.SMEM((nq,), jnp.int32)],
        compiler_params=_cparams(("parallel", "parallel")),
        name="stick_breaking",
    )(qkv, qkv, qkv, uo)


def _residual_ln(x, y, gate, gain, bias, alpha):
    r = alpha * x + gate * y
    mu = jnp.mean(r, axis=-1, keepdims=True)
    rc = r - mu
    var = jnp.mean(rc * rc, axis=-1, keepdims=True)
    return rc * lax.rsqrt(var + LN_EPS) * gain + bias


def _sb_lanes(os_ref):
    return jnp.concatenate([os_ref[p, 0] for p in range(os_ref.shape[0])], axis=-1)


def _emit(xn, outs, mod_next_ref):
    outs[0][0] = xn
    if mod_next_ref is not None:
        mn = mod_next_ref[0]
        outs[1][0] = (xn * (1.0 + mn[1:2]) + mn[0:1]).astype(BF16)


def _row_specs(bsz, s, d, tm, mod_next):
    x_spec = pl.BlockSpec((1, tm, d), lambda b, i: (b, i, 0))
    mod_spec = pl.BlockSpec((1, 3, d), lambda b, i: (b, 0, 0))
    ln_spec = pl.BlockSpec((2, d), lambda b, i: (0, 0))
    out_specs = [pl.BlockSpec((1, tm, d), lambda b, i: (b, i, 0))]
    out_shape = [jax.ShapeDtypeStruct((bsz, s, d), F32)]
    if mod_next is not None:
        out_specs.append(pl.BlockSpec((1, tm, d), lambda b, i: (b, i, 0)))
        out_shape.append(jax.ShapeDtypeStruct((bsz, s, d), BF16))
    return x_spec, mod_spec, ln_spec, out_specs, out_shape


def _mix_dense_kernel(alpha, has_next, n_chunks, oh_ref, os_ref, x_ref, wo_ref, wg_ref, wu_ref, wd_ref,
                      mod0_ref, mod1_ref, ln0_ref, ln1_ref, *rest):
    mod_next_ref = rest[0] if has_next else None
    outs = rest[1:] if has_next else rest
    tm = x_ref.shape[1]
    parts = [slice(r * tm // DENSE_SPLIT, (r + 1) * tm // DENSE_SPLIT) for r in range(DENSE_SPLIT)]
    ln0 = ln0_ref[...]
    ln1 = ln1_ref[...]
    m1 = mod1_ref[0]
    ff = wg_ref.shape[1]
    tf = ff // n_chunks
    osb = [jnp.concatenate([os_ref[p, 0, r, :] for p in range(os_ref.shape[0])], axis=-1) for r in parts]
    ys = [_dot(oh_ref[0, r, :], wo_ref[0:D_HGRN, :]) + _dot(o, wo_ref[D_HGRN:, :]) for r, o in zip(parts, osb)]
    x1s = [_residual_ln(x_ref[0, r, :], y, mod0_ref[0][2:3], ln0[0:1], ln0[1:2], alpha)
           for r, y in zip(parts, ys)]
    hs = [(x1 * (1.0 + m1[1:2]) + m1[0:1]).astype(BF16) for x1 in x1s]
    y2 = [None] * len(parts)
    for j in range(n_chunks):
        cols = slice(j * tf, (j + 1) * tf)
        gs = [_dot(h, wg_ref[:, cols]) for h in hs]
        us = [_dot(h, wu_ref[:, cols]) for h in hs]
        ps = [_dot((_silu(g) * u).astype(BF16), wd_ref[cols, :]) for g, u in zip(gs, us)]
        y2 = [p if y is None else y + p for y, p in zip(y2, ps)]
    for r, x1, y in zip(parts, x1s, y2):
        x2 = _residual_ln(x1, y, m1[2:3], ln1[0:1], ln1[1:2], alpha)
        outs[0][0, r, :] = x2
        if mod_next_ref is not None:
            mn = mod_next_ref[0]
            outs[1][0, r, :] = (x2 * (1.0 + mn[1:2]) + mn[0:1]).astype(BF16)


def _mix_dense_ln(o_h, o_s, x, w_out, wg, wu, wd, mod0, mod1, ln0, ln1, mod_next, alpha):
    bsz, s, d = x.shape
    ff = wg.shape[1]
    tm = min(s, 512)
    x_spec, mod_spec, ln_spec, out_specs, out_shape = _row_specs(bsz, s, d, tm, mod_next)
    once = pl.Buffered(1)
    in_specs = [pl.BlockSpec((1, tm, D_HGRN), lambda b, i: (b, i, 0)),
                pl.BlockSpec((D_SB // LANES, 1, tm, LANES), lambda b, i: (0, b, i, 0)),
                x_spec,
                pl.BlockSpec((d, d), lambda b, i: (0, 0), pipeline_mode=once),
                pl.BlockSpec((d, ff), lambda b, i: (0, 0), pipeline_mode=once),
                pl.BlockSpec((d, ff), lambda b, i: (0, 0), pipeline_mode=once),
                pl.BlockSpec((ff, d), lambda b, i: (0, 0), pipeline_mode=once),
                mod_spec, mod_spec, ln_spec, ln_spec]
    args = [o_h, o_s, x, w_out, wg, wu, wd, mod0, mod1, ln0, ln1]
    if mod_next is not None:
        in_specs.append(mod_spec)
        args.append(mod_next)
    n_chunks = 2 if (ff // 2) % LANES == 0 else 1
    return pl.pallas_call(
        functools.partial(_mix_dense_kernel, alpha, mod_next is not None, n_chunks),
        grid=(bsz, s // tm),
        in_specs=in_specs, out_specs=out_specs, out_shape=out_shape,
        compiler_params=_cparams(("parallel", "parallel")),
        name="out_proj_dense_ffn_ln",
    )(*args)


def _route(h, w_ref, tri_ref, run_ref):
    tm = h.shape[0]
    lane = lax.broadcasted_iota(jnp.int32, (tm, LANES), 1)
    logits = _dot(h, w_ref[...])
    neg = jnp.float32(-jnp.inf)
    logits = jnp.where(lane < N_EXPERTS, logits, neg)
    m1 = jnp.max(logits, axis=-1, keepdims=True)
    i1 = jnp.min(jnp.where(logits == m1, lane, LANES), axis=-1, keepdims=True)
    rest = jnp.where(lane == i1, neg, logits)
    m2 = jnp.max(rest, axis=-1, keepdims=True)
    i2 = jnp.min(jnp.where(rest == m2, lane, LANES), axis=-1, keepdims=True)
    e21 = jnp.exp(m2 - m1)
    w1 = 1.0 / (1.0 + e21)
    w2 = e21 / (1.0 + e21)

    hot1 = lane == i1
    hot2 = lane == i2
    both = jnp.where(jnp.logical_or(hot1, hot2), 1.0, 0.0)
    before = _dot(tri_ref[...], both.astype(BF16)) + run_ref[...]
    r1 = jnp.sum(jnp.where(hot1, before, 0.0), axis=-1, keepdims=True)
    r2 = jnp.sum(jnp.where(hot2, before, 0.0), axis=-1, keepdims=True)
    run_ref[...] = run_ref[...] + jnp.sum(both, axis=0, keepdims=True)

    cols = [i1.astype(F32), i2.astype(F32), r1, r2, w1, w2]
    route = jnp.zeros((tm, LANES), F32)
    for c, val in enumerate(cols):
        route = jnp.where(lane == c, val, route)
    return route[:, 0:8]


def _outproj_route_kernel(alpha, oh_ref, os_ref, x_ref, w_ref, mod_ref, ln_ref, modn_ref, wr_ref, tri_ref,
                          x_out, hp_out, route_out, count_out, run_ref):
    @pl.when(jnp.logical_and(pl.program_id(0) == 0, pl.program_id(1) == 0))
    def _():
        run_ref[...] = jnp.zeros_like(run_ref)

    y = _dot(oh_ref[0], w_ref[0:D_HGRN, :]) + _dot(_sb_lanes(os_ref), w_ref[D_HGRN:, :])
    ln = ln_ref[...]
    xn = _residual_ln(x_ref[0], y, mod_ref[0][2:3], ln[0:1], ln[1:2], alpha)
    x_out[0] = xn
    mn = modn_ref[0]
    hn = xn * (1.0 + mn[1:2]) + mn[0:1]
    _store_planes(hp_out, _pack_rows(hn))
    route_out[0] = _route(hn.astype(BF16), wr_ref, tri_ref, run_ref)
    count_out[...] = run_ref[...]


def _out_proj_route(o_h, o_s, x, w_out, mod, ln, mod_next, w_router, alpha):
    bsz, s, d = x.shape
    tm = min(s, 512)
    w_pad = jnp.zeros((d, LANES), BF16).at[:, :N_EXPERTS].set(w_router.astype(BF16))
    jj = jnp.arange(tm)
    tri = (jj[None, :] < jj[:, None]).astype(BF16)
    row = lambda w: pl.BlockSpec((1, tm, w), lambda b, i: (b, i, 0))
    mod_spec = pl.BlockSpec((1, 3, d), lambda b, i: (b, 0, 0))
    return pl.pallas_call(
        functools.partial(_outproj_route_kernel, alpha),
        grid=(bsz, s // tm),
        in_specs=[row(D_HGRN),
                  pl.BlockSpec((D_SB // LANES, 1, tm, LANES), lambda b, i: (0, b, i, 0)),
                  row(d),
                  pl.BlockSpec((d, d), lambda b, i: (0, 0), pipeline_mode=pl.Buffered(1)),
                  mod_spec,
                  pl.BlockSpec((2, d), lambda b, i: (0, 0)),
                  mod_spec,
                  pl.BlockSpec((d, LANES), lambda b, i: (0, 0)),
                  pl.BlockSpec((tm, tm), lambda b, i: (0, 0))],
        out_specs=[row(d),
                   pl.BlockSpec((2, 1, tm, d // 4), lambda b, i: (0, b, i, 0)),
                   row(8),
                   pl.BlockSpec((1, LANES), lambda b, i: (0, 0))],
        out_shape=[jax.ShapeDtypeStruct((bsz, s, d), F32),
                   jax.ShapeDtypeStruct((2, bsz, s, d // 4), U32),
                   jax.ShapeDtypeStruct((bsz, s, 8), F32),
                   jax.ShapeDtypeStruct((1, LANES), F32)],
        scratch_shapes=[pltpu.VMEM((1, LANES), F32)],
        compiler_params=_cparams(("arbitrary", "arbitrary")),
        name="out_proj_ln_route",
    )(o_h, o_s, x, w_out, mod, ln, mod_next, w_pad, tri)


def _gmm_kernel(te_ref, nt_ref, x_ref, wg_ref, wu_ref, wd_ref, o_ref, acc_ref, xb_ref):
    i = pl.program_id(0)
    j = pl.program_id(1)
    nj = pl.num_programs(1)
    live = i < nt_ref[0]

    @pl.when(jnp.logical_and(live, j == 0))
    def _():
        acc_ref[...] = jnp.zeros_like(acc_ref)
        xb_ref[...] = _load_planes(x_ref[0], x_ref[1]).astype(BF16)

    @pl.when(live)
    def _():
        x = xb_ref[...]
        g = _dot(x, wg_ref[0, 0].astype(BF16))
        u = _dot(x, wu_ref[0, 0].astype(BF16))
        acc_ref[...] += _dot((_silu(g) * u).astype(BF16), wd_ref[0, 0].astype(BF16))

    @pl.when(jnp.logical_and(live, j == nj - 1))
    def _():
        _store_planes(o_ref, _pack_rows(acc_ref[...]))

    @pl.when(jnp.logical_and(jnp.logical_not(live), j == nj - 1))
    def _():
        o_ref[...] = jnp.zeros_like(o_ref)


def _grouped_swiglu(xs, tile_expert, n_tiles_used, wg, wu, wd, layer):
    _, n_slots, pw = xs.shape
    d = 4 * pw
    ff = wg.shape[3]
    tm, tf = MOE_TILE, MOE_FF_TILE
    n_tiles = n_slots // tm

    def row_map(i, j, te, nt):
        return (0, jnp.minimum(i, nt[0] - 1), 0)

    def ff_block(i, j, te, nt):
        return jnp.where(i < nt[0], j, ff // tf - 1)

    grid_spec = pltpu.PrefetchScalarGridSpec(
        num_scalar_prefetch=2,
        grid=(n_tiles, ff // tf),
        in_specs=[pl.BlockSpec((2, tm, pw), row_map),
                  pl.BlockSpec((1, 1, d, tf), lambda i, j, te, nt: (layer, te[i], 0, ff_block(i, j, te, nt))),
                  pl.BlockSpec((1, 1, d, tf), lambda i, j, te, nt: (layer, te[i], 0, ff_block(i, j, te, nt))),
                  pl.BlockSpec((1, 1, tf, d), lambda i, j, te, nt: (layer, te[i], ff_block(i, j, te, nt), 0))],
        out_specs=pl.BlockSpec((2, tm, pw), lambda i, j, te, nt: (0, i, 0)),
        scratch_shapes=[pltpu.VMEM((tm, d), F32), pltpu.VMEM((tm, d), BF16)],
    )
    return pl.pallas_call(
        _gmm_kernel,
        grid_spec=grid_spec,
        out_shape=jax.ShapeDtypeStruct((2, n_slots, pw), U32),
        compiler_params=_cparams(("arbitrary", "arbitrary")),
        name="moe_grouped_swiglu",
    )(tile_expert, n_tiles_used, xs, wg, wu, wd)


def _combine_kernel(alpha, has_next, ya_ref, yb_ref, r_ref, x_ref, mod_ref, ln_ref, *rest):
    mod_next_ref = rest[0] if has_next else None
    outs = rest[1:] if has_next else rest
    r = r_ref[0]
    y = (r[:, 4:5] * _load_planes(ya_ref[0, 0, 0], ya_ref[0, 1, 0])
         + r[:, 5:6] * _load_planes(yb_ref[0, 0, 0], yb_ref[0, 1, 0]))
    ln = ln_ref[...]
    xn = _residual_ln(x_ref[0], y, mod_ref[0][2:3], ln[0:1], ln[1:2], alpha)
    _emit(xn, outs, mod_next_ref)


def _combine_ln(yab, route, x, mod, ln, mod_next, alpha):
    bsz, s, d = x.shape
    tm = min(s, 512)
    x_spec, mod_spec, ln_spec, out_specs, out_shape = _row_specs(bsz, s, d, tm, mod_next)
    yab = yab.reshape(2, 2, bsz, s, d // 4)
    in_specs = [pl.BlockSpec((1, 2, 1, tm, d // 4), lambda b, i: (0, 0, b, i, 0)),
                pl.BlockSpec((1, 2, 1, tm, d // 4), lambda b, i: (1, 0, b, i, 0)),
                pl.BlockSpec((1, tm, 8), lambda b, i: (b, i, 0)),
                x_spec, mod_spec, ln_spec]
    args = [yab, yab, route.reshape(bsz, s, 8), x, mod, ln]
    if mod_next is not None:
        in_specs.append(mod_spec)
        args.append(mod_next)
    return pl.pallas_call(
        functools.partial(_combine_kernel, alpha, mod_next is not None),
        grid=(bsz, s // tm),
        in_specs=in_specs, out_specs=out_specs, out_shape=out_shape,
        compiler_params=_cparams(("parallel", "parallel")),
        name="moe_combine_ln",
    )(*args)


def _sc_mesh():
    return plsc.VectorSubcoreMesh(core_axis_name="c", subcore_axis_name="s")


def _dispatch_rows(rows, idx, n_out):
    m, w = rows.shape

    @pl.kernel(out_type=jax.ShapeDtypeStruct((n_out, w), rows.dtype), mesh=_sc_mesh(), scratch_types=[])
    def scatter(rows_hbm, idx_hbm, out_hbm):
        def body(rows_vmem, idx0_vmem, idx1_vmem):
            pltpu.sync_copy(rows_vmem, out_hbm.at[idx0_vmem.at[0]])
            pltpu.sync_copy(rows_vmem, out_hbm.at[idx1_vmem.at[0]])

        pltpu.emit_pipeline(
            body, grid=(m // SC_WINDOW,),
            in_specs=[pl.BlockSpec((SC_WINDOW, w), lambda i: (i, 0)),
                      pl.BlockSpec((1, SC_WINDOW), lambda i: (0, i)),
                      pl.BlockSpec((1, SC_WINDOW), lambda i: (1, i))],
            out_specs=[],
            core_axis_name=("c", "s"),
            dimension_semantics=(pltpu.PARALLEL,),
        )(rows_hbm, idx_hbm, idx_hbm)

    return scatter(rows, idx)


def _collect_rows(rows, idx):
    m = idx.shape[1]
    w = rows.shape[1]

    @pl.kernel(out_type=jax.ShapeDtypeStruct((m, w), rows.dtype), mesh=_sc_mesh(), scratch_types=[])
    def gather(rows_hbm, idx_hbm, out_hbm):
        def body(idx_vmem, out_vmem):
            pltpu.sync_copy(rows_hbm.at[idx_vmem.at[0]], out_vmem)

        pltpu.emit_pipeline(
            body, grid=(m // SC_WINDOW,),
            in_specs=[pl.BlockSpec((1, SC_WINDOW), lambda i: (0, i))],
            out_specs=[pl.BlockSpec((SC_WINDOW, w), lambda i: (i, 0))],
            core_axis_name=("c", "s"),
            dimension_semantics=(pltpu.PARALLEL,),
        )(idx_hbm, out_hbm)

    return gather(rows, idx)


def _moe_ffn_ln(h, route, counts, x, wg, wu, wd, layer, mod, ln, mod_next, alpha):
    bsz, s, d = x.shape
    n = bsz * s
    hp = h.reshape(2, n, d // 4)
    route = route.reshape(n, 8)

    tm = MOE_TILE
    cnt = counts[0, :N_EXPERTS].astype(jnp.int32)
    padded = ((cnt + tm - 1) // tm) * tm
    ends = jnp.cumsum(padded)
    offs = ends - padded
    e = route[:, 0:2].astype(jnp.int32)
    rank = route[:, 2:4].astype(jnp.int32)
    pos = (jnp.take(offs, e) + rank).T
    n_tiles = (2 * n) // tm + N_EXPERTS
    n_slots = n_tiles * tm
    tile_start = jnp.arange(n_tiles, dtype=jnp.int32) * tm
    tile_expert = jnp.minimum(jnp.sum(tile_start[:, None] >= ends[None, :], axis=1),
                              N_EXPERTS - 1).astype(jnp.int32)
    n_used = (ends[-1] // tm).astype(jnp.int32).reshape(1)

    idx = jnp.concatenate([pos, pos + n_slots], axis=1)
    xs = _dispatch_rows(hp.reshape(2 * n, d // 4), idx, 2 * n_slots).reshape(2, n_slots, d // 4)
    ys = _grouped_swiglu(xs, tile_expert, n_used, wg, wu, wd, layer)
    yab = _collect_rows(ys.reshape(2 * n_slots, d // 4), idx.reshape(1, 4 * n)).reshape(2, 2, n, d // 4)
    return _combine_ln(yab, route, x, mod, ln, mod_next, alpha)


def kernel(x, c, w_ada, b_ada, w_in, w_out, hgrn_lb_logits, hgrn_norm_gain,
           w_dense_gate, w_dense_up, w_dense_down, w_router, w_moe_gate, w_moe_up, w_moe_down,
           ln_gain, ln_bias):
    depth = w_in.shape[0]
    bsz, s, d = x.shape
    alpha = (2 * depth) ** 0.25

    mods = _ada_all(c, w_ada, b_ada).reshape(depth, 2, bsz, 3, d)
    lb_all = _lower_bounds(hgrn_lb_logits)
    ln_all = jnp.stack([ln_gain, ln_bias], axis=2)

    h = x
    for layer in range(depth):
        proj_h, proj_s = _in_proj(h, w_in[layer].astype(BF16), mods[0, 0] if layer == 0 else None)
        o_h = _hgrn(proj_h.reshape(4, bsz, s, D_HGRN), lb_all[layer:layer + 1],
                    hgrn_norm_gain[layer:layer + 1])
        o_s = _stick_breaking(proj_s.reshape(-1, bsz, s, LANES))
        mod_next = mods[layer + 1, 0] if layer + 1 < depth else None
        j = layer // 2
        if layer % 2 == 0:
            res = _mix_dense_ln(o_h, o_s, x, w_out[layer].astype(BF16), w_dense_gate[j].astype(BF16),
                                w_dense_up[j].astype(BF16), w_dense_down[j].astype(BF16),
                                mods[layer, 0], mods[layer, 1], ln_all[layer, 0], ln_all[layer, 1],
                                mod_next, alpha)
        else:
            x, h, route, counts = _out_proj_route(o_h, o_s, x, w_out[layer].astype(BF16), mods[layer, 0],
                                                  ln_all[layer, 0], mods[layer, 1], w_router[j], alpha)
            res = _moe_ffn_ln(h, route, counts, x, w_moe_gate, w_moe_up, w_moe_down, j,
                              mods[layer, 1], ln_all[layer, 1], mod_next, alpha)
        if mod_next is not None:
            x, h = res
        else:
            (x,) = res
    return x
```

```python
import functools
import math

import jax
import jax.numpy as jnp
from jax import lax
from jax.experimental import pallas as pl
from jax.experimental.pallas import tpu as pltpu
from jax.experimental.pallas import tpu_sc as plsc

D_MODEL = 1024
D_HGRN = 512
HGRN_HEADS = 4
HGRN_DH = 128
D_SB = 512
SB_HEADS = 8
SB_DH = 64
D_IN_PROJ = 4 * D_HGRN + 3 * D_SB
N_EXPERTS = 8
LN_EPS = 1e-5
RMS_EPS = 1e-6

LOG2E = math.log2(math.e)
LANES = 128
VMEM_LIMIT = 52 * 1024 * 1024

HG_UNIT = 128
HG_BAND = 4
HG_PAD = 8
HG_LEVELS = (64, 32, 16, 8, 4)
SB_BLOCK = 128
SB_SKIP = 110.0 * LOG2E
SB_GROUP = 32
SB_WIN = 3
MOE_TILE = 1024
MOE_FF_TILE = 512
DENSE_SPLIT = 2
SC_WINDOW = 128

F32 = jnp.float32
BF16 = jnp.bfloat16
U32 = jnp.uint32


def _cparams(sem):
    return pltpu.CompilerParams(dimension_semantics=sem, vmem_limit_bytes=VMEM_LIMIT)


def _dot(a, b):
    return jnp.dot(a, b, preferred_element_type=F32)


def _dot_nt(a, b):
    return lax.dot_general(a, b, (((1,), (1,)), ((), ())), preferred_element_type=F32)


def _dot_tn(a, b):
    return lax.dot_general(a, b, (((0,), (0,)), ((), ())), preferred_element_type=F32)


def _pack_rows(v):
    w = v.shape[-1] // 2
    lo = lax.bitcast_convert_type(v[:, :w].astype(BF16).astype(F32), U32)
    hi = lax.bitcast_convert_type(v[:, w:].astype(BF16).astype(F32), U32)
    return (lo >> 16) | hi


def _unpack_rows(p):
    lo = lax.bitcast_convert_type(p << 16, F32)
    hi = lax.bitcast_convert_type(p & jnp.uint32(0xFFFF0000), F32)
    return jnp.concatenate([lo, hi], axis=-1)


def _store_planes(ref, p):
    w = p.shape[-1] // 2
    ref[0] = p[:, :w].reshape(ref.shape[1:])
    ref[1] = p[:, w:].reshape(ref.shape[1:])


def _load_planes(p0, p1):
    return _unpack_rows(jnp.concatenate([p0, p1], axis=-1))


def _sigmoid(x):
    return 1.0 / (1.0 + jnp.exp(-x))


def _silu(x):
    hx = 0.5 * x
    return hx + hx * jnp.tanh(hx)


def _ada_kernel(c_ref, w_ref, b_ref, o_ref):
    sc = _silu(c_ref[...])
    o_ref[0] = jnp.dot(sc, w_ref[0], preferred_element_type=F32,
                       precision=lax.Precision.HIGHEST) + b_ref[0]


def _ada_all(c, w_ada, b_ada):
    n_sub = w_ada.shape[0] * w_ada.shape[1]
    bsz, d = c.shape
    w = w_ada.reshape(n_sub, d, 3 * d)
    b = b_ada.reshape(n_sub, 1, 3 * d)
    tn = 1024
    return pl.pallas_call(
        _ada_kernel,
        grid=(n_sub, 3 * d // tn),
        in_specs=[pl.BlockSpec((bsz, d), lambda s, j: (0, 0)),
                  pl.BlockSpec((1, d, tn), lambda s, j: (s, 0, j)),
                  pl.BlockSpec((1, 1, tn), lambda s, j: (s, 0, j))],
        out_specs=pl.BlockSpec((1, bsz, tn), lambda s, j: (s, 0, j)),
        out_shape=jax.ShapeDtypeStruct((n_sub, bsz, 3 * d), F32),
        compiler_params=_cparams(("parallel", "parallel")),
        name="ada_modulation",
    )(c, w, b)


def _lb_kernel(l_ref, o_ref):
    x = l_ref[...]
    m = jnp.max(x, axis=0, keepdims=True)
    e = jnp.exp(x - m)
    p = e / jnp.sum(e, axis=0, keepdims=True)
    depth = x.shape[0]
    run = jnp.zeros_like(p[0:1])
    rows = []
    for i in range(depth):
        run = run + p[i:i + 1]
        rows.append(run - p[0:1])
    o_ref[...] = jnp.concatenate(rows, axis=0)


def _lower_bounds(lb_logits):
    return pl.pallas_call(
        _lb_kernel,
        out_shape=jax.ShapeDtypeStruct(lb_logits.shape, F32),
        name="hgrn_lower_bounds",
    )(lb_logits)


def _inproj_kernel(modulate, h_ref, *rest):
    if modulate:
        m_ref, w_ref, oh_ref, os_ref = rest
        m = m_ref[0]
        h = (h_ref[...] * (1.0 + m[1:2]) + m[0:1]).astype(BF16)
    else:
        w_ref, oh_ref, os_ref = rest
        h = h_ref[...]
    tn = 512
    nh = oh_ref.shape[0]
    for j in range(nh):
        oh_ref[j] = _dot(h, w_ref[:, j * tn:(j + 1) * tn]).astype(BF16)
    per = tn // LANES
    for j in range(os_ref.shape[0] // per):
        blk = _dot(h, w_ref[:, (nh + j) * tn:(nh + j + 1) * tn]).astype(BF16)
        for c in range(per):
            os_ref[j * per + c] = blk[:, c * LANES:(c + 1) * LANES]


def _in_proj(h, w_in, mod=None):
    bsz, s, d = h.shape
    n = bsz * s
    dout = w_in.shape[1]
    dh = 4 * D_HGRN
    n_slabs = (dout - dh) // LANES
    tm = min(s, 1024)
    per = s // tm
    in_specs = [pl.BlockSpec((tm, d), lambda b, i: (b * per + i, 0))]
    args = [h.reshape(n, d)]
    if mod is not None:
        in_specs.append(pl.BlockSpec((1, 3, d), lambda b, i: (b, 0, 0)))
        args.append(mod)
    in_specs.append(pl.BlockSpec((d, dout), lambda b, i: (0, 0), pipeline_mode=pl.Buffered(1)))
    args.append(w_in)
    return pl.pallas_call(
        functools.partial(_inproj_kernel, mod is not None),
        grid=(bsz, per),
        in_specs=in_specs,
        out_specs=[pl.BlockSpec((4, tm, D_HGRN), lambda b, i: (0, b * per + i, 0)),
                   pl.BlockSpec((n_slabs, tm, LANES), lambda b, i: (0, b * per + i, 0))],
        out_shape=[jax.ShapeDtypeStruct((4, n, D_HGRN), BF16),
                   jax.ShapeDtypeStruct((n_slabs, n, LANES), BF16)],
        compiler_params=_cparams(("parallel", "parallel")),
        name="in_proj",
    )(*args)


def _hgrn_consts():
    t = jnp.arange(HG_UNIT)[:, None]
    s = jnp.arange(HG_UNIT)[None, :]
    tril = (s <= t).astype(BF16)
    lv = jnp.full((HG_UNIT, HG_UNIT), -1, jnp.int32)
    for idx, m in enumerate(HG_LEVELS):
        blk = 2 * m
        hit = (t // blk == s // blk) & (t % blk >= m) & (s % blk < m)
        lv = jnp.where(hit, idx, lv)
    return tril, lv


def _hgrn_kernel(q_ref, f_ref, i_ref, g_ref, lb_ref, gain_ref, tril_ref, lv_ref, o_ref,
                 states_ref, kpads_ref, fpads_ref, vpads_ref):
    @pl.when(pl.program_id(1) == 0)
    def _():
        states_ref[...] = jnp.zeros_like(states_ref)

    heads = range(HGRN_HEADS)
    zpad = jnp.zeros((HGRN_HEADS, HG_PAD, HGRN_DH), F32)
    kpads_ref[:, 0:HG_PAD, :] = zpad
    fpads_ref[:, 0:HG_PAD, :] = zpad
    vpads_ref[:, 0:HG_PAD, :] = zpad

    n_units = q_ref.shape[2] // HG_UNIT
    rowi = lax.broadcasted_iota(jnp.int32, (HG_UNIT, HGRN_DH), 0)
    tmod = jnp.bitwise_and(rowi, HG_BAND - 1)

    def unit(u, carry):
        rows = pl.ds(pl.multiple_of(u * HG_UNIT, HG_UNIT), HG_UNIT)
        lanes = [slice(h * HGRN_DH, (h + 1) * HGRN_DH) for h in heads]

        fs, kk, qq, v, vb, b3 = [], [], [], [], [], []
        for h in heads:
            lb = lb_ref[:, lanes[h]]
            f = lb + (1.0 - lb) * _sigmoid(f_ref[0, 0, rows, lanes[h]].astype(F32))
            logf = jnp.log2(f)
            fs.append(f)
            kk.append(1.0 - f)
            qq.append(_silu(q_ref[0, 0, rows, lanes[h]].astype(F32)))
            v.append(i_ref[0, 0, rows, lanes[h]].astype(F32))
            vb.append(i_ref[0, 0, rows, lanes[h]])
            p0 = logf.astype(BF16)
            r1 = logf - p0.astype(F32)
            p1 = r1.astype(BF16)
            p2 = (r1 - p1.astype(F32)).astype(BF16)
            b3.append(_dot(tril_ref[...], jnp.concatenate([p0, p1, p2], axis=1)))
        b = [t[:, 0:HGRN_DH] + t[:, HGRN_DH:2 * HGRN_DH] + t[:, 2 * HGRN_DH:] for t in b3]

        state = [states_ref[h] for h in heads]
        o = [_dot_nt((qq[h] * jnp.exp2(b[h])).astype(BF16), state[h].astype(BF16)) for h in heads]

        lv = lv_ref[...]
        scores = [jnp.zeros((HG_UNIT, HG_UNIT), F32) for _ in heads]
        for idx, m in enumerate(HG_LEVELS):
            blk = 2 * m
            nb = HG_UNIT // blk
            upper = jnp.bitwise_and(rowi, m) != 0
            xl = []
            for h in heads:
                ref = b[h].reshape(nb, blk, HGRN_DH)[:, m - 1:m, :]
                ref = jnp.broadcast_to(ref, (nb, blk, HGRN_DH)).reshape(HG_UNIT, HGRN_DH)
                xl.append((jnp.where(upper, qq[h], kk[h]) * jnp.exp2(-jnp.abs(b[h] - ref))).astype(BF16))
            s_l = [_dot_nt(x, x) for x in xl]
            scores = [jnp.where(lv == idx, s_l[h], scores[h]) for h in heads]
        intra = [_dot(scores[h].astype(BF16), vb[h]) for h in heads]

        for h in heads:
            kpads_ref[h, HG_PAD:, :] = kk[h]
            fpads_ref[h, HG_PAD:, :] = fs[h]
            vpads_ref[h, HG_PAD:, :] = v[h]
        band = [jnp.sum(qq[h] * kk[h], axis=-1, keepdims=True) * v[h] for h in heads]
        dec = [None for _ in heads]
        for d in range(1, HG_BAND):
            win = pl.ds(HG_PAD - d, HG_UNIT)
            for h in heads:
                dec[h] = fs[h] if d == 1 else dec[h] * fpads_ref[h, pl.ds(HG_PAD - (d - 1), HG_UNIT), :]
                p = qq[h] * kpads_ref[h, win, :] * dec[h]
                s_d = jnp.sum(p, axis=-1, keepdims=True)
                band[h] = band[h] + jnp.where(tmod >= d, s_d * vpads_ref[h, win, :], 0.0)

        for h in heads:
            b_last = b[h][HG_UNIT - 1:HG_UNIT, :]
            kdec = kk[h] * jnp.exp2(b_last - b[h])
            states_ref[h] = state[h] * jnp.exp2(b_last) + _dot_tn(vb[h], kdec.astype(BF16))
        for h in heads:
            oh = o[h] + intra[h] + band[h]
            oh = oh * lax.rsqrt(jnp.mean(oh * oh, axis=-1, keepdims=True) + RMS_EPS)
            gg = g_ref[0, 0, rows, lanes[h]].astype(F32)
            o_ref[0, rows, lanes[h]] = (oh * gain_ref[:, lanes[h]] * _silu(gg)).astype(BF16)
        return carry

    lax.fori_loop(0, n_units, unit, 0)


def _hgrn(proj, lb, gain):
    _, bsz, s, _ = proj.shape
    ts = min(s, 1024)
    tril, lv = _hgrn_consts()

    def col(off):
        return pl.BlockSpec((1, 1, ts, D_HGRN), lambda b, i, off=off: (off, b, i, 0))

    return pl.pallas_call(
        _hgrn_kernel,
        grid=(bsz, s // ts),
        in_specs=[col(0), col(1), col(2), col(3),
                  pl.BlockSpec((1, D_HGRN), lambda b, i: (0, 0)),
                  pl.BlockSpec((1, D_HGRN), lambda b, i: (0, 0)),
                  pl.BlockSpec((HG_UNIT, HG_UNIT), lambda b, i: (0, 0)),
                  pl.BlockSpec((HG_UNIT, HG_UNIT), lambda b, i: (0, 0))],
        out_specs=pl.BlockSpec((1, ts, D_HGRN), lambda b, i: (b, i, 0)),
        out_shape=jax.ShapeDtypeStruct((bsz, s, D_HGRN), BF16),
        scratch_shapes=[pltpu.VMEM((HGRN_HEADS, HGRN_DH, HGRN_DH), F32),
                        pltpu.VMEM((HGRN_HEADS, HG_UNIT + HG_PAD, HGRN_DH), F32),
                        pltpu.VMEM((HGRN_HEADS, HG_UNIT + HG_PAD, HGRN_DH), F32),
                        pltpu.VMEM((HGRN_HEADS, HG_UNIT + HG_PAD, HGRN_DH), F32)],
        compiler_params=_cparams(("parallel", "arbitrary")),
        name="hgrn2",
    )(proj, proj, proj, proj, lb, gain, tril, lv)


def _sb_kernel(q_ref, k_ref, v_ref, uo_ref, o_ref, kp_ref, vp_ref, c_ref, acc_ref, more_ref):
    tq = SB_BLOCK
    nq = q_ref.shape[2] // tq
    pad = (SB_WIN - 1) * tq
    kp_ref[0:pad, :] = jnp.zeros((pad, LANES), BF16)
    vp_ref[0:pad, :] = jnp.zeros((pad, LANES), BF16)
    kp_ref[pad:, :] = (k_ref[0, 0].astype(F32) * LOG2E).astype(BF16)
    vp_ref[pad:, :] = v_ref[0, 0]

    first = lax.broadcasted_iota(jnp.int32, (tq, LANES), 1) < SB_DH
    rowi = lax.broadcasted_iota(jnp.int32, (2 * tq, tq), 0)
    coli = lax.broadcasted_iota(jnp.int32, (2 * tq, tq), 1)
    strict = coli < jnp.bitwise_and(rowi, tq - 1)

    def stacked_q(qi):
        rows = pl.ds(pl.multiple_of(qi * tq, tq), tq)
        q = q_ref[0, 0, rows, :] * (SB_DH ** -0.5)
        zero = jnp.zeros_like(q)
        return rows, jnp.concatenate([jnp.where(first, q, zero), jnp.where(first, zero, q)], axis=0)

    def windows(items, diag, win):
        order = list(reversed(range(win)))
        zs, cts = [], []
        for q2, p_top, _, _ in items:
            start = pl.multiple_of((p_top - (win - 1)) * tq, tq)
            zs.append(_dot_nt(q2, kp_ref[pl.ds(start, win * tq), :]))
        for z in zs:
            logn = -(jnp.maximum(z, 0.0) + jnp.log2(1.0 + jnp.exp2(-jnp.abs(z))))
            lhs = []
            for j in order:
                lj = logn[:, j * tq:(j + 1) * tq]
                if diag and j == win - 1:
                    lj = jnp.where(strict, lj, 0.0)
                hi = lj.astype(BF16)
                lo = (lj - hi.astype(F32)).astype(BF16)
                lhs.append(jnp.concatenate([hi, lo], axis=1))
            cts.append(_dot(jnp.concatenate(lhs, axis=0), uo_ref[...]))
        outs = []
        for (q2, p_top, crow, acc), z, ct in zip(items, zs, cts):
            start = pl.multiple_of((p_top - (win - 1)) * tq, tq)
            parts = [None] * win
            for idx, j in enumerate(order):
                blk = ct[idx * 2 * tq:(idx + 1) * 2 * tq]
                a = jnp.exp2(z[:, j * tq:(j + 1) * tq] + (crow + blk))
                if diag and j == win - 1:
                    a = jnp.where(strict, a, 0.0)
                parts[j] = a.astype(BF16)
                crow = crow + jnp.broadcast_to(blk[:, 0:1], blk.shape)
            acc = acc + _dot(jnp.concatenate(parts, axis=1), vp_ref[pl.ds(start, win * tq), :])
            outs.append((crow, acc))
        return outs

    def emit(rows, acc):
        o_ref[0, 0, rows, :] = jnp.where(first, acc[0:tq], acc[tq:]).astype(BF16)

    def fast(g, carry):
        zero = jnp.zeros((2 * tq, LANES), F32)
        qis = [g * SB_GROUP + t for t in range(SB_GROUP)]
        prep = [stacked_q(qi) for qi in qis]
        res = windows([(q2, qi + SB_WIN - 1, zero, zero) for qi, (_, q2) in zip(qis, prep)], True, SB_WIN)
        for qi, (rows, _), (crow, acc) in zip(qis, prep, res):
            emit(rows, acc)
            c_ref[qi] = crow
            acc_ref[qi] = acc
            unfinished = jnp.logical_and(qi >= SB_WIN, jnp.max(crow) > -SB_SKIP)
            more_ref[qi] = unfinished.astype(jnp.int32)
        return carry

    lax.fori_loop(0, nq // SB_GROUP, fast, 0)

    def finish(qi, carry):
        @pl.when(more_ref[qi] != 0)
        def _():
            rows, q2 = stacked_q(qi)

            def cond(c):
                p_top, mx, _, _ = c
                return jnp.logical_and(p_top >= SB_WIN - 1, mx > -SB_SKIP)

            def body(c):
                p_top, _, crow, acc = c
                (crow, acc), = windows([(q2, p_top, crow, acc)], False, 1)
                return p_top - 1, jnp.max(crow), crow, acc

            crow0 = c_ref[qi]
            _, _, _, acc = lax.while_loop(cond, body, (qi - 1, jnp.max(crow0), crow0, acc_ref[qi]))
            emit(rows, acc)
        return carry

    lax.fori_loop(SB_WIN, nq, finish, 0)


def _stick_breaking(qkv):
    _, bsz, s, _ = qkv.shape
    tq = SB_BLOCK
    nq = s // tq
    assert nq % SB_GROUP == 0
    n_pairs = D_SB // LANES
    jj = jnp.arange(tq)
    suffix = (jj[:, None] >= jj[None, :]).astype(BF16)
    uo = jnp.concatenate([suffix, suffix], axis=0)

    def seq(off):
        return pl.BlockSpec((1, 1, s, LANES), lambda b, p, off=off: (off * n_pairs + p, b, 0, 0))

    return pl.pallas_call(
        _sb_kernel,
        grid=(bsz, n_pairs),
        in_specs=[seq(0), seq(1), seq(2),
                  pl.BlockSpec((2 * tq, tq), lambda b, p: (0, 0))],
        out_specs=pl.BlockSpec((1, 1, s, LANES), lambda b, p: (p, b, 0, 0)),
        out_shape=jax.ShapeDtypeStruct((n_pairs, bsz, s, LANES), BF16),
        scratch_shapes=[pltpu.VMEM((s + (SB_WIN - 1) * tq, LANES), BF16),
                        pltpu.VMEM((s + (SB_WIN - 1) * tq, LANES), BF16),
                        pltpu.VMEM((nq, 2 * tq, LANES), F32),
                        pltpu.VMEM((nq, 2 * tq, LANES), F32),
                        pltpu.SMEM((nq,), jnp.int32)],
        compiler_params=_cparams(("parallel", "parallel")),
        name="stick_breaking",
    )(qkv, qkv, qkv, uo)


def _residual_ln(x, y, gate, gain, bias, alpha):
    r = alpha * x + gate * y
    mu = jnp.mean(r, axis=-1, keepdims=True)
    rc = r - mu
    var = jnp.mean(rc * rc, axis=-1, keepdims=True)
    return rc * lax.rsqrt(var + LN_EPS) * gain + bias


def _sb_lanes(os_ref):
    return jnp.concatenate([os_ref[p, 0] for p in range(os_ref.shape[0])], axis=-1)


def _emit(xn, outs, mod_next_ref):
    outs[0][0] = xn
    if mod_next_ref is not None:
        mn = mod_next_ref[0]
        outs[1][0] = (xn * (1.0 + mn[1:2]) + mn[0:1]).astype(BF16)


def _row_specs(bsz, s, d, tm, mod_next):
    x_spec = pl.BlockSpec((1, tm, d), lambda b, i: (b, i, 0))
    mod_spec = pl.BlockSpec((1, 3, d), lambda b, i: (b, 0, 0))
    ln_spec = pl.BlockSpec((2, d), lambda b, i: (0, 0))
    out_specs = [pl.BlockSpec((1, tm, d), lambda b, i: (b, i, 0))]
    out_shape = [jax.ShapeDtypeStruct((bsz, s, d), F32)]
    if mod_next is not None:
        out_specs.append(pl.BlockSpec((1, tm, d), lambda b, i: (b, i, 0)))
        out_shape.append(jax.ShapeDtypeStruct((bsz, s, d), BF16))
    return x_spec, mod_spec, ln_spec, out_specs, out_shape


def _mix_dense_kernel(alpha, has_next, n_chunks, oh_ref, os_ref, x_ref, wo_ref, wg_ref, wu_ref, wd_ref,
                      mod0_ref, mod1_ref, ln0_ref, ln1_ref, *rest):
    mod_next_ref = rest[0] if has_next else None
    outs = rest[1:] if has_next else rest
    tm = x_ref.shape[1]
    parts = [slice(r * tm // DENSE_SPLIT, (r + 1) * tm // DENSE_SPLIT) for r in range(DENSE_SPLIT)]
    ln0 = ln0_ref[...]
    ln1 = ln1_ref[...]
    m1 = mod1_ref[0]
    ff = wg_ref.shape[1]
    tf = ff // n_chunks
    osb = [jnp.concatenate([os_ref[p, 0, r, :] for p in range(os_ref.shape[0])], axis=-1) for r in parts]
    ys = [_dot(oh_ref[0, r, :], wo_ref[0:D_HGRN, :]) + _dot(o, wo_ref[D_HGRN:, :]) for r, o in zip(parts, osb)]
    x1s = [_residual_ln(x_ref[0, r, :], y, mod0_ref[0][2:3], ln0[0:1], ln0[1:2], alpha)
           for r, y in zip(parts, ys)]
    hs = [(x1 * (1.0 + m1[1:2]) + m1[0:1]).astype(BF16) for x1 in x1s]
    y2 = [None] * len(parts)
    for j in range(n_chunks):
        cols = slice(j * tf, (j + 1) * tf)
        gs = [_dot(h, wg_ref[:, cols]) for h in hs]
        us = [_dot(h, wu_ref[:, cols]) for h in hs]
        ps = [_dot((_silu(g) * u).astype(BF16), wd_ref[cols, :]) for g, u in zip(gs, us)]
        y2 = [p if y is None else y + p for y, p in zip(y2, ps)]
    for r, x1, y in zip(parts, x1s, y2):
        x2 = _residual_ln(x1, y, m1[2:3], ln1[0:1], ln1[1:2], alpha)
        outs[0][0, r, :] = x2
        if mod_next_ref is not None:
            mn = mod_next_ref[0]
            outs[1][0, r, :] = (x2 * (1.0 + mn[1:2]) + mn[0:1]).astype(BF16)


def _mix_dense_ln(o_h, o_s, x, w_out, wg, wu, wd, mod0, mod1, ln0, ln1, mod_next, alpha):
    bsz, s, d = x.shape
    ff = wg.shape[1]
    tm = min(s, 512)
    x_spec, mod_spec, ln_spec, out_specs, out_shape = _row_specs(bsz, s, d, tm, mod_next)
    once = pl.Buffered(1)
    in_specs = [pl.BlockSpec((1, tm, D_HGRN), lambda b, i: (b, i, 0)),
                pl.BlockSpec((D_SB // LANES, 1, tm, LANES), lambda b, i: (0, b, i, 0)),
                x_spec,
                pl.BlockSpec((d, d), lambda b, i: (0, 0), pipeline_mode=once),
                pl.BlockSpec((d, ff), lambda b, i: (0, 0), pipeline_mode=once),
                pl.BlockSpec((d, ff), lambda b, i: (0, 0), pipeline_mode=once),
                pl.BlockSpec((ff, d), lambda b, i: (0, 0), pipeline_mode=once),
                mod_spec, mod_spec, ln_spec, ln_spec]
    args = [o_h, o_s, x, w_out, wg, wu, wd, mod0, mod1, ln0, ln1]
    if mod_next is not None:
        in_specs.append(mod_spec)
        args.append(mod_next)
    n_chunks = 2 if (ff // 2) % LANES == 0 else 1
    return pl.pallas_call(
        functools.partial(_mix_dense_kernel, alpha, mod_next is not None, n_chunks),
        grid=(bsz, s // tm),
        in_specs=in_specs, out_specs=out_specs, out_shape=out_shape,
        compiler_params=_cparams(("parallel", "parallel")),
        name="out_proj_dense_ffn_ln",
    )(*args)


def _route(h, w_ref, tri_ref, run_ref):
    tm = h.shape[0]
    lane = lax.broadcasted_iota(jnp.int32, (tm, LANES), 1)
    logits = _dot(h, w_ref[...])
    neg = jnp.float32(-jnp.inf)
    logits = jnp.where(lane < N_EXPERTS, logits, neg)
    m1 = jnp.max(logits, axis=-1, keepdims=True)
    i1 = jnp.min(jnp.where(logits == m1, lane, LANES), axis=-1, keepdims=True)
    rest = jnp.where(lane == i1, neg, logits)
    m2 = jnp.max(rest, axis=-1, keepdims=True)
    i2 = jnp.min(jnp.where(rest == m2, lane, LANES), axis=-1, keepdims=True)
    e21 = jnp.exp(m2 - m1)
    w1 = 1.0 / (1.0 + e21)
    w2 = e21 / (1.0 + e21)

    hot1 = lane == i1
    hot2 = lane == i2
    both = jnp.where(jnp.logical_or(hot1, hot2), 1.0, 0.0)
    before = _dot(tri_ref[...], both.astype(BF16)) + run_ref[...]
    r1 = jnp.sum(jnp.where(hot1, before, 0.0), axis=-1, keepdims=True)
    r2 = jnp.sum(jnp.where(hot2, before, 0.0), axis=-1, keepdims=True)
    run_ref[...] = run_ref[...] + jnp.sum(both, axis=0, keepdims=True)

    cols = [i1.astype(F32), i2.astype(F32), r1, r2, w1, w2]
    route = jnp.zeros((tm, LANES), F32)
    for c, val in enumerate(cols):
        route = jnp.where(lane == c, val, route)
    return route[:, 0:8]


def _outproj_route_kernel(alpha, oh_ref, os_ref, x_ref, w_ref, mod_ref, ln_ref, modn_ref, wr_ref, tri_ref,
                          x_out, hp_out, route_out, count_out, run_ref):
    @pl.when(jnp.logical_and(pl.program_id(0) == 0, pl.program_id(1) == 0))
    def _():
        run_ref[...] = jnp.zeros_like(run_ref)

    y = _dot(oh_ref[0], w_ref[0:D_HGRN, :]) + _dot(_sb_lanes(os_ref), w_ref[D_HGRN:, :])
    ln = ln_ref[...]
    xn = _residual_ln(x_ref[0], y, mod_ref[0][2:3], ln[0:1], ln[1:2], alpha)
    x_out[0] = xn
    mn = modn_ref[0]
    hn = xn * (1.0 + mn[1:2]) + mn[0:1]
    _store_planes(hp_out, _pack_rows(hn))
    route_out[0] = _route(hn.astype(BF16), wr_ref, tri_ref, run_ref)
    count_out[...] = run_ref[...]


def _out_proj_route(o_h, o_s, x, w_out, mod, ln, mod_next, w_router, alpha):
    bsz, s, d = x.shape
    tm = min(s, 512)
    w_pad = jnp.zeros((d, LANES), BF16).at[:, :N_EXPERTS].set(w_router.astype(BF16))
    jj = jnp.arange(tm)
    tri = (jj[None, :] < jj[:, None]).astype(BF16)
    row = lambda w: pl.BlockSpec((1, tm, w), lambda b, i: (b, i, 0))
    mod_spec = pl.BlockSpec((1, 3, d), lambda b, i: (b, 0, 0))
    return pl.pallas_call(
        functools.partial(_outproj_route_kernel, alpha),
        grid=(bsz, s // tm),
        in_specs=[row(D_HGRN),
                  pl.BlockSpec((D_SB // LANES, 1, tm, LANES), lambda b, i: (0, b, i, 0)),
                  row(d),
                  pl.BlockSpec((d, d), lambda b, i: (0, 0), pipeline_mode=pl.Buffered(1)),
                  mod_spec,
                  pl.BlockSpec((2, d), lambda b, i: (0, 0)),
                  mod_spec,
                  pl.BlockSpec((d, LANES), lambda b, i: (0, 0)),
                  pl.BlockSpec((tm, tm), lambda b, i: (0, 0))],
        out_specs=[row(d),
                   pl.BlockSpec((2, 1, tm, d // 4), lambda b, i: (0, b, i, 0)),
                   row(8),
                   pl.BlockSpec((1, LANES), lambda b, i: (0, 0))],
        out_shape=[jax.ShapeDtypeStruct((bsz, s, d), F32),
                   jax.ShapeDtypeStruct((2, bsz, s, d // 4), U32),
                   jax.ShapeDtypeStruct((bsz, s, 8), F32),
                   jax.ShapeDtypeStruct((1, LANES), F32)],
        scratch_shapes=[pltpu.VMEM((1, LANES), F32)],
        compiler_params=_cparams(("arbitrary", "arbitrary")),
        name="out_proj_ln_route",
    )(o_h, o_s, x, w_out, mod, ln, mod_next, w_pad, tri)


def _gmm_kernel(te_ref, nt_ref, x_ref, wg_ref, wu_ref, wd_ref, o_ref, acc_ref, xb_ref):
    i = pl.program_id(0)
    j = pl.program_id(1)
    nj = pl.num_programs(1)
    live = i < nt_ref[0]

    @pl.when(jnp.logical_and(live, j == 0))
    def _():
        acc_ref[...] = jnp.zeros_like(acc_ref)
        xb_ref[...] = _load_planes(x_ref[0], x_ref[1]).astype(BF16)

    @pl.when(live)
    def _():
        x = xb_ref[...]
        g = _dot(x, wg_ref[0, 0].astype(BF16))
        u = _dot(x, wu_ref[0, 0].astype(BF16))
        acc_ref[...] += _dot((_silu(g) * u).astype(BF16), wd_ref[0, 0].astype(BF16))

    @pl.when(jnp.logical_and(live, j == nj - 1))
    def _():
        _store_planes(o_ref, _pack_rows(acc_ref[...]))

    @pl.when(jnp.logical_and(jnp.logical_not(live), j == nj - 1))
    def _():
        o_ref[...] = jnp.zeros_like(o_ref)


def _grouped_swiglu(xs, tile_expert, n_tiles_used, wg, wu, wd, layer):
    _, n_slots, pw = xs.shape
    d = 4 * pw
    ff = wg.shape[3]
    tm, tf = MOE_TILE, MOE_FF_TILE
    n_tiles = n_slots // tm

    def row_map(i, j, te, nt):
        return (0, jnp.minimum(i, nt[0] - 1), 0)

    def ff_block(i, j, te, nt):
        return jnp.where(i < nt[0], j, ff // tf - 1)

    grid_spec = pltpu.PrefetchScalarGridSpec(
        num_scalar_prefetch=2,
        grid=(n_tiles, ff // tf),
        in_specs=[pl.BlockSpec((2, tm, pw), row_map),
                  pl.BlockSpec((1, 1, d, tf), lambda i, j, te, nt: (layer, te[i], 0, ff_block(i, j, te, nt))),
                  pl.BlockSpec((1, 1, d, tf), lambda i, j, te, nt: (layer, te[i], 0, ff_block(i, j, te, nt))),
                  pl.BlockSpec((1, 1, tf, d), lambda i, j, te, nt: (layer, te[i], ff_block(i, j, te, nt), 0))],
        out_specs=pl.BlockSpec((2, tm, pw), lambda i, j, te, nt: (0, i, 0)),
        scratch_shapes=[pltpu.VMEM((tm, d), F32), pltpu.VMEM((tm, d), BF16)],
    )
    return pl.pallas_call(
        _gmm_kernel,
        grid_spec=grid_spec,
        out_shape=jax.ShapeDtypeStruct((2, n_slots, pw), U32),
        compiler_params=_cparams(("arbitrary", "arbitrary")),
        name="moe_grouped_swiglu",
    )(tile_expert, n_tiles_used, xs, wg, wu, wd)


def _combine_kernel(alpha, has_next, ya_ref, yb_ref, r_ref, x_ref, mod_ref, ln_ref, *rest):
    mod_next_ref = rest[0] if has_next else None
    outs = rest[1:] if has_next else rest
    r = r_ref[0]
    y = (r[:, 4:5] * _load_planes(ya_ref[0, 0, 0], ya_ref[0, 1, 0])
         + r[:, 5:6] * _load_planes(yb_ref[0, 0, 0], yb_ref[0, 1, 0]))
    ln = ln_ref[...]
    xn = _residual_ln(x_ref[0], y, mod_ref[0][2:3], ln[0:1], ln[1:2], alpha)
    _emit(xn, outs, mod_next_ref)


def _combine_ln(yab, route, x, mod, ln, mod_next, alpha):
    bsz, s, d = x.shape
    tm = min(s, 1024)
    x_spec, mod_spec, ln_spec, out_specs, out_shape = _row_specs(bsz, s, d, tm, mod_next)
    yab = yab.reshape(2, 2, bsz, s, d // 4)
    in_specs = [pl.BlockSpec((1, 2, 1, tm, d // 4), lambda b, i: (0, 0, b, i, 0)),
                pl.BlockSpec((1, 2, 1, tm, d // 4), lambda b, i: (1, 0, b, i, 0)),
                pl.BlockSpec((1, tm, 8), lambda b, i: (b, i, 0)),
                x_spec, mod_spec, ln_spec]
    args = [yab, yab, route.reshape(bsz, s, 8), x, mod, ln]
    if mod_next is not None:
        in_specs.append(mod_spec)
        args.append(mod_next)
    return pl.pallas_call(
        functools.partial(_combine_kernel, alpha, mod_next is not None),
        grid=(bsz, s // tm),
        in_specs=in_specs, out_specs=out_specs, out_shape=out_shape,
        compiler_params=_cparams(("parallel", "parallel")),
        name="moe_combine_ln",
    )(*args)


def _sc_mesh():
    return plsc.VectorSubcoreMesh(core_axis_name="c", subcore_axis_name="s")


def _dispatch_rows(rows, idx, n_out):
    m, w = rows.shape

    @pl.kernel(out_type=jax.ShapeDtypeStruct((n_out, w), rows.dtype), mesh=_sc_mesh(), scratch_types=[])
    def scatter(rows_hbm, idx_hbm, out_hbm):
        def body(rows_vmem, idx0_vmem, idx1_vmem):
            pltpu.sync_copy(rows_vmem, out_hbm.at[idx0_vmem.at[0]])
            pltpu.sync_copy(rows_vmem, out_hbm.at[idx1_vmem.at[0]])

        pltpu.emit_pipeline(
            body, grid=(m // SC_WINDOW,),
            in_specs=[pl.BlockSpec((SC_WINDOW, w), lambda i: (i, 0)),
                      pl.BlockSpec((1, SC_WINDOW), lambda i: (0, i)),
                      pl.BlockSpec((1, SC_WINDOW), lambda i: (1, i))],
            out_specs=[],
            core_axis_name=("c", "s"),
            dimension_semantics=(pltpu.PARALLEL,),
        )(rows_hbm, idx_hbm, idx_hbm)

    return scatter(rows, idx)


def _collect_rows(rows, idx):
    m = idx.shape[1]
    w = rows.shape[1]

    @pl.kernel(out_type=jax.ShapeDtypeStruct((m, w), rows.dtype), mesh=_sc_mesh(), scratch_types=[])
    def gather(rows_hbm, idx_hbm, out_hbm):
        def body(idx_vmem, out_vmem):
            pltpu.sync_copy(rows_hbm.at[idx_vmem.at[0]], out_vmem)

        pltpu.emit_pipeline(
            body, grid=(m // SC_WINDOW,),
            in_specs=[pl.BlockSpec((1, SC_WINDOW), lambda i: (0, i))],
            out_specs=[pl.BlockSpec((SC_WINDOW, w), lambda i: (i, 0))],
            core_axis_name=("c", "s"),
            dimension_semantics=(pltpu.PARALLEL,),
        )(idx_hbm, out_hbm)

    return gather(rows, idx)


def _moe_ffn_ln(h, route, counts, x, wg, wu, wd, layer, mod, ln, mod_next, alpha):
    bsz, s, d = x.shape
    n = bsz * s
    hp = h.reshape(2, n, d // 4)
    route = route.reshape(n, 8)

    tm = MOE_TILE
    cnt = counts[0, :N_EXPERTS].astype(jnp.int32)
    padded = ((cnt + tm - 1) // tm) * tm
    ends = jnp.cumsum(padded)
    offs = ends - padded
    e = route[:, 0:2].astype(jnp.int32)
    rank = route[:, 2:4].astype(jnp.int32)
    pos = (jnp.take(offs, e) + rank).T
    n_tiles = (2 * n) // tm + N_EXPERTS
    n_slots = n_tiles * tm
    tile_start = jnp.arange(n_tiles, dtype=jnp.int32) * tm
    tile_expert = jnp.minimum(jnp.sum(tile_start[:, None] >= ends[None, :], axis=1),
                              N_EXPERTS - 1).astype(jnp.int32)
    n_used = (ends[-1] // tm).astype(jnp.int32).reshape(1)

    idx = jnp.concatenate([pos, pos + n_slots], axis=1)
    xs = _dispatch_rows(hp.reshape(2 * n, d // 4), idx, 2 * n_slots).reshape(2, n_slots, d // 4)
    ys = _grouped_swiglu(xs, tile_expert, n_used, wg, wu, wd, layer)
    yab = _collect_rows(ys.reshape(2 * n_slots, d // 4), idx.reshape(1, 4 * n)).reshape(2, 2, n, d // 4)
    return _combine_ln(yab, route, x, mod, ln, mod_next, alpha)


def kernel(x, c, w_ada, b_ada, w_in, w_out, hgrn_lb_logits, hgrn_norm_gain,
           w_dense_gate, w_dense_up, w_dense_down, w_router, w_moe_gate, w_moe_up, w_moe_down,
           ln_gain, ln_bias):
    depth = w_in.shape[0]
    bsz, s, d = x.shape
    alpha = (2 * depth) ** 0.25

    mods = _ada_all(c, w_ada, b_ada).reshape(depth, 2, bsz, 3, d)
    lb_all = _lower_bounds(hgrn_lb_logits)
    ln_all = jnp.stack([ln_gain, ln_bias], axis=2)

    h = x
    for layer in range(depth):
        proj_h, proj_s = _in_proj(h, w_in[layer].astype(BF16), mods[0, 0] if layer == 0 else None)
        o_h = _hgrn(proj_h.reshape(4, bsz, s, D_HGRN), lb_all[layer:layer + 1],
                    hgrn_norm_gain[layer:layer + 1])
        o_s = _stick_breaking(proj_s.reshape(-1, bsz, s, LANES))
        mod_next = mods[layer + 1, 0] if layer + 1 < depth else None
        j = layer // 2
        if layer % 2 == 0:
            res = _mix_dense_ln(o_h, o_s, x, w_out[layer].astype(BF16), w_dense_gate[j].astype(BF16),
                                w_dense_up[j].astype(BF16), w_dense_down[j].astype(BF16),
                                mods[layer, 0], mods[layer, 1], ln_all[layer, 0], ln_all[layer, 1],
                                mod_next, alpha)
        else:
            x, h, route, counts = _out_proj_route(o_h, o_s, x, w_out[layer].astype(BF16), mods[layer, 0],
                                                  ln_all[layer, 0], mods[layer, 1], w_router[j], alpha)
            res = _moe_ffn_ln(h, route, counts, x, w_moe_gate, w_moe_up, w_moe_down, j,
                              mods[layer, 1], ln_all[layer, 1], mod_next, alpha)
        if mod_next is not None:
            x, h = res
        else:
            (x,) = res
    return x
```

```python
import functools
import math

import jax
import jax.numpy as jnp
from jax import lax
from jax.experimental import pallas as pl
from jax.experimental.pallas import tpu as pltpu
from jax.experimental.pallas import tpu_sc as plsc

D_MODEL = 1024
D_HGRN = 512
HGRN_HEADS = 4
HGRN_DH = 128
D_SB = 512
SB_HEADS = 8
SB_DH = 64
D_IN_PROJ = 4 * D_HGRN + 3 * D_SB
N_EXPERTS = 8
LN_EPS = 1e-5
RMS_EPS = 1e-6

LOG2E = math.log2(math.e)
LANES = 128
VMEM_LIMIT = 52 * 1024 * 1024

HG_UNIT = 128
HG_BAND = 4
HG_PAD = 8
HG_LEVELS = (64, 32, 16, 8, 4)
SB_BLOCK = 128
SB_SKIP = 110.0 * LOG2E
SB_GROUP = 16
SB_WIN = 3
MOE_TILE = 1024
MOE_FF_TILE = 512
DENSE_SPLIT = 2
SC_WINDOW = 128

F32 = jnp.float32
BF16 = jnp.bfloat16
U32 = jnp.uint32


def _cparams(sem):
    return pltpu.CompilerParams(dimension_semantics=sem, vmem_limit_bytes=VMEM_LIMIT)


def _dot(a, b):
    return jnp.dot(a, b, preferred_element_type=F32)


def _dot_nt(a, b):
    return lax.dot_general(a, b, (((1,), (1,)), ((), ())), preferred_element_type=F32)


def _dot_tn(a, b):
    return lax.dot_general(a, b, (((0,), (0,)), ((), ())), preferred_element_type=F32)


def _pack_rows(v):
    w = v.shape[-1] // 2
    lo = lax.bitcast_convert_type(v[:, :w].astype(BF16).astype(F32), U32)
    hi = lax.bitcast_convert_type(v[:, w:].astype(BF16).astype(F32), U32)
    return (lo >> 16) | hi


def _unpack_rows(p):
    lo = lax.bitcast_convert_type(p << 16, F32)
    hi = lax.bitcast_convert_type(p & jnp.uint32(0xFFFF0000), F32)
    return jnp.concatenate([lo, hi], axis=-1)


def _store_planes(ref, p):
    w = p.shape[-1] // 2
    ref[0] = p[:, :w].reshape(ref.shape[1:])
    ref[1] = p[:, w:].reshape(ref.shape[1:])


def _load_planes(p0, p1):
    return _unpack_rows(jnp.concatenate([p0, p1], axis=-1))


def _sigmoid(x):
    return 1.0 / (1.0 + jnp.exp(-x))


def _silu(x):
    hx = 0.5 * x
    return hx + hx * jnp.tanh(hx)


def _ada_kernel(c_ref, w_ref, b_ref, o_ref):
    sc = _silu(c_ref[...])
    o_ref[0] = jnp.dot(sc, w_ref[0], preferred_element_type=F32,
                       precision=lax.Precision.HIGHEST) + b_ref[0]


def _ada_all(c, w_ada, b_ada):
    n_sub = w_ada.shape[0] * w_ada.shape[1]
    bsz, d = c.shape
    w = w_ada.reshape(n_sub, d, 3 * d)
    b = b_ada.reshape(n_sub, 1, 3 * d)
    tn = 1024
    return pl.pallas_call(
        _ada_kernel,
        grid=(n_sub, 3 * d // tn),
        in_specs=[pl.BlockSpec((bsz, d), lambda s, j: (0, 0)),
                  pl.BlockSpec((1, d, tn), lambda s, j: (s, 0, j)),
                  pl.BlockSpec((1, 1, tn), lambda s, j: (s, 0, j))],
        out_specs=pl.BlockSpec((1, bsz, tn), lambda s, j: (s, 0, j)),
        out_shape=jax.ShapeDtypeStruct((n_sub, bsz, 3 * d), F32),
        compiler_params=_cparams(("parallel", "parallel")),
        name="ada_modulation",
    )(c, w, b)


def _lb_kernel(l_ref, o_ref):
    x = l_ref[...]
    m = jnp.max(x, axis=0, keepdims=True)
    e = jnp.exp(x - m)
    p = e / jnp.sum(e, axis=0, keepdims=True)
    depth = x.shape[0]
    run = jnp.zeros_like(p[0:1])
    rows = []
    for i in range(depth):
        run = run + p[i:i + 1]
        rows.append(run - p[0:1])
    o_ref[...] = jnp.concatenate(rows, axis=0)


def _lower_bounds(lb_logits):
    return pl.pallas_call(
        _lb_kernel,
        out_shape=jax.ShapeDtypeStruct(lb_logits.shape, F32),
        name="hgrn_lower_bounds",
    )(lb_logits)


def _inproj_kernel(modulate, h_ref, *rest):
    if modulate:
        m_ref, w_ref, oh_ref, os_ref = rest
        m = m_ref[0]
        h = (h_ref[...] * (1.0 + m[1:2]) + m[0:1]).astype(BF16)
    else:
        w_ref, oh_ref, os_ref = rest
        h = h_ref[...]
    tn = 512
    nh = oh_ref.shape[0]
    for j in range(nh):
        oh_ref[j] = _dot(h, w_ref[:, j * tn:(j + 1) * tn]).astype(BF16)
    per = tn // LANES
    for j in range(os_ref.shape[0] // per):
        blk = _dot(h, w_ref[:, (nh + j) * tn:(nh + j + 1) * tn]).astype(BF16)
        for c in range(per):
            os_ref[j * per + c] = blk[:, c * LANES:(c + 1) * LANES]


def _in_proj(h, w_in, mod=None):
    bsz, s, d = h.shape
    n = bsz * s
    dout = w_in.shape[1]
    dh = 4 * D_HGRN
    n_slabs = (dout - dh) // LANES
    tm = min(s, 1024)
    per = s // tm
    in_specs = [pl.BlockSpec((tm, d), lambda b, i: (b * per + i, 0))]
    args = [h.reshape(n, d)]
    if mod is not None:
        in_specs.append(pl.BlockSpec((1, 3, d), lambda b, i: (b, 0, 0)))
        args.append(mod)
    in_specs.append(pl.BlockSpec((d, dout), lambda b, i: (0, 0), pipeline_mode=pl.Buffered(1)))
    args.append(w_in)
    return pl.pallas_call(
        functools.partial(_inproj_kernel, mod is not None),
        grid=(bsz, per),
        in_specs=in_specs,
        out_specs=[pl.BlockSpec((4, tm, D_HGRN), lambda b, i: (0, b * per + i, 0)),
                   pl.BlockSpec((n_slabs, tm, LANES), lambda b, i: (0, b * per + i, 0))],
        out_shape=[jax.ShapeDtypeStruct((4, n, D_HGRN), BF16),
                   jax.ShapeDtypeStruct((n_slabs, n, LANES), BF16)],
        compiler_params=_cparams(("parallel", "parallel")),
        name="in_proj",
    )(*args)


def _hgrn_consts():
    t = jnp.arange(HG_UNIT)[:, None]
    s = jnp.arange(HG_UNIT)[None, :]
    tril = (s <= t).astype(BF16)
    lv = jnp.full((HG_UNIT, HG_UNIT), -1, jnp.int32)
    for idx, m in enumerate(HG_LEVELS):
        blk = 2 * m
        hit = (t // blk == s // blk) & (t % blk >= m) & (s % blk < m)
        lv = jnp.where(hit, idx, lv)
    return tril, lv


def _hgrn_kernel(q_ref, f_ref, i_ref, g_ref, lb_ref, gain_ref, tril_ref, lv_ref, o_ref,
                 states_ref, kpads_ref, fpads_ref, vpads_ref):
    @pl.when(pl.program_id(1) == 0)
    def _():
        states_ref[...] = jnp.zeros_like(states_ref)

    heads = range(HGRN_HEADS)
    zpad = jnp.zeros((HGRN_HEADS, HG_PAD, HGRN_DH), F32)
    kpads_ref[:, 0:HG_PAD, :] = zpad
    fpads_ref[:, 0:HG_PAD, :] = zpad
    vpads_ref[:, 0:HG_PAD, :] = zpad

    n_units = q_ref.shape[2] // HG_UNIT
    rowi = lax.broadcasted_iota(jnp.int32, (HG_UNIT, HGRN_DH), 0)
    tmod = jnp.bitwise_and(rowi, HG_BAND - 1)

    def unit(u, carry):
        rows = pl.ds(pl.multiple_of(u * HG_UNIT, HG_UNIT), HG_UNIT)
        lanes = [slice(h * HGRN_DH, (h + 1) * HGRN_DH) for h in heads]

        fs, kk, qq, v, vb, b3 = [], [], [], [], [], []
        for h in heads:
            lb = lb_ref[:, lanes[h]]
            f = lb + (1.0 - lb) * _sigmoid(f_ref[0, 0, rows, lanes[h]].astype(F32))
            logf = jnp.log2(f)
            fs.append(f)
            kk.append(1.0 - f)
            qq.append(_silu(q_ref[0, 0, rows, lanes[h]].astype(F32)))
            v.append(i_ref[0, 0, rows, lanes[h]].astype(F32))
            vb.append(i_ref[0, 0, rows, lanes[h]])
            p0 = logf.astype(BF16)
            r1 = logf - p0.astype(F32)
            p1 = r1.astype(BF16)
            p2 = (r1 - p1.astype(F32)).astype(BF16)
            b3.append(_dot(tril_ref[...], jnp.concatenate([p0, p1, p2], axis=1)))
        b = [t[:, 0:HGRN_DH] + t[:, HGRN_DH:2 * HGRN_DH] + t[:, 2 * HGRN_DH:] for t in b3]

        state = [states_ref[h] for h in heads]
        o = [_dot_nt((qq[h] * jnp.exp2(b[h])).astype(BF16), state[h].astype(BF16)) for h in heads]

        lv = lv_ref[...]
        scores = [jnp.zeros((HG_UNIT, HG_UNIT), F32) for _ in heads]
        for idx, m in enumerate(HG_LEVELS):
            blk = 2 * m
            nb = HG_UNIT // blk
            upper = jnp.bitwise_and(rowi, m) != 0
            xl = []
            for h in heads:
                ref = b[h].reshape(nb, blk, HGRN_DH)[:, m - 1:m, :]
                ref = jnp.broadcast_to(ref, (nb, blk, HGRN_DH)).reshape(HG_UNIT, HGRN_DH)
                xl.append((jnp.where(upper, qq[h], kk[h]) * jnp.exp2(-jnp.abs(b[h] - ref))).astype(BF16))
            s_l = [_dot_nt(x, x) for x in xl]
            scores = [jnp.where(lv == idx, s_l[h], scores[h]) for h in heads]
        intra = [_dot(scores[h].astype(BF16), vb[h]) for h in heads]

        for h in heads:
            kpads_ref[h, HG_PAD:, :] = kk[h]
            fpads_ref[h, HG_PAD:, :] = fs[h]
            vpads_ref[h, HG_PAD:, :] = v[h]
        band = [jnp.sum(qq[h] * kk[h], axis=-1, keepdims=True) * v[h] for h in heads]
        dec = [None for _ in heads]
        for d in range(1, HG_BAND):
            win = pl.ds(HG_PAD - d, HG_UNIT)
            for h in heads:
                dec[h] = fs[h] if d == 1 else dec[h] * fpads_ref[h, pl.ds(HG_PAD - (d - 1), HG_UNIT), :]
                p = qq[h] * kpads_ref[h, win, :] * dec[h]
                s_d = jnp.sum(p, axis=-1, keepdims=True)
                band[h] = band[h] + jnp.where(tmod >= d, s_d * vpads_ref[h, win, :], 0.0)

        for h in heads:
            b_last = b[h][HG_UNIT - 1:HG_UNIT, :]
            kdec = kk[h] * jnp.exp2(b_last - b[h])
            states_ref[h] = state[h] * jnp.exp2(b_last) + _dot_tn(vb[h], kdec.astype(BF16))
        for h in heads:
            oh = o[h] + intra[h] + band[h]
            oh = oh * lax.rsqrt(jnp.mean(oh * oh, axis=-1, keepdims=True) + RMS_EPS)
            gg = g_ref[0, 0, rows, lanes[h]].astype(F32)
            o_ref[0, rows, lanes[h]] = (oh * gain_ref[:, lanes[h]] * _silu(gg)).astype(BF16)
        return carry

    lax.fori_loop(0, n_units, unit, 0)


def _hgrn(proj, lb, gain):
    _, bsz, s, _ = proj.shape
    ts = min(s, 1024)
    tril, lv = _hgrn_consts()

    def col(off):
        return pl.BlockSpec((1, 1, ts, D_HGRN), lambda b, i, off=off: (off, b, i, 0))

    return pl.pallas_call(
        _hgrn_kernel,
        grid=(bsz, s // ts),
        in_specs=[col(0), col(1), col(2), col(3),
                  pl.BlockSpec((1, D_HGRN), lambda b, i: (0, 0)),
                  pl.BlockSpec((1, D_HGRN), lambda b, i: (0, 0)),
                  pl.BlockSpec((HG_UNIT, HG_UNIT), lambda b, i: (0, 0)),
                  pl.BlockSpec((HG_UNIT, HG_UNIT), lambda b, i: (0, 0))],
        out_specs=pl.BlockSpec((1, ts, D_HGRN), lambda b, i: (b, i, 0)),
        out_shape=jax.ShapeDtypeStruct((bsz, s, D_HGRN), BF16),
        scratch_shapes=[pltpu.VMEM((HGRN_HEADS, HGRN_DH, HGRN_DH), F32),
                        pltpu.VMEM((HGRN_HEADS, HG_UNIT + HG_PAD, HGRN_DH), F32),
                        pltpu.VMEM((HGRN_HEADS, HG_UNIT + HG_PAD, HGRN_DH), F32),
                        pltpu.VMEM((HGRN_HEADS, HG_UNIT + HG_PAD, HGRN_DH), F32)],
        compiler_params=_cparams(("parallel", "arbitrary")),
        name="hgrn2",
    )(proj, proj, proj, proj, lb, gain, tril, lv)


def _sb_kernel(q_ref, k_ref, v_ref, uo_ref, o_ref, kp_ref, vp_ref, c_ref, acc_ref, more_ref):
    tq = SB_BLOCK
    nq = q_ref.shape[2] // tq
    pad = (SB_WIN - 1) * tq
    kp_ref[0:pad, :] = jnp.zeros((pad, LANES), BF16)
    vp_ref[0:pad, :] = jnp.zeros((pad, LANES), BF16)
    kp_ref[pad:, :] = (k_ref[0, 0].astype(F32) * LOG2E).astype(BF16)
    vp_ref[pad:, :] = v_ref[0, 0]

    first = lax.broadcasted_iota(jnp.int32, (tq, LANES), 1) < SB_DH
    rowi = lax.broadcasted_iota(jnp.int32, (2 * tq, tq), 0)
    coli = lax.broadcasted_iota(jnp.int32, (2 * tq, tq), 1)
    strict = coli < jnp.bitwise_and(rowi, tq - 1)

    def stacked_q(qi):
        rows = pl.ds(pl.multiple_of(qi * tq, tq), tq)
        q = q_ref[0, 0, rows, :] * (SB_DH ** -0.5)
        zero = jnp.zeros_like(q)
        return rows, jnp.concatenate([jnp.where(first, q, zero), jnp.where(first, zero, q)], axis=0)

    def windows(items, diag, win):
        order = list(reversed(range(win)))
        zs, cts = [], []
        for q2, p_top, _, _ in items:
            start = pl.multiple_of((p_top - (win - 1)) * tq, tq)
            zs.append(_dot_nt(q2, kp_ref[pl.ds(start, win * tq), :]))
        for z in zs:
            logn = -(jnp.maximum(z, 0.0) + jnp.log2(1.0 + jnp.exp2(-jnp.abs(z))))
            lhs = []
            for j in order:
                lj = logn[:, j * tq:(j + 1) * tq]
                if diag and j == win - 1:
                    lj = jnp.where(strict, lj, 0.0)
                hi = lj.astype(BF16)
                lo = (lj - hi.astype(F32)).astype(BF16)
                lhs.append(jnp.concatenate([hi, lo], axis=1))
            cts.append(_dot(jnp.concatenate(lhs, axis=0), uo_ref[...]))
        outs = []
        for (q2, p_top, crow, acc), z, ct in zip(items, zs, cts):
            start = pl.multiple_of((p_top - (win - 1)) * tq, tq)
            parts = [None] * win
            for idx, j in enumerate(order):
                blk = ct[idx * 2 * tq:(idx + 1) * 2 * tq]
                a = jnp.exp2(z[:, j * tq:(j + 1) * tq] + (crow + blk))
                if diag and j == win - 1:
                    a = jnp.where(strict, a, 0.0)
                parts[j] = a.astype(BF16)
                crow = crow + jnp.broadcast_to(blk[:, 0:1], blk.shape)
            acc = acc + _dot(jnp.concatenate(parts, axis=1), vp_ref[pl.ds(start, win * tq), :])
            outs.append((crow, acc))
        return outs

    def emit(rows, acc):
        o_ref[0, 0, rows, :] = jnp.where(first, acc[0:tq], acc[tq:]).astype(BF16)

    def fast(g, carry):
        zero = jnp.zeros((2 * tq, LANES), F32)
        qis = [g * SB_GROUP + t for t in range(SB_GROUP)]
        prep = [stacked_q(qi) for qi in qis]
        res = windows([(q2, qi + SB_WIN - 1, zero, zero) for qi, (_, q2) in zip(qis, prep)], True, SB_WIN)
        for qi, (rows, _), (crow, acc) in zip(qis, prep, res):
            emit(rows, acc)
            c_ref[qi] = crow
            acc_ref[qi] = acc
            unfinished = jnp.logical_and(qi >= SB_WIN, jnp.max(crow) > -SB_SKIP)
            more_ref[qi] = unfinished.astype(jnp.int32)
        return carry

    lax.fori_loop(0, nq // SB_GROUP, fast, 0)

    def finish(qi, carry):
        @pl.when(more_ref[qi] != 0)
        def _():
            rows, q2 = stacked_q(qi)

            def cond(c):
                p_top, mx, _, _ = c
                return jnp.logical_and(p_top >= SB_WIN - 1, mx > -SB_SKIP)

            def body(c):
                p_top, _, crow, acc = c
                (crow, acc), = windows([(q2, p_top, crow, acc)], False, 1)
                return p_top - 1, jnp.max(crow), crow, acc

            crow0 = c_ref[qi]
            _, _, _, acc = lax.while_loop(cond, body, (qi - 1, jnp.max(crow0), crow0, acc_ref[qi]))
            emit(rows, acc)
        return carry

    lax.fori_loop(SB_WIN, nq, finish, 0)


def _stick_breaking(qkv):
    _, bsz, s, _ = qkv.shape
    tq = SB_BLOCK
    nq = s // tq
    assert nq % SB_GROUP == 0
    n_pairs = D_SB // LANES
    jj = jnp.arange(tq)
    suffix = (jj[:, None] >= jj[None, :]).astype(BF16)
    uo = jnp.concatenate([suffix, suffix], axis=0)

    def seq(off):
        return pl.BlockSpec((1, 1, s, LANES), lambda b, p, off=off: (off * n_pairs + p, b, 0, 0))

    return pl.pallas_call(
        _sb_kernel,
        grid=(bsz, n_pairs),
        in_specs=[seq(0), seq(1), seq(2),
                  pl.BlockSpec((2 * tq, tq), lambda b, p: (0, 0))],
        out_specs=pl.BlockSpec((1, 1, s, LANES), lambda b, p: (p, b, 0, 0)),
        out_shape=jax.ShapeDtypeStruct((n_pairs, bsz, s, LANES), BF16),
        scratch_shapes=[pltpu.VMEM((s + (SB_WIN - 1) * tq, LANES), BF16),
                        pltpu.VMEM((s + (SB_WIN - 1) * tq, LANES), BF16),
                        pltpu.VMEM((nq, 2 * tq, LANES), F32),
                        pltpu.VMEM((nq, 2 * tq, LANES), F32),
                        pltpu.SMEM((nq,), jnp.int32)],
        compiler_params=_cparams(("parallel", "parallel")),
        name="stick_breaking",
    )(qkv, qkv, qkv, uo)


def _residual_ln(x, y, gate, gain, bias, alpha):
    r = alpha * x + gate * y
    mu = jnp.mean(r, axis=-1, keepdims=True)
    rc = r - mu
    var = jnp.mean(rc * rc, axis=-1, keepdims=True)
    return rc * lax.rsqrt(var + LN_EPS) * gain + bias


def _sb_lanes(os_ref):
    return jnp.concatenate([os_ref[p, 0] for p in range(os_ref.shape[0])], axis=-1)


def _emit(xn, outs, mod_next_ref):
    outs[0][0] = xn
    if mod_next_ref is not None:
        mn = mod_next_ref[0]
        outs[1][0] = (xn * (1.0 + mn[1:2]) + mn[0:1]).astype(BF16)


def _row_specs(bsz, s, d, tm, mod_next):
    x_spec = pl.BlockSpec((1, tm, d), lambda b, i: (b, i, 0))
    mod_spec = pl.BlockSpec((1, 3, d), lambda b, i: (b, 0, 0))
    ln_spec = pl.BlockSpec((2, d), lambda b, i: (0, 0))
    out_specs = [pl.BlockSpec((1, tm, d), lambda b, i: (b, i, 0))]
    out_shape = [jax.ShapeDtypeStruct((bsz, s, d), F32)]
    if mod_next is not None:
        out_specs.append(pl.BlockSpec((1, tm, d), lambda b, i: (b, i, 0)))
        out_shape.append(jax.ShapeDtypeStruct((bsz, s, d), BF16))
    return x_spec, mod_spec, ln_spec, out_specs, out_shape


def _mix_dense_kernel(alpha, has_next, n_chunks, oh_ref, os_ref, x_ref, wo_ref, wg_ref, wu_ref, wd_ref,
                      mod0_ref, mod1_ref, ln0_ref, ln1_ref, *rest):
    mod_next_ref = rest[0] if has_next else None
    outs = rest[1:] if has_next else rest
    tm = x_ref.shape[1]
    parts = [slice(r * tm // DENSE_SPLIT, (r + 1) * tm // DENSE_SPLIT) for r in range(DENSE_SPLIT)]
    ln0 = ln0_ref[...]
    ln1 = ln1_ref[...]
    m1 = mod1_ref[0]
    ff = wg_ref.shape[1]
    tf = ff // n_chunks
    osb = [jnp.concatenate([os_ref[p, 0, r, :] for p in range(os_ref.shape[0])], axis=-1) for r in parts]
    ys = [_dot(oh_ref[0, r, :], wo_ref[0:D_HGRN, :]) + _dot(o, wo_ref[D_HGRN:, :]) for r, o in zip(parts, osb)]
    x1s = [_residual_ln(x_ref[0, r, :], y, mod0_ref[0][2:3], ln0[0:1], ln0[1:2], alpha)
           for r, y in zip(parts, ys)]
    hs = [(x1 * (1.0 + m1[1:2]) + m1[0:1]).astype(BF16) for x1 in x1s]
    y2 = [None] * len(parts)
    for j in range(n_chunks):
        cols = slice(j * tf, (j + 1) * tf)
        gs = [_dot(h, wg_ref[:, cols]) for h in hs]
        us = [_dot(h, wu_ref[:, cols]) for h in hs]
        ps = [_dot((_silu(g) * u).astype(BF16), wd_ref[cols, :]) for g, u in zip(gs, us)]
        y2 = [p if y is None else y + p for y, p in zip(y2, ps)]
    for r, x1, y in zip(parts, x1s, y2):
        x2 = _residual_ln(x1, y, m1[2:3], ln1[0:1], ln1[1:2], alpha)
        outs[0][0, r, :] = x2
        if mod_next_ref is not None:
            mn = mod_next_ref[0]
            outs[1][0, r, :] = (x2 * (1.0 + mn[1:2]) + mn[0:1]).astype(BF16)


def _mix_dense_ln(o_h, o_s, x, w_out, wg, wu, wd, mod0, mod1, ln0, ln1, mod_next, alpha):
    bsz, s, d = x.shape
    ff = wg.shape[1]
    tm = min(s, 512)
    x_spec, mod_spec, ln_spec, out_specs, out_shape = _row_specs(bsz, s, d, tm, mod_next)
    once = pl.Buffered(1)
    in_specs = [pl.BlockSpec((1, tm, D_HGRN), lambda b, i: (b, i, 0)),
                pl.BlockSpec((D_SB // LANES, 1, tm, LANES), lambda b, i: (0, b, i, 0)),
                x_spec,
                pl.BlockSpec((d, d), lambda b, i: (0, 0), pipeline_mode=once),
                pl.BlockSpec((d, ff), lambda b, i: (0, 0), pipeline_mode=once),
                pl.BlockSpec((d, ff), lambda b, i: (0, 0), pipeline_mode=once),
                pl.BlockSpec((ff, d), lambda b, i: (0, 0), pipeline_mode=once),
                mod_spec, mod_spec, ln_spec, ln_spec]
    args = [o_h, o_s, x, w_out, wg, wu, wd, mod0, mod1, ln0, ln1]
    if mod_next is not None:
        in_specs.append(mod_spec)
        args.append(mod_next)
    n_chunks = 1
    return pl.pallas_call(
        functools.partial(_mix_dense_kernel, alpha, mod_next is not None, n_chunks),
        grid=(bsz, s // tm),
        in_specs=in_specs, out_specs=out_specs, out_shape=out_shape,
        compiler_params=_cparams(("parallel", "parallel")),
        name="out_proj_dense_ffn_ln",
    )(*args)


def _route(h, w_ref, tri_ref, run_ref):
    tm = h.shape[0]
    lane = lax.broadcasted_iota(jnp.int32, (tm, LANES), 1)
    logits = _dot(h, w_ref[...])
    neg = jnp.float32(-jnp.inf)
    logits = jnp.where(lane < N_EXPERTS, logits, neg)
    m1 = jnp.max(logits, axis=-1, keepdims=True)
    i1 = jnp.min(jnp.where(logits == m1, lane, LANES), axis=-1, keepdims=True)
    rest = jnp.where(lane == i1, neg, logits)
    m2 = jnp.max(rest, axis=-1, keepdims=True)
    i2 = jnp.min(jnp.where(rest == m2, lane, LANES), axis=-1, keepdims=True)
    e21 = jnp.exp(m2 - m1)
    w1 = 1.0 / (1.0 + e21)
    w2 = e21 / (1.0 + e21)

    hot1 = lane == i1
    hot2 = lane == i2
    both = jnp.where(jnp.logical_or(hot1, hot2), 1.0, 0.0)
    before = _dot(tri_ref[...], both.astype(BF16)) + run_ref[...]
    r1 = jnp.sum(jnp.where(hot1, before, 0.0), axis=-1, keepdims=True)
    r2 = jnp.sum(jnp.where(hot2, before, 0.0), axis=-1, keepdims=True)
    run_ref[...] = run_ref[...] + jnp.sum(both, axis=0, keepdims=True)

    cols = [i1.astype(F32), i2.astype(F32), r1, r2, w1, w2]
    route = jnp.zeros((tm, LANES), F32)
    for c, val in enumerate(cols):
        route = jnp.where(lane == c, val, route)
    return route[:, 0:8]


def _outproj_route_kernel(alpha, oh_ref, os_ref, x_ref, w_ref, mod_ref, ln_ref, modn_ref, wr_ref, tri_ref,
                          x_out, hp_out, route_out, count_out, run_ref):
    @pl.when(jnp.logical_and(pl.program_id(0) == 0, pl.program_id(1) == 0))
    def _():
        run_ref[...] = jnp.zeros_like(run_ref)

    y = _dot(oh_ref[0], w_ref[0:D_HGRN, :]) + _dot(_sb_lanes(os_ref), w_ref[D_HGRN:, :])
    ln = ln_ref[...]
    xn = _residual_ln(x_ref[0], y, mod_ref[0][2:3], ln[0:1], ln[1:2], alpha)
    x_out[0] = xn
    mn = modn_ref[0]
    hn = xn * (1.0 + mn[1:2]) + mn[0:1]
    _store_planes(hp_out, _pack_rows(hn))
    route_out[0] = _route(hn.astype(BF16), wr_ref, tri_ref, run_ref)
    count_out[...] = run_ref[...]


def _out_proj_route(o_h, o_s, x, w_out, mod, ln, mod_next, w_router, alpha):
    bsz, s, d = x.shape
    tm = min(s, 512)
    w_pad = jnp.zeros((d, LANES), BF16).at[:, :N_EXPERTS].set(w_router.astype(BF16))
    jj = jnp.arange(tm)
    tri = (jj[None, :] < jj[:, None]).astype(BF16)
    row = lambda w: pl.BlockSpec((1, tm, w), lambda b, i: (b, i, 0))
    mod_spec = pl.BlockSpec((1, 3, d), lambda b, i: (b, 0, 0))
    return pl.pallas_call(
        functools.partial(_outproj_route_kernel, alpha),
        grid=(bsz, s // tm),
        in_specs=[row(D_HGRN),
                  pl.BlockSpec((D_SB // LANES, 1, tm, LANES), lambda b, i: (0, b, i, 0)),
                  row(d),
                  pl.BlockSpec((d, d), lambda b, i: (0, 0), pipeline_mode=pl.Buffered(1)),
                  mod_spec,
                  pl.BlockSpec((2, d), lambda b, i: (0, 0)),
                  mod_spec,
                  pl.BlockSpec((d, LANES), lambda b, i: (0, 0)),
                  pl.BlockSpec((tm, tm), lambda b, i: (0, 0))],
        out_specs=[row(d),
                   pl.BlockSpec((2, 1, tm, d // 4), lambda b, i: (0, b, i, 0)),
                   row(8),
                   pl.BlockSpec((1, LANES), lambda b, i: (0, 0))],
        out_shape=[jax.ShapeDtypeStruct((bsz, s, d), F32),
                   jax.ShapeDtypeStruct((2, bsz, s, d // 4), U32),
                   jax.ShapeDtypeStruct((bsz, s, 8), F32),
                   jax.ShapeDtypeStruct((1, LANES), F32)],
        scratch_shapes=[pltpu.VMEM((1, LANES), F32)],
        compiler_params=_cparams(("arbitrary", "arbitrary")),
        name="out_proj_ln_route",
    )(o_h, o_s, x, w_out, mod, ln, mod_next, w_pad, tri)


def _gmm_kernel(te_ref, nt_ref, x_ref, wg_ref, wu_ref, wd_ref, o_ref, acc_ref, xb_ref):
    i = pl.program_id(0)
    j = pl.program_id(1)
    nj = pl.num_programs(1)
    live = i < nt_ref[0]

    @pl.when(jnp.logical_and(live, j == 0))
    def _():
        acc_ref[...] = jnp.zeros_like(acc_ref)
        xb_ref[...] = _load_planes(x_ref[0], x_ref[1]).astype(BF16)

    @pl.when(live)
    def _():
        x = xb_ref[...]
        g = _dot(x, wg_ref[0, 0].astype(BF16))
        u = _dot(x, wu_ref[0, 0].astype(BF16))
        acc_ref[...] += _dot((_silu(g) * u).astype(BF16), wd_ref[0, 0].astype(BF16))

    @pl.when(jnp.logical_and(live, j == nj - 1))
    def _():
        _store_planes(o_ref, _pack_rows(acc_ref[...]))

    @pl.when(jnp.logical_and(jnp.logical_not(live), j == nj - 1))
    def _():
        o_ref[...] = jnp.zeros_like(o_ref)


def _grouped_swiglu(xs, tile_expert, n_tiles_used, wg, wu, wd, layer):
    _, n_slots, pw = xs.shape
    d = 4 * pw
    ff = wg.shape[3]
    tm, tf = MOE_TILE, MOE_FF_TILE
    n_tiles = n_slots // tm

    def row_map(i, j, te, nt):
        return (0, jnp.minimum(i, nt[0] - 1), 0)

    def ff_block(i, j, te, nt):
        return jnp.where(i < nt[0], j, ff // tf - 1)

    grid_spec = pltpu.PrefetchScalarGridSpec(
        num_scalar_prefetch=2,
        grid=(n_tiles, ff // tf),
        in_specs=[pl.BlockSpec((2, tm, pw), row_map),
                  pl.BlockSpec((1, 1, d, tf), lambda i, j, te, nt: (layer, te[i], 0, ff_block(i, j, te, nt))),
                  pl.BlockSpec((1, 1, d, tf), lambda i, j, te, nt: (layer, te[i], 0, ff_block(i, j, te, nt))),
                  pl.BlockSpec((1, 1, tf, d), lambda i, j, te, nt: (layer, te[i], ff_block(i, j, te, nt), 0))],
        out_specs=pl.BlockSpec((2, tm, pw), lambda i, j, te, nt: (0, i, 0)),
        scratch_shapes=[pltpu.VMEM((tm, d), F32), pltpu.VMEM((tm, d), BF16)],
    )
    return pl.pallas_call(
        _gmm_kernel,
        grid_spec=grid_spec,
        out_shape=jax.ShapeDtypeStruct((2, n_slots, pw), U32),
        compiler_params=_cparams(("arbitrary", "arbitrary")),
        name="moe_grouped_swiglu",
    )(tile_expert, n_tiles_used, xs, wg, wu, wd)


def _combine_kernel(alpha, has_next, ya_ref, yb_ref, r_ref, x_ref, mod_ref, ln_ref, *rest):
    mod_next_ref = rest[0] if has_next else None
    outs = rest[1:] if has_next else rest
    r = r_ref[0]
    y = (r[:, 4:5] * _load_planes(ya_ref[0, 0, 0], ya_ref[0, 1, 0])
         + r[:, 5:6] * _load_planes(yb_ref[0, 0, 0], yb_ref[0, 1, 0]))
    ln = ln_ref[...]
    xn = _residual_ln(x_ref[0], y, mod_ref[0][2:3], ln[0:1], ln[1:2], alpha)
    _emit(xn, outs, mod_next_ref)


def _combine_ln(yab, route, x, mod, ln, mod_next, alpha):
    bsz, s, d = x.shape
    tm = min(s, 1024)
    x_spec, mod_spec, ln_spec, out_specs, out_shape = _row_specs(bsz, s, d, tm, mod_next)
    yab = yab.reshape(2, 2, bsz, s, d // 4)
    in_specs = [pl.BlockSpec((1, 2, 1, tm, d // 4), lambda b, i: (0, 0, b, i, 0)),
                pl.BlockSpec((1, 2, 1, tm, d // 4), lambda b, i: (1, 0, b, i, 0)),
                pl.BlockSpec((1, tm, 8), lambda b, i: (b, i, 0)),
                x_spec, mod_spec, ln_spec]
    args = [yab, yab, route.reshape(bsz, s, 8), x, mod, ln]
    if mod_next is not None:
        in_specs.append(mod_spec)
        args.append(mod_next)
    return pl.pallas_call(
        functools.partial(_combine_kernel, alpha, mod_next is not None),
        grid=(bsz, s // tm),
        in_specs=in_specs, out_specs=out_specs, out_shape=out_shape,
        compiler_params=_cparams(("parallel", "parallel")),
        name="moe_combine_ln",
    )(*args)


def _sc_mesh():
    return plsc.VectorSubcoreMesh(core_axis_name="c", subcore_axis_name="s")


def _dispatch_rows(rows, idx, n_out):
    m, w = rows.shape

    @pl.kernel(out_type=jax.ShapeDtypeStruct((n_out, w), rows.dtype), mesh=_sc_mesh(), scratch_types=[])
    def scatter(rows_hbm, idx_hbm, out_hbm):
        def body(rows_vmem, idx0_vmem, idx1_vmem):
            pltpu.sync_copy(rows_vmem, out_hbm.at[idx0_vmem.at[0]])
            pltpu.sync_copy(rows_vmem, out_hbm.at[idx1_vmem.at[0]])

        pltpu.emit_pipeline(
            body, grid=(m // SC_WINDOW,),
            in_specs=[pl.BlockSpec((SC_WINDOW, w), lambda i: (i, 0)),
                      pl.BlockSpec((1, SC_WINDOW), lambda i: (0, i)),
                      pl.BlockSpec((1, SC_WINDOW), lambda i: (1, i))],
            out_specs=[],
            core_axis_name=("c", "s"),
            dimension_semantics=(pltpu.PARALLEL,),
        )(rows_hbm, idx_hbm, idx_hbm)

    return scatter(rows, idx)


def _collect_rows(rows, idx):
    m = idx.shape[1]
    w = rows.shape[1]

    @pl.kernel(out_type=jax.ShapeDtypeStruct((m, w), rows.dtype), mesh=_sc_mesh(), scratch_types=[])
    def gather(rows_hbm, idx_hbm, out_hbm):
        def body(idx_vmem, out_vmem):
            pltpu.sync_copy(rows_hbm.at[idx_vmem.at[0]], out_vmem)

        pltpu.emit_pipeline(
            body, grid=(m // SC_WINDOW,),
            in_specs=[pl.BlockSpec((1, SC_WINDOW), lambda i: (0, i))],
            out_specs=[pl.BlockSpec((SC_WINDOW, w), lambda i: (i, 0))],
            core_axis_name=("c", "s"),
            dimension_semantics=(pltpu.PARALLEL,),
        )(idx_hbm, out_hbm)

    return gather(rows, idx)


def _moe_ffn_ln(h, route, counts, x, wg, wu, wd, layer, mod, ln, mod_next, alpha):
    bsz, s, d = x.shape
    n = bsz * s
    hp = h.reshape(2, n, d // 4)
    route = route.reshape(n, 8)

    tm = MOE_TILE
    cnt = counts[0, :N_EXPERTS].astype(jnp.int32)
    padded = ((cnt + tm - 1) // tm) * tm
    ends = jnp.cumsum(padded)
    offs = ends - padded
    e = route[:, 0:2].astype(jnp.int32)
    rank = route[:, 2:4].astype(jnp.int32)
    pos = (jnp.take(offs, e) + rank).T
    n_tiles = (2 * n) // tm + N_EXPERTS
    n_slots = n_tiles * tm
    tile_start = jnp.arange(n_tiles, dtype=jnp.int32) * tm
    tile_expert = jnp.minimum(jnp.sum(tile_start[:, None] >= ends[None, :], axis=1),
                              N_EXPERTS - 1).astype(jnp.int32)
    n_used = (ends[-1] // tm).astype(jnp.int32).reshape(1)

    idx = jnp.concatenate([pos, pos + n_slots], axis=1)
    xs = _dispatch_rows(hp.reshape(2 * n, d // 4), idx, 2 * n_slots).reshape(2, n_slots, d // 4)
    ys = _grouped_swiglu(xs, tile_expert, n_used, wg, wu, wd, layer)
    yab = _collect_rows(ys.reshape(2 * n_slots, d // 4), idx.reshape(1, 4 * n)).reshape(2, 2, n, d // 4)
    return _combine_ln(yab, route, x, mod, ln, mod_next, alpha)


def kernel(x, c, w_ada, b_ada, w_in, w_out, hgrn_lb_logits, hgrn_norm_gain,
           w_dense_gate, w_dense_up, w_dense_down, w_router, w_moe_gate, w_moe_up, w_moe_down,
           ln_gain, ln_bias):
    depth = w_in.shape[0]
    bsz, s, d = x.shape
    alpha = (2 * depth) ** 0.25

    mods = _ada_all(c, w_ada, b_ada).reshape(depth, 2, bsz, 3, d)
    lb_all = _lower_bounds(hgrn_lb_logits)
    ln_all = jnp.stack([ln_gain, ln_bias], axis=2)

    h = x
    for layer in range(depth):
        proj_h, proj_s = _in_proj(h, w_in[layer].astype(BF16), mods[0, 0] if layer == 0 else None)
        o_h = _hgrn(proj_h.reshape(4, bsz, s, D_HGRN), lb_all[layer:layer + 1],
                    hgrn_norm_gain[layer:layer + 1])
        o_s = _stick_breaking(proj_s.reshape(-1, bsz, s, LANES))
        mod_next = mods[layer + 1, 0] if layer + 1 < depth else None
        j = layer // 2
        if layer % 2 == 0:
            res = _mix_dense_ln(o_h, o_s, x, w_out[layer].astype(BF16), w_dense_gate[j].astype(BF16),
                                w_dense_up[j].astype(BF16), w_dense_down[j].astype(BF16),
                                mods[layer, 0], mods[layer, 1], ln_all[layer, 0], ln_all[layer, 1],
                                mod_next, alpha)
        else:
            x, h, route, counts = _out_proj_route(o_h, o_s, x, w_out[layer].astype(BF16), mods[layer, 0],
                                                  ln_all[layer, 0], mods[layer, 1], w_router[j], alpha)
            res = _moe_ffn_ln(h, route, counts, x, w_moe_gate, w_moe_up, w_moe_down, j,
                              mods[layer, 1], ln_all[layer, 1], mod_next, alpha)
        if mod_next is not None:
            x, h = res
        else:
            (x,) = res
    return x
```
